```python
import jax, jax.numpy as jnp
from jax import lax
import numpy as np

D_MODEL = 1024
BATCH = 4
SEQ = 4096
DEPTH = 4
DEC_BATCH = 128
DEC_SEQ = 1
PAST_LEN = 8192
PAGE_SIZE = 128

N_META = 16
BLOCK = 128
WINDOW = 128
META_PAD = BLOCK - N_META
HEAD_DIM = 64
N_Q_HEADS = 8
N_KV_HEADS = 2
Q_PER_KV = N_Q_HEADS // N_KV_HEADS
ROT_DIM = HEAD_DIM // 4
ROPE_THETA = 500000.0
GLA_HEADS = 4
GLA_DK = 64
GLA_DV = 128
GLA_RANK = 16
GLA_TAU = 16.0
GLA_CHUNK = 64
D_FF = 4 * D_MODEL
ATT_W = N_Q_HEADS * HEAD_DIM
KV_W = N_KV_HEADS * HEAD_DIM
GK_W = GLA_HEADS * GLA_DK
GV_W = GLA_HEADS * GLA_DV
DEEPNORM_ALPHA = (2 * DEPTH) ** 0.25
DEEPNORM_BETA = (8 * DEPTH) ** -0.25
LN_EPS = 1e-5
RMS_EPS = 1e-6

kernel_name = 'hybrid_swa_sink_gla_deepnorm_decoder_step'


def _splits():
    return (ATT_W, KV_W, KV_W, GK_W, GK_W, GV_W, GLA_RANK, GV_W, D_MODEL, D_MODEL)


def layer_norm(x, g, b):
    xf = x.astype(jnp.float32)
    mu = jnp.mean(xf, -1, keepdims=True)
    var = jnp.mean(jnp.square(xf - mu), -1, keepdims=True)
    return ((xf - mu) * lax.rsqrt(var + LN_EPS) * g.astype(jnp.float32) + b.astype(jnp.float32)).astype(x.dtype)


def partial_rope(x, pos):
    half = ROT_DIM // 2
    inv = ROPE_THETA ** (-jnp.arange(half, dtype=jnp.float32) * 2.0 / ROT_DIM)
    ang = pos.astype(jnp.float32)[:, None] * inv[None, :]
    cos = jnp.cos(ang)[None, :, None, :]
    sin = jnp.sin(ang)[None, :, None, :]
    xf = x.astype(jnp.float32)
    x1, x2 = xf[..., :half], xf[..., half:ROT_DIM]
    out = jnp.concatenate([x1 * cos - x2 * sin, x2 * cos + x1 * sin, xf[..., ROT_DIM:]], -1)
    return out.astype(x.dtype)


def mixer_projections(x, pos, w_in, w_a2, b_a):
    B, T, _ = x.shape
    idx = np.cumsum(_splits())[:-1].tolist()
    q, k, v, gq, gk, gv, glr, gr, ga, gb = jnp.split(x @ w_in, idx, axis=-1)
    q = partial_rope(q.reshape(B, T, N_Q_HEADS, HEAD_DIM), pos)
    k = partial_rope(k.reshape(B, T, N_KV_HEADS, HEAD_DIM), pos)
    v = v.reshape(B, T, N_KV_HEADS, HEAD_DIM)
    def to_heads(t, d):
        return t.reshape(B, T, GLA_HEADS, d).astype(jnp.float32).transpose(0, 2, 1, 3)
    log_a = jax.nn.log_sigmoid((glr @ w_a2 + b_a).astype(jnp.float32)) / GLA_TAU
    gla_in = (to_heads(gq, GLA_DK) * GLA_DK ** -0.5, to_heads(gk, GLA_DK), to_heads(gv, GLA_DV), to_heads(log_a, GLA_DK))
    return q, k, v, gla_in, gr, ga, gb


def banded_sink_attention(q, k, v, mask, sink):
    s = jnp.einsum('bnqhgd,bnkhd->bnhgqk', q, k, preferred_element_type=jnp.float32) * HEAD_DIM ** -0.5
    s = jnp.where(mask[None, :, None, None], s, -jnp.inf)
    sk = sink.astype(jnp.float32)[None, None, :, :, None, None]
    m = jnp.maximum(jnp.max(s, -1, keepdims=True), sk)
    p = jnp.exp(s - m)
    p = p / (jnp.sum(p, -1, keepdims=True) + jnp.exp(sk - m))
    return jnp.einsum('bnhgqk,bnkhd->bnqhgd', p.astype(v.dtype), v)


def swa_prompt(q, k, v, sink):
    B, L = q.shape[:2]
    pad = ((0, 0), (META_PAD, 0), (0, 0), (0, 0))
    qp, kp, vp = jnp.pad(q, pad), jnp.pad(k, pad), jnp.pad(v, pad)
    nb = (L + META_PAD) // BLOCK
    qb = qp.reshape(B, nb, BLOCK, N_KV_HEADS, Q_PER_KV, HEAD_DIM)
    def band(t):
        tb = t.reshape(B, nb, BLOCK, N_KV_HEADS, HEAD_DIM)
        prev = jnp.pad(tb[:, :-1], ((0, 0), (1, 0), (0, 0), (0, 0), (0, 0)))
        return jnp.concatenate([prev, tb], axis=2)
    slot_q = jnp.arange(nb)[:, None] * BLOCK + jnp.arange(BLOCK)[None, :]
    slot_k = (jnp.arange(nb)[:, None] - 1) * BLOCK + jnp.arange(2 * BLOCK)[None, :]
    diff = slot_q[:, :, None] - slot_k[:, None, :]
    mask = (diff >= 0) & (diff < WINDOW) & (slot_k[:, None, :] >= META_PAD)
    o = banded_sink_attention(qb, band(kp), band(vp), mask, sink)
    return o.reshape(B, nb * BLOCK, ATT_W)[:, META_PAD:]


def swa_sample(q, k, v, k_buf, v_buf, sink):
    B, S = q.shape[:2]
    R = k_buf.shape[1]
    kk = jnp.concatenate([k_buf.astype(k.dtype), k], 1)
    vv = jnp.concatenate([v_buf.astype(v.dtype), v], 1)
    pos_q = PAST_LEN + jnp.arange(S)
    pos_k = PAST_LEN - R + jnp.arange(R + S)
    diff = pos_q[:, None] - pos_k[None, :]
    mask = ((diff >= 0) & (diff < WINDOW))[None]
    o = banded_sink_attention(q.reshape(B, 1, S, N_KV_HEADS, Q_PER_KV, HEAD_DIM), kk[:, None], vv[:, None], mask, sink)
    return o.reshape(B, S, ATT_W), kk[:, -R:], vv[:, -R:]


def gla_chunk(S0, q, k, v, log_a):
    C = q.shape[2]
    b = jnp.cumsum(log_a, axis=2)
    causal = jnp.tril(jnp.ones((C, C), dtype=bool))
    rel = jnp.where(causal[None, None, :, :, None], b[:, :, :, None, :] - b[:, :, None, :, :], -jnp.inf)
    A = jnp.einsum('bhtd,bhjd,bhtjd->bhtj', q, k, jnp.exp(rel))
    o = jnp.einsum('bhtd,bhde->bhte', q * jnp.exp(b), S0) + jnp.einsum('bhtj,bhje->bhte', A, v)
    b_last = b[:, :, -1:]
    S1 = jnp.exp(b_last[:, :, 0])[..., None] * S0 + jnp.einsum('bhjd,bhje->bhde', k * jnp.exp(b_last - b), v)
    return S1, o


def gla_prompt(q, k, v, log_a):
    B, H, L = q.shape[:3]
    nc = (L + META_PAD) // GLA_CHUNK
    def chunks(t):
        t = jnp.pad(t, ((0, 0), (0, 0), (META_PAD, 0), (0, 0)))
        return t.reshape(B, H, nc, GLA_CHUNK, t.shape[-1]).transpose(2, 0, 1, 3, 4)
    S0 = jnp.zeros((B, H, GLA_DK, GLA_DV), jnp.float32)
    S_fin, o = lax.scan(lambda S, c: gla_chunk(S, *c), S0, (chunks(q), chunks(k), chunks(v), chunks(log_a)))
    o = o.transpose(1, 2, 0, 3, 4).reshape(B, H, nc * GLA_CHUNK, GLA_DV)[:, :, META_PAD:]
    return o, S_fin


def gla_output(o, gr, g_norm):
    B, H, T, _ = o.shape
    o = o * lax.rsqrt(jnp.mean(o * o, -1, keepdims=True) + RMS_EPS) * g_norm.astype(jnp.float32)
    o = o.transpose(0, 2, 1, 3).reshape(B, T, GV_W)
    return (o * jax.nn.silu(gr.astype(jnp.float32))).astype(gr.dtype)


def finish_layer(x, att, gla, ga, gb, w_pa, w_pb, w_out, ln1_g, ln1_b, w_up, w_down, ln2_g, ln2_b):
    m = jax.nn.sigmoid(ga) * (att @ w_pa) + jax.nn.sigmoid(gb) * (gla @ w_pb)
    h = layer_norm(DEEPNORM_ALPHA * x + m @ w_out, ln1_g, ln1_b)
    f = jnp.square(jax.nn.relu(h @ w_up)) @ w_down
    return layer_norm(DEEPNORM_ALPHA * h + f, ln2_g, ln2_b)


def setup_inputs(seed: int = 0) -> dict:
    key = jax.random.key(seed)
    ks = jax.random.split(key, 22)
    def nrm(k, shape, scale):
        return jax.random.normal(k, shape, jnp.float32) * scale
    win_rows = min(WINDOW, PAST_LEN)
    d_in = sum(_splits())
    return {
        'x_prompt': nrm(ks[0], (BATCH, SEQ, D_MODEL), 1.0),
        'x_sample': nrm(ks[1], (DEC_BATCH, DEC_SEQ, D_MODEL), 1.0),
        'cache_k_win': nrm(ks[2], (DEPTH, DEC_BATCH, win_rows, N_KV_HEADS, HEAD_DIM), 1.0),
        'cache_v_win': nrm(ks[3], (DEPTH, DEC_BATCH, win_rows, N_KV_HEADS, HEAD_DIM), 1.0),
        'state_gla': nrm(ks[4], (DEPTH, DEC_BATCH, GLA_HEADS, GLA_DK, GLA_DV), 1.0),
        'meta_tokens': nrm(ks[5], (N_META, D_MODEL), 1.0),
        'w_in': nrm(ks[6], (DEPTH, D_MODEL, d_in), D_MODEL ** -0.5),
        'w_a2': nrm(ks[7], (DEPTH, GLA_RANK, GK_W), GLA_RANK ** -0.5),
        'b_a': nrm(ks[8], (DEPTH, GK_W), 0.1),
        'attn_sink': nrm(ks[9], (DEPTH, N_Q_HEADS), 0.5),
        'gla_norm_g': 1.0 + nrm(ks[10], (DEPTH, GLA_DV), 0.02),
        'w_proj_a': nrm(ks[11], (DEPTH, ATT_W, D_MODEL), ATT_W ** -0.5),
        'w_proj_b': nrm(ks[12], (DEPTH, GV_W, D_MODEL), GV_W ** -0.5),
        'w_out': nrm(ks[13], (DEPTH, D_MODEL, D_MODEL), DEEPNORM_BETA * D_MODEL ** -0.5),
        'ln1_g': 1.0 + nrm(ks[14], (DEPTH, D_MODEL), 0.02),
        'ln1_b': nrm(ks[15], (DEPTH, D_MODEL), 0.02),
        'w_up': nrm(ks[16], (DEPTH, D_MODEL, D_FF), D_MODEL ** -0.5),
        'w_down': nrm(ks[17], (DEPTH, D_FF, D_MODEL), DEEPNORM_BETA * D_FF ** -0.5),
        'ln2_g': 1.0 + nrm(ks[18], (DEPTH, D_MODEL), 0.02),
        'ln2_b': nrm(ks[19], (DEPTH, D_MODEL), 0.02),
    }


def reference(x_prompt, x_sample, cache_k_win, cache_v_win, state_gla, meta_tokens, w_in, w_a2, b_a, attn_sink,
              gla_norm_g, w_proj_a, w_proj_b, w_out, ln1_g, ln1_b, w_up, w_down, ln2_g, ln2_b):
    B = x_prompt.shape[0]
    meta = jnp.broadcast_to(meta_tokens[None].astype(x_prompt.dtype), (B, N_META, D_MODEL))
    xp = jnp.concatenate([meta, x_prompt], axis=1)
    xs = x_sample
    pos_p = jnp.arange(xp.shape[1])
    pos_s = PAST_LEN + jnp.arange(xs.shape[1])
    pk, pv, pst, sk, sv, sst = [], [], [], [], [], []
    for l in range(DEPTH):
        sink = attn_sink[l].reshape(N_KV_HEADS, Q_PER_KV)
        post = (w_proj_a[l], w_proj_b[l], w_out[l], ln1_g[l], ln1_b[l], w_up[l], w_down[l], ln2_g[l], ln2_b[l])
        q, k, v, g, gr, ga, gb = mixer_projections(xp, pos_p, w_in[l], w_a2[l], b_a[l])
        att = swa_prompt(q, k, v, sink)
        o, S = gla_prompt(*g)
        gla = gla_output(o, gr, gla_norm_g[l])
        pk.append(k[:, -WINDOW:])
        pv.append(v[:, -WINDOW:])
        pst.append(S.astype(xp.dtype))
        xp = finish_layer(xp, att, gla, ga, gb, *post)
        q, k, v, g, gr, ga, gb = mixer_projections(xs, pos_s, w_in[l], w_a2[l], b_a[l])
        att, kb, vb = swa_sample(q, k, v, cache_k_win[l], cache_v_win[l], sink)
        S, o = gla_chunk(state_gla[l].astype(jnp.float32), *g)
        gla = gla_output(o, gr, gla_norm_g[l])
        sk.append(kb)
        sv.append(vb)
        sst.append(S.astype(xs.dtype))
        xs = finish_layer(xs, att, gla, ga, gb, *post)
    y_prompt = xp[:, N_META:]
    y_sample = xs
    return (y_prompt, y_sample, jnp.stack(pk), jnp.stack(pv), jnp.stack(pst), jnp.stack(sk), jnp.stack(sv), jnp.stack(sst))
```

```python
import functools

import jax
import jax.numpy as jnp
from jax import lax
from jax.experimental import pallas as pl
from jax.experimental.pallas import tpu as pltpu

F32 = jnp.float32
BF16 = jnp.bfloat16

D_MODEL = 1024
PAST_LEN = 8192
N_META = 16
BLOCK = 128
META_PAD = BLOCK - N_META
HEAD_DIM = 64
N_Q_HEADS = 8
N_KV_HEADS = 2
Q_PER_KV = N_Q_HEADS // N_KV_HEADS
ROT_DIM = HEAD_DIM // 4
ROPE_THETA = 500000.0
GLA_HEADS = 4
GLA_DK = 64
GLA_DV = 128
GLA_RANK = 16
GLA_TAU = 16.0
D_FF = 4 * D_MODEL
ATT_W = N_Q_HEADS * HEAD_DIM
KV_W = N_KV_HEADS * HEAD_DIM
GK_W = GLA_HEADS * GLA_DK
GV_W = GLA_HEADS * GLA_DV
LN_EPS = 1e-5
RMS_EPS = 1e-6
LANES = 128
SUBLANES = 8
VMEM_LIMIT = 56 * 1024 * 1024

C_Q = 0
C_K = C_Q + ATT_W
C_V = C_K + KV_W
C_GQ = C_V + KV_W
C_GK = C_GQ + GK_W
C_GV = C_GK + GK_W
C_GR = C_GV + GV_W
C_GA = C_GR + GV_W
C_GB = C_GA + D_MODEL
C_LR = C_GB + D_MODEL
W_IN_COLS = C_LR + LANES

GLA_FAST_MAX_DECAY = 40.0


def _sigmoid(x):
    return 1.0 / (1.0 + jnp.exp(-x))


def _layer_norm(y, g, b):
    mu = jnp.mean(y, axis=-1, keepdims=True)
    yc = y - mu
    var = jnp.mean(yc * yc, axis=-1, keepdims=True)
    return yc * lax.rsqrt(var + LN_EPS) * g + b


def _rope(t, cos, sin_lo, sin_hi):
    outs = []
    for j in range(t.shape[1] // LANES):
        tj = t[:, j * LANES:(j + 1) * LANES]
        outs.append(tj * cos + pltpu.roll(tj, LANES - ROT_DIM // 2, 1) * sin_lo
                    + pltpu.roll(tj, ROT_DIM // 2, 1) * sin_hi)
    return outs[0] if len(outs) == 1 else jnp.concatenate(outs, axis=1)


def _log_decay(glr, w_a2, b_a):
    z = jnp.dot(glr.astype(BF16), w_a2, preferred_element_type=F32) + b_a
    return (jnp.minimum(z, 0.0) - jnp.log1p(jnp.exp(-jnp.abs(z)))) * (1.0 / GLA_TAU)


def _row_to_col(row):
    n = row.shape[1]
    eye = lax.broadcasted_iota(jnp.int32, (n, n), 0) == lax.broadcasted_iota(jnp.int32, (n, n), 1)
    return jnp.sum(jnp.where(eye, jnp.broadcast_to(row, (n, n)), 0.0), axis=1, keepdims=True)


def _split3_bf16(a):
    hi = a.astype(BF16)
    r = a - hi.astype(F32)
    mid = r.astype(BF16)
    lo = (r - mid.astype(F32)).astype(BF16)
    return hi, mid, lo


def _gla_gate_out(o, gr, gn):
    outs = []
    for h in range(GLA_HEADS):
        oh = o[:, h * GLA_DV:(h + 1) * GLA_DV]
        ms = jnp.mean(oh * oh, axis=-1, keepdims=True)
        outs.append(oh * lax.rsqrt(ms + RMS_EPS) * gn)
    on = jnp.concatenate(outs, axis=1)
    return on * (gr * _sigmoid(gr))


def _finish(x, att, gla, ga, gb, w_pa, w_pb, w_out, g1, b1, alpha):
    pa = jnp.dot(att.astype(BF16), w_pa, preferred_element_type=F32)
    pb = jnp.dot(gla.astype(BF16), w_pb, preferred_element_type=F32)
    m = _sigmoid(ga) * pa + _sigmoid(gb) * pb
    y = alpha * x + jnp.dot(m.astype(BF16), w_out, preferred_element_type=F32)
    return _layer_norm(y, g1, b1)


def _mixer_prompt_kernel(x_ref, cos_ref, slo_ref, shi_ref, w_in_ref, w_a2_ref, b_a_ref, sink_ref, gn_ref,
                         w_pa_ref, w_pb_ref, w_out_ref, g1_ref, b1_ref,
                         h_ref, kwin_ref, vwin_ref, sfin_ref,
                         kc_scr, vc_scr, s_scr, gq_scr, gk_scr, gv_scr, la_scr, o_scr, att_scr,
                         *, ts, n_steps, alpha):
    s = pl.program_id(1)
    nblk = ts // BLOCK

    @pl.when(s == 0)
    def _():
        kc_scr[0:BLOCK, :] = jnp.zeros((BLOCK, KV_W), BF16)
        vc_scr[0:BLOCK, :] = jnp.zeros((BLOCK, KV_W), BF16)
        s_scr[...] = jnp.zeros(s_scr.shape, F32)

    x = x_ref[0]
    xb = x.astype(BF16)

    def proj(lo, hi):
        return jnp.dot(xb, w_in_ref[:, lo:hi], preferred_element_type=F32)

    cos, slo, shi = cos_ref[...], slo_ref[...], shi_ref[...]
    q = _rope(proj(C_Q, C_K), cos, slo, shi)
    k = _rope(proj(C_K, C_V), cos, slo, shi)
    v = proj(C_V, C_GQ)

    @pl.when(s == n_steps - 1)
    def _():
        kwin_ref[0] = k[ts - BLOCK:, :]
        vwin_ref[0] = v[ts - BLOCK:, :]

    q_bf = (q * (HEAD_DIM ** -0.5)).astype(BF16)
    kc_scr[BLOCK:BLOCK + ts, :] = k.astype(BF16)
    vc_scr[BLOCK:BLOCK + ts, :] = v.astype(BF16)
    qi = lax.broadcasted_iota(jnp.int32, (BLOCK, 2 * BLOCK), 0)
    kj = lax.broadcasted_iota(jnp.int32, (BLOCK, 2 * BLOCK), 1)
    band = (kj - qi >= 1) & (kj - qi <= BLOCK)
    for blk in range(nblk):
        first_key_slot = (s * nblk + blk - 1) * BLOCK
        valid = band & (kj + first_key_slot >= META_PAD)
        r0 = blk * BLOCK
        for kv in range(N_KV_HEADS):
            kcat = kc_scr[r0:r0 + 2 * BLOCK, kv * HEAD_DIM:(kv + 1) * HEAD_DIM]
            vcat = vc_scr[r0:r0 + 2 * BLOCK, kv * HEAD_DIM:(kv + 1) * HEAD_DIM]
            heads = [kv * Q_PER_KV + g for g in range(Q_PER_KV)]
            qs = jnp.concatenate([q_bf[r0:r0 + BLOCK, hq * HEAD_DIM:(hq + 1) * HEAD_DIM] for hq in heads], axis=0)
            sc = lax.dot_general(qs, kcat, (((1,), (1,)), ((), ())), preferred_element_type=F32)
            ps, inv = [], []
            for g, hq in enumerate(heads):
                sg = jnp.where(valid, sc[g * BLOCK:(g + 1) * BLOCK], -jnp.inf)
                sk = sink_ref[hq]
                m = jnp.maximum(jnp.max(sg, axis=-1, keepdims=True), sk)
                p = jnp.exp(sg - m)
                den = jnp.sum(p, axis=-1, keepdims=True) + jnp.exp(sk - m)
                ps.append(p.astype(BF16))
                inv.append(1.0 / den)
            o = jnp.dot(jnp.concatenate(ps, axis=0), vcat, preferred_element_type=F32)
            for g, hq in enumerate(heads):
                att_scr[r0:r0 + BLOCK, hq * HEAD_DIM:(hq + 1) * HEAD_DIM] = o[g * BLOCK:(g + 1) * BLOCK] * inv[g]
    kc_scr[0:BLOCK, :] = kc_scr[ts:ts + BLOCK, :]
    vc_scr[0:BLOCK, :] = vc_scr[ts:ts + BLOCK, :]

    live = (s * ts + lax.broadcasted_iota(jnp.int32, (ts, 1), 0)) >= META_PAD
    gq = proj(C_GQ, C_GK) * (GLA_DK ** -0.5)
    gk = jnp.where(live, proj(C_GK, C_GV), 0.0)
    gv = jnp.where(live, proj(C_GV, C_GR), 0.0)
    la = jnp.where(live, _log_decay(proj(C_LR, W_IN_COLS), w_a2_ref[...], b_a_ref[...]), 0.0)

    tri = (lax.broadcasted_iota(jnp.int32, (BLOCK, BLOCK), 1)
           <= lax.broadcasted_iota(jnp.int32, (BLOCK, BLOCK), 0))
    tri_bf = jnp.where(tri, 1.0, 0.0).astype(BF16)
    cums = []
    worst = jnp.zeros((1, GK_W), F32)
    for c in range(nblk):
        parts = jnp.concatenate(_split3_bf16(la[c * BLOCK:(c + 1) * BLOCK]), axis=1)
        b3 = jnp.dot(tri_bf, parts, preferred_element_type=F32)
        b = b3[:, 0:GK_W] + b3[:, GK_W:2 * GK_W] + b3[:, 2 * GK_W:3 * GK_W]
        cums.append(b)
        worst = jnp.maximum(worst, -b[BLOCK - 1:BLOCK, :])
    fast_ok = jnp.max(worst) <= GLA_FAST_MAX_DECAY

    gq_scr[...] = gq
    gk_scr[...] = gk
    gv_scr[...] = gv
    la_scr[...] = la

    head_of_k = lax.broadcasted_iota(jnp.int32, (BLOCK, GK_W), 1) // GLA_DK
    head_of_v = lax.broadcasted_iota(jnp.int32, (BLOCK, GV_W), 1) // GLA_DV
    state_diag = (lax.broadcasted_iota(jnp.int32, (GK_W, GV_W), 0) // GLA_DK
                  == lax.broadcasted_iota(jnp.int32, (GK_W, GV_W), 1) // GLA_DV)

    def gla_fast():
        t_idx = lax.broadcasted_iota(jnp.int32, (BLOCK, GLA_HEADS * BLOCK), 0)
        j_idx = lax.broadcasted_iota(jnp.int32, (BLOCK, GLA_HEADS * BLOCK), 1) % BLOCK
        causal = j_idx <= t_idx
        for c in range(nblk):
            rows = slice(c * BLOCK, (c + 1) * BLOCK)
            b = cums[c]
            b_last = b[BLOCK - 1:BLOCK, :]
            kc = gk[rows]
            q_dec = (gq[rows] * jnp.exp(b)).astype(BF16)
            k_inv = (kc * jnp.exp(-b)).astype(BF16)
            k_end = kc * jnp.exp(b_last - b)
            vc = gv[rows].astype(BF16)
            zk = jnp.zeros_like(k_inv)
            k_bd = jnp.concatenate([jnp.where(head_of_k == h, k_inv, zk) for h in range(GLA_HEADS)], axis=0)
            a = lax.dot_general(q_dec, k_bd, (((1,), (1,)), ((), ())), preferred_element_type=F32)
            a = jnp.where(causal, a, 0.0).astype(BF16)
            zv = jnp.zeros_like(vc)
            v_bd = jnp.concatenate([jnp.where(head_of_v == h, vc, zv) for h in range(GLA_HEADS)], axis=0)
            s0 = s_scr[...]
            o_scr[rows, :] = (jnp.dot(a, v_bd, preferred_element_type=F32)
                              + jnp.dot(q_dec, s0.astype(BF16), preferred_element_type=F32))
            ds = jnp.dot(k_end.T.astype(BF16), vc, preferred_element_type=F32)
            s_scr[...] = _row_to_col(jnp.exp(b_last)) * s0 + jnp.where(state_diag, ds, 0.0)

    def gla_slow():
        def body(i, carry):
            rows = pl.ds(pl.multiple_of(i * SUBLANES, SUBLANES), SUBLANES)
            la8, k8, q8, v8 = la_scr[rows, :], gk_scr[rows, :], gq_scr[rows, :], gv_scr[rows, :]
            outs = []
            for r in range(SUBLANES):
                a_col = _row_to_col(jnp.exp(la8[r:r + 1]))
                k_col = _row_to_col(k8[r:r + 1])
                q_col = _row_to_col(q8[r:r + 1])
                s1 = a_col * s_scr[...] + jnp.where(state_diag, k_col * v8[r:r + 1], 0.0)
                s_scr[...] = s1
                outs.append(jnp.sum(q_col * s1, axis=0, keepdims=True))
            o_scr[rows, :] = jnp.concatenate(outs, axis=0)
            return carry
        lax.fori_loop(0, ts // SUBLANES, body, 0)

    lax.cond(fast_ok, gla_fast, gla_slow)

    @pl.when(s == n_steps - 1)
    def _():
        for h in range(GLA_HEADS):
            sfin_ref[0, h] = s_scr[h * GLA_DK:(h + 1) * GLA_DK, h * GLA_DV:(h + 1) * GLA_DV]

    gla = _gla_gate_out(o_scr[...], proj(C_GR, C_GA), gn_ref[...])
    h_ref[0] = _finish(x, att_scr[...], gla, proj(C_GA, C_GB), proj(C_GB, C_LR),
                       w_pa_ref[...], w_pb_ref[...], w_out_ref[...], g1_ref[...], b1_ref[...], alpha)


def _ffn_kernel(h_ref, w_up_ref, w_dn_ref, g_ref, b_ref, o_ref, *, alpha, col_chunk):
    h = h_ref[...]
    hb = h.astype(BF16)
    acc = jnp.zeros(h.shape, F32)
    for c in range(D_FF // col_chunk):
        u = jnp.dot(hb, w_up_ref[:, c * col_chunk:(c + 1) * col_chunk], preferred_element_type=F32)
        u = jnp.maximum(u, 0.0)
        acc = acc + jnp.dot((u * u).astype(BF16), w_dn_ref[c * col_chunk:(c + 1) * col_chunk, :],
                            preferred_element_type=F32)
    o_ref[...] = _layer_norm(alpha * h + acc, g_ref[...], b_ref[...])


def _sample_proj_kernel(x_ref, cos_ref, slo_ref, shi_ref, w_in_ref, w_a2_ref, b_a_ref,
                        qkv_ref, gin_ref, gate_ref):
    xb = x_ref[...].astype(BF16)

    def proj(lo, hi):
        return jnp.dot(xb, w_in_ref[:, lo:hi], preferred_element_type=F32)

    cos, slo, shi = cos_ref[...], slo_ref[...], shi_ref[...]
    qkv_ref[:, C_Q:C_K] = _rope(proj(C_Q, C_K), cos, slo, shi)
    qkv_ref[:, C_K:C_V] = _rope(proj(C_K, C_V), cos, slo, shi)
    qkv_ref[:, C_V:C_GQ] = proj(C_V, C_GQ)
    gin_ref[:, 0:GK_W] = proj(C_GQ, C_GK) * (GLA_DK ** -0.5)
    gin_ref[:, GK_W:2 * GK_W + GV_W] = proj(C_GK, C_GR)
    gin_ref[:, 2 * GK_W + GV_W:] = _log_decay(proj(C_LR, W_IN_COLS), w_a2_ref[...], b_a_ref[...])
    gate_ref[...] = proj(C_GR, C_LR)


def _sample_mix_kernel(qkv_ref, gin_ref, ck_ref, cv_ref, st_ref, sink_ref,
                       att_ref, o_ref, nk_ref, nv_ref, nst_ref, *, group):
    head_row = lax.broadcasted_iota(jnp.int32, (N_Q_HEADS, ATT_W), 0)
    head_lane = lax.broadcasted_iota(jnp.int32, (N_Q_HEADS, ATT_W), 1) // HEAD_DIM
    own = head_row == head_lane
    r8 = lax.broadcasted_iota(jnp.int32, (N_Q_HEADS, KV_W), 0)
    swap = (r8 % 2) != (r8 // Q_PER_KV)
    key_i = lax.broadcasted_iota(jnp.int32, (N_Q_HEADS, BLOCK), 1)
    last_row = lax.broadcasted_iota(jnp.int32, (BLOCK, KV_W), 0) == BLOCK - 1
    sink = sink_ref[...][:, 0:1]

    qkv8 = qkv_ref[...]
    gin8 = gin_ref[...]
    att_rows, o_rows = [], []
    for j in range(group):
        q_row = qkv8[j:j + 1, C_Q:C_K]
        k_new = qkv8[j:j + 1, C_K:C_V]
        v_new = qkv8[j:j + 1, C_V:C_GQ]
        k_old = ck_ref[j]
        v_old = cv_ref[j]
        nk_ref[j] = jnp.where(last_row, k_new, pltpu.roll(k_old, BLOCK - 1, 0))
        nv_ref[j] = jnp.where(last_row, v_new, pltpu.roll(v_old, BLOCK - 1, 0))

        qm = jnp.where(own, jnp.broadcast_to(q_row, (N_Q_HEADS, ATT_W)), 0.0)
        fold = qm[:, 0:128] + qm[:, 128:256] + qm[:, 256:384] + qm[:, 384:512]
        q8 = (jnp.where(swap, pltpu.roll(fold, HEAD_DIM, 1), fold) * (HEAD_DIM ** -0.5)).astype(BF16)
        sc = lax.dot_general(q8, k_old.astype(BF16), (((1,), (1,)), ((), ())), preferred_element_type=F32)
        sc = jnp.where(key_i >= 1, sc, -jnp.inf)
        s_new = jnp.sum(q8.astype(F32) * k_new.astype(BF16).astype(F32), axis=-1, keepdims=True)
        m = jnp.maximum(jnp.maximum(jnp.max(sc, axis=-1, keepdims=True), s_new), sink)
        p = jnp.exp(sc - m)
        p_new = jnp.exp(s_new - m)
        den = jnp.sum(p, axis=-1, keepdims=True) + p_new + jnp.exp(sink - m)
        o8 = (jnp.dot(p.astype(BF16), v_old.astype(BF16), preferred_element_type=F32)
              + p_new.astype(BF16).astype(F32) * v_new.astype(BF16).astype(F32)) / den
        o8 = jnp.where(swap, pltpu.roll(o8, HEAD_DIM, 1), o8)
        o_wide = jnp.concatenate([o8, o8, o8, o8], axis=1)
        att_rows.append(jnp.sum(jnp.where(own, o_wide, 0.0), axis=0, keepdims=True))

        a_col = _row_to_col(jnp.exp(gin8[j:j + 1, 2 * GK_W + GV_W:]))
        q_col = _row_to_col(gin8[j:j + 1, 0:GK_W])
        k_col = _row_to_col(gin8[j:j + 1, GK_W:2 * GK_W])
        v_row = gin8[j:j + 1, 2 * GK_W:2 * GK_W + GV_W]
        v_exp = jnp.concatenate([jnp.broadcast_to(v_row[:, h * GLA_DV:(h + 1) * GLA_DV], (GLA_DK, GLA_DV))
                                 for h in range(GLA_HEADS)], axis=0)
        s1 = a_col * st_ref[j].reshape(GK_W, GLA_DV) + k_col * v_exp
        nst_ref[j] = s1.reshape(GLA_HEADS, GLA_DK, GLA_DV)
        qs = q_col * s1
        o_rows.append(jnp.concatenate(
            [jnp.sum(qs[h * GLA_DK:(h + 1) * GLA_DK], axis=0, keepdims=True) for h in range(GLA_HEADS)], axis=1))
    att_ref[...] = jnp.concatenate(att_rows, axis=0)
    o_ref[...] = jnp.concatenate(o_rows, axis=0)


def _sample_finish_kernel(x_ref, att_ref, o_ref, gate_ref, gn_ref, w_pa_ref, w_pb_ref, w_out_ref, g1_ref, b1_ref,
                          h_ref, *, alpha):
    gla = _gla_gate_out(o_ref[...], gate_ref[:, 0:GV_W], gn_ref[...])
    h_ref[...] = _finish(x_ref[...], att_ref[...], gla, gate_ref[:, GV_W:GV_W + D_MODEL],
                         gate_ref[:, GV_W + D_MODEL:], w_pa_ref[...], w_pb_ref[...], w_out_ref[...],
                         g1_ref[...], b1_ref[...], alpha)


def _rope_tables(pos):
    half = ROT_DIM // 2
    inv = ROPE_THETA ** (-jnp.arange(half, dtype=F32) * 2.0 / ROT_DIM)
    ang = pos.astype(F32)[:, None] * inv[None, :]
    cos, sin = jnp.cos(ang), jnp.sin(ang)
    n = pos.shape[0]
    one = jnp.ones((n, HEAD_DIM - ROT_DIM), F32)
    zero = jnp.zeros((n, HEAD_DIM - ROT_DIM), F32)
    z8 = jnp.zeros((n, half), F32)
    cos_t = jnp.concatenate([cos, cos, one] * 2, axis=1)
    sin_lo = jnp.concatenate([-sin, z8, zero] * 2, axis=1)
    sin_hi = jnp.concatenate([z8, sin, zero] * 2, axis=1)
    return cos_t, sin_lo, sin_hi


def _const_spec(shape, layer=None):
    if layer is None:
        return pl.BlockSpec(shape, lambda *_: (0,) * len(shape), pipeline_mode=pl.Buffered(1))
    return pl.BlockSpec((None,) + shape, lambda *_: (layer,) + (0,) * len(shape), pipeline_mode=pl.Buffered(1))


def _step_rows(total):
    for t in (384, 256, 128):
        if total % t == 0:
            return t
    raise ValueError("padded prompt length must be a multiple of 128")


def kernel(x_prompt, x_sample, cache_k_win, cache_v_win, state_gla, meta_tokens, w_in, w_a2, b_a, attn_sink,
           gla_norm_g, w_proj_a, w_proj_b, w_out, ln1_g, ln1_b, w_up, w_down, ln2_g, ln2_b):
    depth = w_in.shape[0]
    bsz, seq, _ = x_prompt.shape
    nsmp, dec_seq, _ = x_sample.shape
    assert dec_seq == 1 and cache_k_win.shape[2] == BLOCK and seq % BLOCK == 0
    alpha = (2 * depth) ** 0.25
    lp = seq + BLOCK
    ts = _step_rows(lp)
    n_steps = lp // ts
    rows = bsz * lp
    ffn_tile = next(t for t in (512, 384, 256, 128) if rows % t == 0)
    group = SUBLANES
    assert nsmp % group == 0

    w_in_r = jnp.concatenate(
        [w_in[..., :C_GR], w_in[..., C_GR + GLA_RANK:], w_in[..., C_GR:C_GR + GLA_RANK],
         jnp.zeros((depth, D_MODEL, LANES - GLA_RANK), w_in.dtype)], axis=-1).astype(BF16)
    w_a2_p = jnp.concatenate([w_a2, jnp.zeros((depth, LANES - GLA_RANK, GK_W), w_a2.dtype)], axis=1).astype(BF16)
    w_pa, w_pb, w_o = w_proj_a.astype(BF16), w_proj_b.astype(BF16), w_out.astype(BF16)
    w_u, w_d = w_up.astype(BF16), w_down.astype(BF16)
    b_a3 = b_a.reshape(depth, 1, GK_W)
    gn3 = gla_norm_g.reshape(depth, 1, GLA_DV)
    g1, b1 = ln1_g.reshape(depth, 1, D_MODEL), ln1_b.reshape(depth, 1, D_MODEL)
    g2, b2 = ln2_g.reshape(depth, 1, D_MODEL), ln2_b.reshape(depth, 1, D_MODEL)
    sink_lanes = jnp.broadcast_to(attn_sink[:, :, None], (depth, N_Q_HEADS, LANES))

    cos_p, slo_p, shi_p = _rope_tables(jnp.arange(lp) - META_PAD)
    cos_s, slo_s, shi_s = (jnp.broadcast_to(t, (nsmp, LANES)) for t in _rope_tables(PAST_LEN + jnp.arange(1)))

    xp = jnp.concatenate([jnp.zeros((bsz, META_PAD, D_MODEL), x_prompt.dtype),
                          jnp.broadcast_to(meta_tokens[None].astype(x_prompt.dtype), (bsz, N_META, D_MODEL)),
                          x_prompt], axis=1)
    xs = x_sample.reshape(nsmp, D_MODEL)
    ck = cache_k_win.reshape(depth, nsmp, BLOCK, KV_W)
    cv = cache_v_win.reshape(depth, nsmp, BLOCK, KV_W)

    cparams = functools.partial(pltpu.CompilerParams, vmem_limit_bytes=VMEM_LIMIT)
    pk, pv, pst, sk, sv, sst = [], [], [], [], [], []
    for l in range(depth):
        step_spec = pl.BlockSpec((1, ts, D_MODEL), lambda b, s: (b, s, 0))
        tab_spec = pl.BlockSpec((ts, LANES), lambda b, s: (s, 0))
        hp, kwin, vwin, sfin = pl.pallas_call(
            functools.partial(_mixer_prompt_kernel, ts=ts, n_steps=n_steps, alpha=alpha),
            grid=(bsz, n_steps),
            in_specs=[step_spec, tab_spec, tab_spec, tab_spec,
                      _const_spec((D_MODEL, W_IN_COLS), l), _const_spec((LANES, GK_W), l), _const_spec((1, GK_W), l),
                      pl.BlockSpec(memory_space=pltpu.SMEM), _const_spec((1, GLA_DV), l),
                      _const_spec((ATT_W, D_MODEL), l), _const_spec((GV_W, D_MODEL), l),
                      _const_spec((D_MODEL, D_MODEL), l), _const_spec((1, D_MODEL), l), _const_spec((1, D_MODEL), l)],
            out_specs=[step_spec,
                       pl.BlockSpec((1, BLOCK, KV_W), lambda b, s: (b, 0, 0)),
                       pl.BlockSpec((1, BLOCK, KV_W), lambda b, s: (b, 0, 0)),
                       pl.BlockSpec((1, GLA_HEADS, GLA_DK, GLA_DV), lambda b, s: (b, 0, 0, 0))],
            out_shape=[jax.ShapeDtypeStruct((bsz, lp, D_MODEL), F32),
                       jax.ShapeDtypeStruct((bsz, BLOCK, KV_W), F32),
                       jax.ShapeDtypeStruct((bsz, BLOCK, KV_W), F32),
                       jax.ShapeDtypeStruct((bsz, GLA_HEADS, GLA_DK, GLA_DV), F32)],
            scratch_shapes=[pltpu.VMEM((BLOCK + ts, KV_W), BF16), pltpu.VMEM((BLOCK + ts, KV_W), BF16),
                            pltpu.VMEM((GK_W, GV_W), F32),
                            pltpu.VMEM((ts, GK_W), F32), pltpu.VMEM((ts, GK_W), F32), pltpu.VMEM((ts, GV_W), F32),
                            pltpu.VMEM((ts, GK_W), F32), pltpu.VMEM((ts, GV_W), F32), pltpu.VMEM((ts, ATT_W), F32)],
            compiler_params=cparams(dimension_semantics=("arbitrary", "arbitrary")),
            name=f"mixer_prompt_{l}",
        )(xp, cos_p, slo_p, shi_p, w_in_r, w_a2_p, b_a3, attn_sink[l], gn3, w_pa, w_pb, w_o, g1, b1)
        pk.append(kwin.reshape(bsz, BLOCK, N_KV_HEADS, HEAD_DIM))
        pv.append(vwin.reshape(bsz, BLOCK, N_KV_HEADS, HEAD_DIM))
        pst.append(sfin)

        ffn = functools.partial(_ffn_kernel, alpha=alpha, col_chunk=1024)
        ffn_w = [_const_spec((D_MODEL, D_FF), l), _const_spec((D_FF, D_MODEL), l),
                 _const_spec((1, D_MODEL), l), _const_spec((1, D_MODEL), l)]
        xp = pl.pallas_call(
            ffn, grid=(rows // ffn_tile,),
            in_specs=[pl.BlockSpec((ffn_tile, D_MODEL), lambda i: (i, 0))] + ffn_w,
            out_specs=pl.BlockSpec((ffn_tile, D_MODEL), lambda i: (i, 0)),
            out_shape=jax.ShapeDtypeStruct((rows, D_MODEL), F32),
            compiler_params=cparams(dimension_semantics=("arbitrary",)),
            name=f"ffn_prompt_{l}",
        )(hp.reshape(rows, D_MODEL), w_u, w_d, g2, b2).reshape(bsz, lp, D_MODEL)

        qkv, gin, gate = pl.pallas_call(
            _sample_proj_kernel, grid=(1,),
            in_specs=[_const_spec((nsmp, D_MODEL)), _const_spec((nsmp, LANES)), _const_spec((nsmp, LANES)),
                      _const_spec((nsmp, LANES)), _const_spec((D_MODEL, W_IN_COLS), l),
                      _const_spec((LANES, GK_W), l), _const_spec((1, GK_W), l)],
            out_specs=[_const_spec((nsmp, C_GQ)), _const_spec((nsmp, 3 * GK_W + GV_W)),
                       _const_spec((nsmp, C_LR - C_GR))],
            out_shape=[jax.ShapeDtypeStruct((nsmp, C_GQ), F32), jax.ShapeDtypeStruct((nsmp, 3 * GK_W + GV_W), F32),
                       jax.ShapeDtypeStruct((nsmp, C_LR - C_GR), F32)],
            compiler_params=cparams(dimension_semantics=("arbitrary",)),
            name=f"sample_proj_{l}",
        )(xs, cos_s, slo_s, shi_s, w_in_r, w_a2_p, b_a3)

        grp = lambda width: pl.BlockSpec((group, width), lambda i: (i, 0))
        cache_spec = pl.BlockSpec((None, group, BLOCK, KV_W), lambda i: (l, i, 0, 0))
        state_spec = pl.BlockSpec((None, group, GLA_HEADS, GLA_DK, GLA_DV), lambda i: (l, i, 0, 0, 0))
        att_s, o_s, nk, nv, nst = pl.pallas_call(
            functools.partial(_sample_mix_kernel, group=group), grid=(nsmp // group,),
            in_specs=[grp(C_GQ), grp(3 * GK_W + GV_W), cache_spec, cache_spec, state_spec,
                      _const_spec((N_Q_HEADS, LANES), l)],
            out_specs=[grp(ATT_W), grp(GV_W),
                       pl.BlockSpec((group, BLOCK, KV_W), lambda i: (i, 0, 0)),
                       pl.BlockSpec((group, BLOCK, KV_W), lambda i: (i, 0, 0)),
                       pl.BlockSpec((group, GLA_HEADS, GLA_DK, GLA_DV), lambda i: (i, 0, 0, 0))],
            out_shape=[jax.ShapeDtypeStruct((nsmp, ATT_W), F32), jax.ShapeDtypeStruct((nsmp, GV_W), F32),
                       jax.ShapeDtypeStruct((nsmp, BLOCK, KV_W), F32), jax.ShapeDtypeStruct((nsmp, BLOCK, KV_W), F32),
                       jax.ShapeDtypeStruct((nsmp, GLA_HEADS, GLA_DK, GLA_DV), F32)],
            compiler_params=cparams(dimension_semantics=("arbitrary",)),
            name=f"sample_mix_{l}",
        )(qkv, gin, ck, cv, state_gla, sink_lanes)
        sk.append(nk.reshape(nsmp, BLOCK, N_KV_HEADS, HEAD_DIM))
        sv.append(nv.reshape(nsmp, BLOCK, N_KV_HEADS, HEAD_DIM))
        sst.append(nst)

        hs = pl.pallas_call(
            functools.partial(_sample_finish_kernel, alpha=alpha), grid=(1,),
            in_specs=[_const_spec((nsmp, D_MODEL)), _const_spec((nsmp, ATT_W)), _const_spec((nsmp, GV_W)),
                      _const_spec((nsmp, C_LR - C_GR)), _const_spec((1, GLA_DV), l),
                      _const_spec((ATT_W, D_MODEL), l), _const_spec((GV_W, D_MODEL), l),
                      _const_spec((D_MODEL, D_MODEL), l), _const_spec((1, D_MODEL), l), _const_spec((1, D_MODEL), l)],
            out_specs=_const_spec((nsmp, D_MODEL)),
            out_shape=jax.ShapeDtypeStruct((nsmp, D_MODEL), F32),
            compiler_params=cparams(dimension_semantics=("arbitrary",)),
            name=f"sample_finish_{l}",
        )(xs, att_s, o_s, gate, gn3, w_pa, w_pb, w_o, g1, b1)
        xs = pl.pallas_call(
            ffn, grid=(1,),
            in_specs=[_const_spec((nsmp, D_MODEL))] + ffn_w,
            out_specs=_const_spec((nsmp, D_MODEL)),
            out_shape=jax.ShapeDtypeStruct((nsmp, D_MODEL), F32),
            compiler_params=cparams(dimension_semantics=("arbitrary",)),
            name=f"ffn_sample_{l}",
        )(hs, w_u, w_d, g2, b2)

    y_prompt = xp[:, BLOCK:]
    y_sample = xs.reshape(nsmp, 1, D_MODEL)
    return (y_prompt, y_sample, jnp.stack(pk), jnp.stack(pv), jnp.stack(pst),
            jnp.stack(sk), jnp.stack(sv), jnp.stack(sst))
```

```python
import functools

import jax
import jax.numpy as jnp
from jax import lax
from jax.experimental import pallas as pl
from jax.experimental.pallas import tpu as pltpu

F32 = jnp.float32
BF16 = jnp.bfloat16

D_MODEL = 1024
PAST_LEN = 8192
N_META = 16
BLOCK = 128
META_PAD = BLOCK - N_META
HEAD_DIM = 64
N_Q_HEADS = 8
N_KV_HEADS = 2
Q_PER_KV = N_Q_HEADS // N_KV_HEADS
ROT_DIM = HEAD_DIM // 4
ROPE_THETA = 500000.0
GLA_HEADS = 4
GLA_DK = 64
GLA_DV = 128
GLA_RANK = 16
GLA_TAU = 16.0
D_FF = 4 * D_MODEL
ATT_W = N_Q_HEADS * HEAD_DIM
KV_W = N_KV_HEADS * HEAD_DIM
GK_W = GLA_HEADS * GLA_DK
GV_W = GLA_HEADS * GLA_DV
LN_EPS = 1e-5
RMS_EPS = 1e-6
LANES = 128
SUBLANES = 8
VMEM_LIMIT = 56 * 1024 * 1024

C_Q = 0
C_K = C_Q + ATT_W
C_V = C_K + KV_W
C_GQ = C_V + KV_W
C_GK = C_GQ + GK_W
C_GV = C_GK + GK_W
C_GR = C_GV + GV_W
C_GA = C_GR + GV_W
C_GB = C_GA + D_MODEL
C_LR = C_GB + D_MODEL
W_IN_COLS = C_LR + LANES

GLA_FAST_MAX_DECAY = 40.0


def _sigmoid(x):
    return 1.0 / (1.0 + jnp.exp(-x))


def _layer_norm(y, g, b):
    mu = jnp.mean(y, axis=-1, keepdims=True)
    yc = y - mu
    var = jnp.mean(yc * yc, axis=-1, keepdims=True)
    return yc * lax.rsqrt(var + LN_EPS) * g + b


def _rope(t, cos, sin_lo, sin_hi):
    outs = []
    for j in range(t.shape[1] // LANES):
        tj = t[:, j * LANES:(j + 1) * LANES]
        outs.append(tj * cos + pltpu.roll(tj, LANES - ROT_DIM // 2, 1) * sin_lo
                    + pltpu.roll(tj, ROT_DIM // 2, 1) * sin_hi)
    return outs[0] if len(outs) == 1 else jnp.concatenate(outs, axis=1)


def _log_decay(glr, w_a2, b_a):
    z = jnp.dot(glr.astype(BF16), w_a2, preferred_element_type=F32) + b_a
    return (jnp.minimum(z, 0.0) - jnp.log1p(jnp.exp(-jnp.abs(z)))) * (1.0 / GLA_TAU)


def _row_to_col(row):
    n = row.shape[1]
    eye = lax.broadcasted_iota(jnp.int32, (n, n), 0) == lax.broadcasted_iota(jnp.int32, (n, n), 1)
    return jnp.sum(jnp.where(eye, jnp.broadcast_to(row, (n, n)), 0.0), axis=1, keepdims=True)


def _split3_bf16(a):
    hi = a.astype(BF16)
    r = a - hi.astype(F32)
    mid = r.astype(BF16)
    lo = (r - mid.astype(F32)).astype(BF16)
    return hi, mid, lo


def _gla_gate_out(o, gr, gn):
    outs = []
    for h in range(GLA_HEADS):
        oh = o[:, h * GLA_DV:(h + 1) * GLA_DV]
        ms = jnp.mean(oh * oh, axis=-1, keepdims=True)
        outs.append(oh * lax.rsqrt(ms + RMS_EPS) * gn)
    on = jnp.concatenate(outs, axis=1)
    return on * (gr * _sigmoid(gr))


def _finish(x, att, gla, ga, gb, w_pa, w_pb, w_out, g1, b1, alpha):
    pa = jnp.dot(att.astype(BF16), w_pa, preferred_element_type=F32)
    pb = jnp.dot(gla.astype(BF16), w_pb, preferred_element_type=F32)
    m = _sigmoid(ga) * pa + _sigmoid(gb) * pb
    y = alpha * x + jnp.dot(m.astype(BF16), w_out, preferred_element_type=F32)
    return _layer_norm(y, g1, b1)


def _mixer_prompt_kernel(x_ref, cos_ref, slo_ref, shi_ref, w_in_ref, w_a2_ref, b_a_ref, sink_ref, gn_ref,
                         w_pa_ref, w_pb_ref, w_out_ref, g1_ref, b1_ref,
                         h_ref, kwin_ref, vwin_ref, sfin_ref,
                         kc_scr, vc_scr, s_scr, sprev_scr, gq_scr, gk_scr, gv_scr, la_scr, o_scr, att_scr,
                         *, ts, n_steps, alpha):
    s = pl.program_id(1)
    nblk = ts // BLOCK

    @pl.when(s == 0)
    def _():
        kc_scr[0:BLOCK, :] = jnp.zeros((BLOCK, KV_W), BF16)
        vc_scr[0:BLOCK, :] = jnp.zeros((BLOCK, KV_W), BF16)
        s_scr[...] = jnp.zeros(s_scr.shape, F32)

    x = x_ref[0]
    xb = x.astype(BF16)

    def proj(lo, hi):
        return jnp.dot(xb, w_in_ref[:, lo:hi], preferred_element_type=F32)

    cos, slo, shi = cos_ref[...], slo_ref[...], shi_ref[...]
    q = _rope(proj(C_Q, C_K), cos, slo, shi)
    k = _rope(proj(C_K, C_V), cos, slo, shi)
    v = proj(C_V, C_GQ)

    @pl.when(s == n_steps - 1)
    def _():
        kwin_ref[0] = k[ts - BLOCK:, :].T
        vwin_ref[0] = v[ts - BLOCK:, :].T

    q_bf = (q * (HEAD_DIM ** -0.5)).astype(BF16)
    kc_scr[BLOCK:BLOCK + ts, :] = k.astype(BF16)
    vc_scr[BLOCK:BLOCK + ts, :] = v.astype(BF16)
    qi = lax.broadcasted_iota(jnp.int32, (BLOCK, 2 * BLOCK), 0)
    kj = lax.broadcasted_iota(jnp.int32, (BLOCK, 2 * BLOCK), 1)
    band = (kj - qi >= 1) & (kj - qi <= BLOCK)
    for blk in range(nblk):
        first_key_slot = (s * nblk + blk - 1) * BLOCK
        valid = band & (kj + first_key_slot >= META_PAD)
        r0 = blk * BLOCK
        for kv in range(N_KV_HEADS):
            kcat = kc_scr[r0:r0 + 2 * BLOCK, kv * HEAD_DIM:(kv + 1) * HEAD_DIM]
            vcat = vc_scr[r0:r0 + 2 * BLOCK, kv * HEAD_DIM:(kv + 1) * HEAD_DIM]
            heads = [kv * Q_PER_KV + g for g in range(Q_PER_KV)]
            qs = jnp.concatenate([q_bf[r0:r0 + BLOCK, hq * HEAD_DIM:(hq + 1) * HEAD_DIM] for hq in heads], axis=0)
            sc = lax.dot_general(qs, kcat, (((1,), (1,)), ((), ())), preferred_element_type=F32)
            ps, inv = [], []
            for g, hq in enumerate(heads):
                sg = jnp.where(valid, sc[g * BLOCK:(g + 1) * BLOCK], -jnp.inf)
                sk = sink_ref[hq]
                m = jnp.maximum(jnp.max(sg, axis=-1, keepdims=True), sk)
                p = jnp.exp(sg - m)
                den = jnp.sum(p, axis=-1, keepdims=True) + jnp.exp(sk - m)
                ps.append(p.astype(BF16))
                inv.append(1.0 / den)
            o = jnp.dot(jnp.concatenate(ps, axis=0), vcat, preferred_element_type=F32)
            for g, hq in enumerate(heads):
                att_scr[r0:r0 + BLOCK, hq * HEAD_DIM:(hq + 1) * HEAD_DIM] = o[g * BLOCK:(g + 1) * BLOCK] * inv[g]
    kc_scr[0:BLOCK, :] = kc_scr[ts:ts + BLOCK, :]
    vc_scr[0:BLOCK, :] = vc_scr[ts:ts + BLOCK, :]

    live = (s * ts + lax.broadcasted_iota(jnp.int32, (ts, 1), 0)) >= META_PAD
    gq = proj(C_GQ, C_GK) * (GLA_DK ** -0.5)
    gk = jnp.where(live, proj(C_GK, C_GV), 0.0)
    gv = jnp.where(live, proj(C_GV, C_GR), 0.0)
    la = jnp.where(live, _log_decay(proj(C_LR, W_IN_COLS), w_a2_ref[...], b_a_ref[...]), 0.0)

    tri = (lax.broadcasted_iota(jnp.int32, (BLOCK, BLOCK), 1)
           <= lax.broadcasted_iota(jnp.int32, (BLOCK, BLOCK), 0))
    tri_bf = jnp.where(tri, 1.0, 0.0).astype(BF16)
    cums = []
    worst = jnp.zeros((1, GK_W), F32)
    for c in range(nblk):
        parts = jnp.concatenate(_split3_bf16(la[c * BLOCK:(c + 1) * BLOCK]), axis=1)
        b3 = jnp.dot(tri_bf, parts, preferred_element_type=F32)
        b = b3[:, 0:GK_W] + b3[:, GK_W:2 * GK_W] + b3[:, 2 * GK_W:3 * GK_W]
        cums.append(b)
        worst = jnp.maximum(worst, -b[BLOCK - 1:BLOCK, :])
    fast_ok = jnp.max(worst) <= GLA_FAST_MAX_DECAY

    gq_scr[...] = gq
    gk_scr[...] = gk
    gv_scr[...] = gv
    la_scr[...] = la

    head_of_k = lax.broadcasted_iota(jnp.int32, (BLOCK, GK_W), 1) // GLA_DK
    head_of_v = lax.broadcasted_iota(jnp.int32, (BLOCK, GV_W), 1) // GLA_DV
    state_diag = (lax.broadcasted_iota(jnp.int32, (GK_W, GV_W), 0) // GLA_DK
                  == lax.broadcasted_iota(jnp.int32, (GK_W, GV_W), 1) // GLA_DV)

    def gla_fast():
        t_idx = lax.broadcasted_iota(jnp.int32, (BLOCK, GLA_HEADS * BLOCK), 0)
        j_idx = lax.broadcasted_iota(jnp.int32, (BLOCK, GLA_HEADS * BLOCK), 1) % BLOCK
        causal = j_idx <= t_idx
        for c in range(nblk):
            rows = slice(c * BLOCK, (c + 1) * BLOCK)
            b = cums[c]
            b_last = b[BLOCK - 1:BLOCK, :]
            kc = gk[rows]
            q_dec = (gq[rows] * jnp.exp(b)).astype(BF16)
            k_inv = (kc * jnp.exp(-b)).astype(BF16)
            k_end = kc * jnp.exp(b_last - b)
            vc = gv[rows].astype(BF16)
            zk = jnp.zeros_like(k_inv)
            k_bd = jnp.concatenate([jnp.where(head_of_k == h, k_inv, zk) for h in range(GLA_HEADS)], axis=0)
            a = lax.dot_general(q_dec, k_bd, (((1,), (1,)), ((), ())), preferred_element_type=F32)
            a = jnp.where(causal, a, 0.0).astype(BF16)
            zv = jnp.zeros_like(vc)
            v_bd = jnp.concatenate([jnp.where(head_of_v == h, vc, zv) for h in range(GLA_HEADS)], axis=0)
            s0 = s_scr[...]
            o_scr[rows, :] = (jnp.dot(a, v_bd, preferred_element_type=F32)
                              + jnp.dot(q_dec, s0.astype(BF16), preferred_element_type=F32))
            ds = jnp.dot(k_end.T.astype(BF16), vc, preferred_element_type=F32)
            s_scr[...] = _row_to_col(jnp.exp(b_last)) * s0 + jnp.where(state_diag, ds, 0.0)

    def gla_slow():
        def body(i, carry):
            rows = pl.ds(pl.multiple_of(i * SUBLANES, SUBLANES), SUBLANES)
            la8, k8, q8, v8 = la_scr[rows, :], gk_scr[rows, :], gq_scr[rows, :], gv_scr[rows, :]
            outs = []
            for r in range(SUBLANES):
                a_col = _row_to_col(jnp.exp(la8[r:r + 1]))
                k_col = _row_to_col(k8[r:r + 1])
                q_col = _row_to_col(q8[r:r + 1])
                s1 = a_col * s_scr[...] + jnp.where(state_diag, k_col * v8[r:r + 1], 0.0)
                s_scr[...] = s1
                outs.append(jnp.sum(q_col * s1, axis=0, keepdims=True))
            o_scr[rows, :] = jnp.concatenate(outs, axis=0)
            return carry
        lax.fori_loop(0, ts // SUBLANES, body, 0)

    def finish_step(x_val, proj_fn):
        gla = _gla_gate_out(o_scr[...], proj_fn(C_GR, C_GA), gn_ref[...])
        h_ref[0] = _finish(x_val, att_scr[...], gla, proj_fn(C_GA, C_GB), proj_fn(C_GB, C_LR),
                           w_pa_ref[...], w_pb_ref[...], w_out_ref[...], g1_ref[...], b1_ref[...], alpha)

    sprev_scr[...] = s_scr[...]
    gla_fast()
    finish_step(x, proj)

    @pl.when(jnp.logical_not(fast_ok))
    def _():
        s_scr[...] = sprev_scr[...]
        gla_slow()
        x_again = x_ref[0]
        xb_again = x_again.astype(BF16)
        finish_step(x_again, lambda lo, hi: jnp.dot(xb_again, w_in_ref[:, lo:hi], preferred_element_type=F32))

    @pl.when(s == n_steps - 1)
    def _():
        for h in range(GLA_HEADS):
            sfin_ref[0, h] = s_scr[h * GLA_DK:(h + 1) * GLA_DK, h * GLA_DV:(h + 1) * GLA_DV]


def _ffn_kernel(h_ref, w_up_ref, w_dn_ref, g_ref, b_ref, o_ref, *, alpha, col_chunk):
    h = h_ref[...]
    hb = h.astype(BF16)
    acc = jnp.zeros(h.shape, F32)
    for c in range(D_FF // col_chunk):
        u = jnp.dot(hb, w_up_ref[:, c * col_chunk:(c + 1) * col_chunk], preferred_element_type=F32)
        u = jnp.maximum(u, 0.0)
        acc = acc + jnp.dot((u * u).astype(BF16), w_dn_ref[c * col_chunk:(c + 1) * col_chunk, :],
                            preferred_element_type=F32)
    o_ref[...] = _layer_norm(alpha * h + acc, g_ref[...], b_ref[...])


def _sample_proj_kernel(x_ref, cos_ref, slo_ref, shi_ref, w_in_ref, w_a2_ref, b_a_ref,
                        qkv_ref, gin_ref, gate_ref):
    xb = x_ref[...].astype(BF16)

    def proj(lo, hi):
        return jnp.dot(xb, w_in_ref[:, lo:hi], preferred_element_type=F32)

    cos, slo, shi = cos_ref[...], slo_ref[...], shi_ref[...]
    qkv_ref[:, C_Q:C_K] = _rope(proj(C_Q, C_K), cos, slo, shi)
    qkv_ref[:, C_K:C_V] = _rope(proj(C_K, C_V), cos, slo, shi)
    qkv_ref[:, C_V:C_GQ] = proj(C_V, C_GQ)
    gin_ref[:, 0:GK_W] = proj(C_GQ, C_GK) * (GLA_DK ** -0.5)
    gin_ref[:, GK_W:2 * GK_W + GV_W] = proj(C_GK, C_GR)
    gin_ref[:, 2 * GK_W + GV_W:] = _log_decay(proj(C_LR, W_IN_COLS), w_a2_ref[...], b_a_ref[...])
    gate_ref[...] = proj(C_GR, C_LR)


def _sample_mix_kernel(qkv_ref, gin_ref, ck_ref, cv_ref, st_ref, sink_ref, *rest, group):
    att_ref, o_ref, nk_ref, nv_ref, nst_ref = rest[-5:]
    head_row = lax.broadcasted_iota(jnp.int32, (N_Q_HEADS, ATT_W), 0)
    head_lane = lax.broadcasted_iota(jnp.int32, (N_Q_HEADS, ATT_W), 1) // HEAD_DIM
    own = head_row == head_lane
    r8 = lax.broadcasted_iota(jnp.int32, (N_Q_HEADS, KV_W), 0)
    swap = (r8 % 2) != (r8 // Q_PER_KV)
    key_i = lax.broadcasted_iota(jnp.int32, (N_Q_HEADS, BLOCK), 1)
    last_row = lax.broadcasted_iota(jnp.int32, (KV_W, BLOCK), 1) == BLOCK - 1
    sink = sink_ref[...][:, 0:1]

    qkv8 = qkv_ref[...]
    gin8 = gin_ref[...]
    att_rows, o_rows = [], []
    for j in range(group):
        q_row = qkv8[j:j + 1, C_Q:C_K]
        k_new = qkv8[j:j + 1, C_K:C_V]
        v_new = qkv8[j:j + 1, C_V:C_GQ]
        k_old = ck_ref[j].reshape(KV_W, BLOCK)
        v_old = cv_ref[j].reshape(KV_W, BLOCK)
        nk_ref[j] = jnp.where(last_row, _row_to_col(k_new), pltpu.roll(k_old, BLOCK - 1, 1)).reshape(
            N_KV_HEADS, HEAD_DIM, BLOCK)
        nv_ref[j] = jnp.where(last_row, _row_to_col(v_new), pltpu.roll(v_old, BLOCK - 1, 1)).reshape(
            N_KV_HEADS, HEAD_DIM, BLOCK)

        qm = jnp.where(own, jnp.broadcast_to(q_row, (N_Q_HEADS, ATT_W)), 0.0)
        fold = qm[:, 0:128] + qm[:, 128:256] + qm[:, 256:384] + qm[:, 384:512]
        q8 = (jnp.where(swap, pltpu.roll(fold, HEAD_DIM, 1), fold) * (HEAD_DIM ** -0.5)).astype(BF16)
        sc = jnp.dot(q8, k_old.astype(BF16), preferred_element_type=F32)
        sc = jnp.where(key_i >= 1, sc, -jnp.inf)
        s_new = jnp.sum(q8.astype(F32) * k_new.astype(BF16).astype(F32), axis=-1, keepdims=True)
        m = jnp.maximum(jnp.maximum(jnp.max(sc, axis=-1, keepdims=True), s_new), sink)
        p = jnp.exp(sc - m)
        p_new = jnp.exp(s_new - m)
        den = jnp.sum(p, axis=-1, keepdims=True) + p_new + jnp.exp(sink - m)
        o8 = (lax.dot_general(p.astype(BF16), v_old.astype(BF16), (((1,), (1,)), ((), ())),
                              preferred_element_type=F32)
              + p_new.astype(BF16).astype(F32) * v_new.astype(BF16).astype(F32)) / den
        o8 = jnp.where(swap, pltpu.roll(o8, HEAD_DIM, 1), o8)
        o_wide = jnp.concatenate([o8, o8, o8, o8], axis=1)
        att_rows.append(jnp.sum(jnp.where(own, o_wide, 0.0), axis=0, keepdims=True))

        a_col = _row_to_col(jnp.exp(gin8[j:j + 1, 2 * GK_W + GV_W:]))
        q_col = _row_to_col(gin8[j:j + 1, 0:GK_W])
        k_col = _row_to_col(gin8[j:j + 1, GK_W:2 * GK_W])
        v_row = gin8[j:j + 1, 2 * GK_W:2 * GK_W + GV_W]
        v_exp = jnp.concatenate([jnp.broadcast_to(v_row[:, h * GLA_DV:(h + 1) * GLA_DV], (GLA_DK, GLA_DV))
                                 for h in range(GLA_HEADS)], axis=0)
        s1 = a_col * st_ref[j].reshape(GK_W, GLA_DV) + k_col * v_exp
        nst_ref[j] = s1.reshape(GLA_HEADS, GLA_DK, GLA_DV)
        qs = q_col * s1
        o_rows.append(jnp.concatenate(
            [jnp.sum(qs[h * GLA_DK:(h + 1) * GLA_DK], axis=0, keepdims=True) for h in range(GLA_HEADS)], axis=1))
    att_ref[...] = jnp.concatenate(att_rows, axis=0)
    o_ref[...] = jnp.concatenate(o_rows, axis=0)


def _sample_finish_kernel(x_ref, att_ref, o_ref, gate_ref, gn_ref, w_pa_ref, w_pb_ref, w_out_ref, g1_ref, b1_ref,
                          h_ref, *, alpha):
    gla = _gla_gate_out(o_ref[...], gate_ref[:, 0:GV_W], gn_ref[...])
    h_ref[...] = _finish(x_ref[...], att_ref[...], gla, gate_ref[:, GV_W:GV_W + D_MODEL],
                         gate_ref[:, GV_W + D_MODEL:], w_pa_ref[...], w_pb_ref[...], w_out_ref[...],
                         g1_ref[...], b1_ref[...], alpha)


def _rope_tables(pos):
    half = ROT_DIM // 2
    inv = ROPE_THETA ** (-jnp.arange(half, dtype=F32) * 2.0 / ROT_DIM)
    d = jnp.arange(LANES) % HEAD_DIM
    ang = pos.astype(F32)[:, None] * inv[d % half][None, :]
    cos, sin = jnp.cos(ang), jnp.sin(ang)
    cos_t = jnp.where(d < ROT_DIM, cos, 1.0)
    sin_lo = jnp.where(d < half, -sin, 0.0)
    sin_hi = jnp.where((d >= half) & (d < ROT_DIM), sin, 0.0)
    return cos_t, sin_lo, sin_hi


def _const_spec(shape, layer=None):
    if layer is None:
        return pl.BlockSpec(shape, lambda *_: (0,) * len(shape), pipeline_mode=pl.Buffered(1))
    return pl.BlockSpec((None,) + shape, lambda *_: (layer,) + (0,) * len(shape), pipeline_mode=pl.Buffered(1))


def _step_rows(total):
    for t in (384, 256, 128):
        if total % t == 0:
            return t
    raise ValueError("padded prompt length must be a multiple of 128")


def kernel(x_prompt, x_sample, cache_k_win, cache_v_win, state_gla, meta_tokens, w_in, w_a2, b_a, attn_sink,
           gla_norm_g, w_proj_a, w_proj_b, w_out, ln1_g, ln1_b, w_up, w_down, ln2_g, ln2_b):
    depth = w_in.shape[0]
    bsz, seq, _ = x_prompt.shape
    nsmp, dec_seq, _ = x_sample.shape
    assert dec_seq == 1 and cache_k_win.shape[2] == BLOCK and seq % BLOCK == 0
    alpha = (2 * depth) ** 0.25
    lp = seq + BLOCK
    ts = _step_rows(lp)
    n_steps = lp // ts
    rows = bsz * lp
    ffn_tile = next(t for t in (512, 384, 256, 128) if rows % t == 0)
    group = SUBLANES
    assert nsmp % group == 0

    w_in_r = jnp.concatenate(
        [w_in[..., :C_GR], w_in[..., C_GR + GLA_RANK:], w_in[..., C_GR:C_GR + GLA_RANK],
         jnp.zeros((depth, D_MODEL, LANES - GLA_RANK), w_in.dtype)], axis=-1).astype(BF16)
    w_a2_p = jnp.concatenate([w_a2, jnp.zeros((depth, LANES - GLA_RANK, GK_W), w_a2.dtype)], axis=1).astype(BF16)
    w_pa, w_pb, w_o = w_proj_a.astype(BF16), w_proj_b.astype(BF16), w_out.astype(BF16)
    w_u, w_d = w_up.astype(BF16), w_down.astype(BF16)
    b_a3 = b_a.reshape(depth, 1, GK_W)
    gn3 = gla_norm_g.reshape(depth, 1, GLA_DV)
    g1, b1 = ln1_g.reshape(depth, 1, D_MODEL), ln1_b.reshape(depth, 1, D_MODEL)
    g2, b2 = ln2_g.reshape(depth, 1, D_MODEL), ln2_b.reshape(depth, 1, D_MODEL)
    sink_lanes = jnp.broadcast_to(attn_sink[:, :, None], (depth, N_Q_HEADS, LANES))

    cos_p, slo_p, shi_p = _rope_tables(jnp.arange(lp) - META_PAD)
    cos_s, slo_s, shi_s = (jnp.broadcast_to(t, (nsmp, LANES)) for t in _rope_tables(PAST_LEN + jnp.arange(1)))

    xp = jnp.concatenate([jnp.zeros((bsz, META_PAD, D_MODEL), x_prompt.dtype),
                          jnp.broadcast_to(meta_tokens[None].astype(x_prompt.dtype), (bsz, N_META, D_MODEL)),
                          x_prompt], axis=1)
    xs = x_sample.reshape(nsmp, D_MODEL)
    ck = jnp.transpose(cache_k_win, (0, 1, 3, 4, 2))
    cv = jnp.transpose(cache_v_win, (0, 1, 3, 4, 2))

    cparams = functools.partial(pltpu.CompilerParams, vmem_limit_bytes=VMEM_LIMIT)
    pk, pv, pst = [], [], []
    stacked = []
    for l in range(depth):
        step_spec = pl.BlockSpec((1, ts, D_MODEL), lambda b, s: (b, s, 0))
        tab_spec = pl.BlockSpec((ts, LANES), lambda b, s: (s, 0))
        hp, kwin, vwin, sfin = pl.pallas_call(
            functools.partial(_mixer_prompt_kernel, ts=ts, n_steps=n_steps, alpha=alpha),
            grid=(bsz, n_steps),
            in_specs=[step_spec, tab_spec, tab_spec, tab_spec,
                      _const_spec((D_MODEL, W_IN_COLS), l), _const_spec((LANES, GK_W), l), _const_spec((1, GK_W), l),
                      pl.BlockSpec(memory_space=pltpu.SMEM), _const_spec((1, GLA_DV), l),
                      _const_spec((ATT_W, D_MODEL), l), _const_spec((GV_W, D_MODEL), l),
                      _const_spec((D_MODEL, D_MODEL), l), _const_spec((1, D_MODEL), l), _const_spec((1, D_MODEL), l)],
            out_specs=[step_spec,
                       pl.BlockSpec((1, BLOCK, KV_W), lambda b, s: (b, 0, 0)),
                       pl.BlockSpec((1, BLOCK, KV_W), lambda b, s: (b, 0, 0)),
                       pl.BlockSpec((1, GLA_HEADS, GLA_DK, GLA_DV), lambda b, s: (b, 0, 0, 0))],
            out_shape=[jax.ShapeDtypeStruct((bsz, lp, D_MODEL), F32),
                       jax.ShapeDtypeStruct((bsz, BLOCK, KV_W), F32),
                       jax.ShapeDtypeStruct((bsz, BLOCK, KV_W), F32),
                       jax.ShapeDtypeStruct((bsz, GLA_HEADS, GLA_DK, GLA_DV), F32)],
            scratch_shapes=[pltpu.VMEM((BLOCK + ts, KV_W), BF16), pltpu.VMEM((BLOCK + ts, KV_W), BF16),
                            pltpu.VMEM((GK_W, GV_W), F32), pltpu.VMEM((GK_W, GV_W), F32),
                            pltpu.VMEM((ts, GK_W), F32), pltpu.VMEM((ts, GK_W), F32), pltpu.VMEM((ts, GV_W), F32),
                            pltpu.VMEM((ts, GK_W), F32), pltpu.VMEM((ts, GV_W), F32), pltpu.VMEM((ts, ATT_W), F32)],
            compiler_params=cparams(dimension_semantics=("arbitrary", "arbitrary")),
            name=f"mixer_prompt_{l}",
        )(xp, cos_p, slo_p, shi_p, w_in_r, w_a2_p, b_a3, attn_sink[l], gn3, w_pa, w_pb, w_o, g1, b1)
        pk.append(kwin.reshape(bsz, N_KV_HEADS, HEAD_DIM, BLOCK))
        pv.append(vwin.reshape(bsz, N_KV_HEADS, HEAD_DIM, BLOCK))
        pst.append(sfin)

        ffn = functools.partial(_ffn_kernel, alpha=alpha, col_chunk=1024)
        ffn_w = [_const_spec((D_MODEL, D_FF), l), _const_spec((D_FF, D_MODEL), l),
                 _const_spec((1, D_MODEL), l), _const_spec((1, D_MODEL), l)]
        xp = pl.pallas_call(
            ffn, grid=(rows // ffn_tile,),
            in_specs=[pl.BlockSpec((ffn_tile, D_MODEL), lambda i: (i, 0))] + ffn_w,
            out_specs=pl.BlockSpec((ffn_tile, D_MODEL), lambda i: (i, 0)),
            out_shape=jax.ShapeDtypeStruct((rows, D_MODEL), F32),
            compiler_params=cparams(dimension_semantics=("arbitrary",)),
            name=f"ffn_prompt_{l}",
        )(hp.reshape(rows, D_MODEL), w_u, w_d, g2, b2).reshape(bsz, lp, D_MODEL)

        qkv, gin, gate = pl.pallas_call(
            _sample_proj_kernel, grid=(1,),
            in_specs=[_const_spec((nsmp, D_MODEL)), _const_spec((nsmp, LANES)), _const_spec((nsmp, LANES)),
                      _const_spec((nsmp, LANES)), _const_spec((D_MODEL, W_IN_COLS), l),
                      _const_spec((LANES, GK_W), l), _const_spec((1, GK_W), l)],
            out_specs=[_const_spec((nsmp, C_GQ)), _const_spec((nsmp, 3 * GK_W + GV_W)),
                       _const_spec((nsmp, C_LR - C_GR))],
            out_shape=[jax.ShapeDtypeStruct((nsmp, C_GQ), F32), jax.ShapeDtypeStruct((nsmp, 3 * GK_W + GV_W), F32),
                       jax.ShapeDtypeStruct((nsmp, C_LR - C_GR), F32)],
            compiler_params=cparams(dimension_semantics=("arbitrary",)),
            name=f"sample_proj_{l}",
        )(xs, cos_s, slo_s, shi_s, w_in_r, w_a2_p, b_a3)

        grp = lambda width: pl.BlockSpec((group, width), lambda i: (i, 0))
        cache_spec = pl.BlockSpec((None, group, N_KV_HEADS, HEAD_DIM, BLOCK), lambda i: (l, i, 0, 0, 0))
        state_spec = pl.BlockSpec((None, group, GLA_HEADS, GLA_DK, GLA_DV), lambda i: (l, i, 0, 0, 0))
        n_mix_in = 6
        att_s, o_s, *stacked = pl.pallas_call(
            functools.partial(_sample_mix_kernel, group=group), grid=(nsmp // group,),
            in_specs=[grp(C_GQ), grp(3 * GK_W + GV_W), cache_spec, cache_spec, state_spec,
                      _const_spec((N_Q_HEADS, LANES), l)] + [pl.BlockSpec(memory_space=pl.ANY)] * len(stacked),
            out_specs=[grp(ATT_W), grp(GV_W), cache_spec, cache_spec, state_spec],
            out_shape=[jax.ShapeDtypeStruct((nsmp, ATT_W), F32), jax.ShapeDtypeStruct((nsmp, GV_W), F32),
                       jax.ShapeDtypeStruct(ck.shape, F32), jax.ShapeDtypeStruct(cv.shape, F32),
                       jax.ShapeDtypeStruct(state_gla.shape, F32)],
            input_output_aliases={n_mix_in + i: 2 + i for i in range(len(stacked))},
            compiler_params=cparams(dimension_semantics=("arbitrary",)),
            name=f"sample_mix_{l}",
        )(qkv, gin, ck, cv, state_gla, sink_lanes, *stacked)

        hs = pl.pallas_call(
            functools.partial(_sample_finish_kernel, alpha=alpha), grid=(1,),
            in_specs=[_const_spec((nsmp, D_MODEL)), _const_spec((nsmp, ATT_W)), _const_spec((nsmp, GV_W)),
                      _const_spec((nsmp, C_LR - C_GR)), _const_spec((1, GLA_DV), l),
                      _const_spec((ATT_W, D_MODEL), l), _const_spec((GV_W, D_MODEL), l),
                      _const_spec((D_MODEL, D_MODEL), l), _const_spec((1, D_MODEL), l), _const_spec((1, D_MODEL), l)],
            out_specs=_const_spec((nsmp, D_MODEL)),
            out_shape=jax.ShapeDtypeStruct((nsmp, D_MODEL), F32),
            compiler_params=cparams(dimension_semantics=("arbitrary",)),
            name=f"sample_finish_{l}",
        )(xs, att_s, o_s, gate, gn3, w_pa, w_pb, w_o, g1, b1)
        xs = pl.pallas_call(
            ffn, grid=(1,),
            in_specs=[_const_spec((nsmp, D_MODEL))] + ffn_w,
            out_specs=_const_spec((nsmp, D_MODEL)),
            out_shape=jax.ShapeDtypeStruct((nsmp, D_MODEL), F32),
            compiler_params=cparams(dimension_semantics=("arbitrary",)),
            name=f"ffn_sample_{l}",
        )(hs, w_u, w_d, g2, b2)

    y_prompt = xp[:, BLOCK:]
    y_sample = xs.reshape(nsmp, 1, D_MODEL)
    to_rows = lambda t: jnp.transpose(t, (0, 1, 4, 2, 3))
    return (y_prompt, y_sample, to_rows(jnp.stack(pk)), to_rows(jnp.stack(pv)), jnp.stack(pst),
            to_rows(stacked[0]), to_rows(stacked[1]), stacked[2])
```

```python
import functools

import jax
import jax.numpy as jnp
from jax import lax
from jax.experimental import pallas as pl
from jax.experimental.pallas import tpu as pltpu

F32 = jnp.float32
BF16 = jnp.bfloat16

D_MODEL = 1024
PAST_LEN = 8192
N_META = 16
BLOCK = 128
META_PAD = BLOCK - N_META
HEAD_DIM = 64
N_Q_HEADS = 8
N_KV_HEADS = 2
Q_PER_KV = N_Q_HEADS // N_KV_HEADS
ROT_DIM = HEAD_DIM // 4
ROPE_THETA = 500000.0
GLA_HEADS = 4
GLA_DK = 64
GLA_DV = 128
GLA_RANK = 16
GLA_TAU = 16.0
D_FF = 4 * D_MODEL
ATT_W = N_Q_HEADS * HEAD_DIM
KV_W = N_KV_HEADS * HEAD_DIM
GK_W = GLA_HEADS * GLA_DK
GV_W = GLA_HEADS * GLA_DV
LN_EPS = 1e-5
RMS_EPS = 1e-6
LANES = 128
SUBLANES = 8
VMEM_LIMIT = 56 * 1024 * 1024

C_Q = 0
C_K = C_Q + ATT_W
C_V = C_K + KV_W
C_GQ = C_V + KV_W
C_GK = C_GQ + GK_W
C_GV = C_GK + GK_W
C_LR = C_GV + GV_W
C_GR = C_LR + GLA_RANK
C_GA = C_GR + GV_W
C_GB = C_GA + D_MODEL
W_IN_COLS = C_GB + D_MODEL
C_LR_END = C_LR + LANES

GLA_FAST_MAX_DECAY = 40.0


def _sigmoid(x):
    return 1.0 / (1.0 + jnp.exp(-x))


def _layer_norm(y, g, b):
    mu = jnp.mean(y, axis=-1, keepdims=True)
    yc = y - mu
    var = jnp.mean(yc * yc, axis=-1, keepdims=True)
    return yc * lax.rsqrt(var + LN_EPS) * g + b


def _rope(t, cos, sin_lo, sin_hi):
    outs = []
    for j in range(t.shape[1] // LANES):
        tj = t[:, j * LANES:(j + 1) * LANES]
        outs.append(tj * cos + pltpu.roll(tj, LANES - ROT_DIM // 2, 1) * sin_lo
                    + pltpu.roll(tj, ROT_DIM // 2, 1) * sin_hi)
    return outs[0] if len(outs) == 1 else jnp.concatenate(outs, axis=1)


def _in_proj(xb, w_t_ref, lo, hi):
    return lax.dot_general(xb, w_t_ref[lo:hi, :], (((1,), (1,)), ((), ())), preferred_element_type=F32)


def _log_decay(glr, w_a2, b_a):
    z = jnp.dot(glr.astype(BF16), w_a2, preferred_element_type=F32) + b_a
    return (jnp.minimum(z, 0.0) - jnp.log1p(jnp.exp(-jnp.abs(z)))) * (1.0 / GLA_TAU)


def _row_to_col(row):
    n = row.shape[1]
    eye = lax.broadcasted_iota(jnp.int32, (n, n), 0) == lax.broadcasted_iota(jnp.int32, (n, n), 1)
    return jnp.sum(jnp.where(eye, jnp.broadcast_to(row, (n, n)), 0.0), axis=1, keepdims=True)


N_SPLIT = 3


def _split3_bf16(a):
    hi = a.astype(BF16)
    r = a - hi.astype(F32)
    mid = r.astype(BF16)
    lo = (r - mid.astype(F32)).astype(BF16)
    return hi, mid, lo


def _gla_gate_out(o, gr, gn):
    outs = []
    for h in range(GLA_HEADS):
        oh = o[:, h * GLA_DV:(h + 1) * GLA_DV]
        ms = jnp.mean(oh * oh, axis=-1, keepdims=True)
        outs.append(oh * lax.rsqrt(ms + RMS_EPS) * gn)
    on = jnp.concatenate(outs, axis=1)
    return on * (gr * _sigmoid(gr))


def _finish(x, att, gla, ga, gb, w_pa, w_pb, w_out, g1, b1, alpha):
    pa = jnp.dot(att.astype(BF16), w_pa, preferred_element_type=F32)
    pb = jnp.dot(gla.astype(BF16), w_pb, preferred_element_type=F32)
    m = _sigmoid(ga) * pa + _sigmoid(gb) * pb
    y = alpha * x + jnp.dot(m.astype(BF16), w_out, preferred_element_type=F32)
    return _layer_norm(y, g1, b1)


def _mixer_prompt_kernel(*refs, ts, n_steps, alpha, from_tokens):
    s = pl.program_id(1)
    nblk = ts // BLOCK
    n_x = 1 + nblk if from_tokens else 1
    x_refs = refs[:n_x]
    (cos_ref, slo_ref, shi_ref, w_in_ref, w_a2_ref, b_a_ref, sink_ref, gn_ref,
     w_pa_ref, w_pb_ref, w_out_ref, g1_ref, b1_ref,
     h_ref, kwin_ref, vwin_ref, sfin_ref,
     kc_scr, vc_scr, s_scr, sprev_scr, gq_scr, gk_scr, gv_scr, la_scr, o_scr, att_scr) = refs[n_x:]

    def load_x():
        if not from_tokens:
            return x_refs[0][0]
        blocks = [r[0] for r in x_refs[1:]]
        blocks[0] = jnp.where(s == 0, x_refs[0][...], blocks[0])
        return jnp.concatenate(blocks, axis=0)

    @pl.when(s == 0)
    def _():
        kc_scr[0:BLOCK, :] = jnp.zeros((BLOCK, KV_W), BF16)
        vc_scr[0:BLOCK, :] = jnp.zeros((BLOCK, KV_W), BF16)
        s_scr[...] = jnp.zeros(s_scr.shape, F32)

    x = load_x()
    xb = x.astype(BF16)
    proj = functools.partial(_in_proj, xb, w_in_ref)

    cos, slo, shi = cos_ref[...], slo_ref[...], shi_ref[...]
    q = _rope(proj(C_Q, C_K), cos, slo, shi)
    k = _rope(proj(C_K, C_V), cos, slo, shi)
    v = proj(C_V, C_GQ)

    @pl.when(s == n_steps - 1)
    def _():
        kwin_ref[0] = k[ts - BLOCK:, :].T
        vwin_ref[0] = v[ts - BLOCK:, :].T

    q_bf = (q * (HEAD_DIM ** -0.5)).astype(BF16)
    kc_scr[BLOCK:BLOCK + ts, :] = k.astype(BF16)
    vc_scr[BLOCK:BLOCK + ts, :] = v.astype(BF16)
    qi = lax.broadcasted_iota(jnp.int32, (BLOCK, 2 * BLOCK), 0)
    kj = lax.broadcasted_iota(jnp.int32, (BLOCK, 2 * BLOCK), 1)
    band = (kj - qi >= 1) & (kj - qi <= BLOCK)
    for blk in range(nblk):
        first_key_slot = (s * nblk + blk - 1) * BLOCK
        valid = band & (kj + first_key_slot >= META_PAD)
        r0 = blk * BLOCK
        for kv in range(N_KV_HEADS):
            kcat = kc_scr[r0:r0 + 2 * BLOCK, kv * HEAD_DIM:(kv + 1) * HEAD_DIM]
            vcat = vc_scr[r0:r0 + 2 * BLOCK, kv * HEAD_DIM:(kv + 1) * HEAD_DIM]
            heads = [kv * Q_PER_KV + g for g in range(Q_PER_KV)]
            qs = jnp.concatenate([q_bf[r0:r0 + BLOCK, hq * HEAD_DIM:(hq + 1) * HEAD_DIM] for hq in heads], axis=0)
            sc = lax.dot_general(qs, kcat, (((1,), (1,)), ((), ())), preferred_element_type=F32)
            ps, inv = [], []
            for g, hq in enumerate(heads):
                sg = jnp.where(valid, sc[g * BLOCK:(g + 1) * BLOCK], -jnp.inf)
                sk = sink_ref[hq]
                m = jnp.maximum(jnp.max(sg, axis=-1, keepdims=True), sk)
                p = jnp.exp(sg - m)
                den = jnp.sum(p, axis=-1, keepdims=True) + jnp.exp(sk - m)
                ps.append(p.astype(BF16))
                inv.append(1.0 / den)
            o = jnp.dot(jnp.concatenate(ps, axis=0), vcat, preferred_element_type=F32)
            for g, hq in enumerate(heads):
                att_scr[r0:r0 + BLOCK, hq * HEAD_DIM:(hq + 1) * HEAD_DIM] = o[g * BLOCK:(g + 1) * BLOCK] * inv[g]
    kc_scr[0:BLOCK, :] = kc_scr[ts:ts + BLOCK, :]
    vc_scr[0:BLOCK, :] = vc_scr[ts:ts + BLOCK, :]

    live = (s * ts + lax.broadcasted_iota(jnp.int32, (ts, 1), 0)) >= META_PAD
    gq = proj(C_GQ, C_GK) * (GLA_DK ** -0.5)
    gk = jnp.where(live, proj(C_GK, C_GV), 0.0)
    gv = jnp.where(live, proj(C_GV, C_LR), 0.0)
    la = jnp.where(live, _log_decay(proj(C_LR, C_LR_END), w_a2_ref[...], b_a_ref[...]), 0.0)

    tri = (lax.broadcasted_iota(jnp.int32, (BLOCK, BLOCK), 1)
           <= lax.broadcasted_iota(jnp.int32, (BLOCK, BLOCK), 0))
    tri_bf = jnp.where(tri, 1.0, 0.0).astype(BF16)
    cums = []
    worst = jnp.zeros((1, GK_W), F32)
    for c in range(nblk):
        parts = jnp.concatenate(_split3_bf16(la[c * BLOCK:(c + 1) * BLOCK]), axis=1)
        b3 = jnp.dot(tri_bf, parts, preferred_element_type=F32)
        b = b3[:, 0:GK_W] + b3[:, GK_W:2 * GK_W] + b3[:, 2 * GK_W:3 * GK_W]
        cums.append(b)
        worst = jnp.maximum(worst, -b[BLOCK - 1:BLOCK, :])
    fast_ok = jnp.max(worst) <= GLA_FAST_MAX_DECAY

    gq_scr[...] = gq
    gk_scr[...] = gk
    gv_scr[...] = gv
    la_scr[...] = la

    head_of_k = lax.broadcasted_iota(jnp.int32, (BLOCK, GK_W), 1) // GLA_DK
    head_of_v = lax.broadcasted_iota(jnp.int32, (BLOCK, GV_W), 1) // GLA_DV
    state_diag = (lax.broadcasted_iota(jnp.int32, (GK_W, GV_W), 0) // GLA_DK
                  == lax.broadcasted_iota(jnp.int32, (GK_W, GV_W), 1) // GLA_DV)

    def gla_fast():
        t_idx = lax.broadcasted_iota(jnp.int32, (BLOCK, GLA_HEADS * BLOCK), 0)
        j_idx = lax.broadcasted_iota(jnp.int32, (BLOCK, GLA_HEADS * BLOCK), 1) % BLOCK
        causal = j_idx <= t_idx
        for c in range(nblk):
            rows = slice(c * BLOCK, (c + 1) * BLOCK)
            b = cums[c]
            b_last = b[BLOCK - 1:BLOCK, :]
            kc = gk[rows]
            q_dec = (gq[rows] * jnp.exp(b)).astype(BF16)
            k_inv = (kc * jnp.exp(-b)).astype(BF16)
            k_end = kc * jnp.exp(b_last - b)
            vc = gv[rows].astype(BF16)
            zk = jnp.zeros_like(k_inv)
            k_bd = jnp.concatenate([jnp.where(head_of_k == h, k_inv, zk) for h in range(GLA_HEADS)], axis=0)
            a = lax.dot_general(q_dec, k_bd, (((1,), (1,)), ((), ())), preferred_element_type=F32)
            a = jnp.where(causal, a, 0.0).astype(BF16)
            zv = jnp.zeros_like(vc)
            v_bd = jnp.concatenate([jnp.where(head_of_v == h, vc, zv) for h in range(GLA_HEADS)], axis=0)
            s0 = s_scr[...]
            o_scr[rows, :] = (jnp.dot(a, v_bd, preferred_element_type=F32)
                              + jnp.dot(q_dec, s0.astype(BF16), preferred_element_type=F32))
            ds = jnp.dot(k_end.T.astype(BF16), vc, preferred_element_type=F32)
            s_scr[...] = _row_to_col(jnp.exp(b_last)) * s0 + jnp.where(state_diag, ds, 0.0)

    def gla_slow():
        def body(i, carry):
            rows = pl.ds(pl.multiple_of(i * SUBLANES, SUBLANES), SUBLANES)
            la8, k8, q8, v8 = la_scr[rows, :], gk_scr[rows, :], gq_scr[rows, :], gv_scr[rows, :]
            outs = []
            for r in range(SUBLANES):
                a_col = _row_to_col(jnp.exp(la8[r:r + 1]))
                k_col = _row_to_col(k8[r:r + 1])
                q_col = _row_to_col(q8[r:r + 1])
                s1 = a_col * s_scr[...] + jnp.where(state_diag, k_col * v8[r:r + 1], 0.0)
                s_scr[...] = s1
                outs.append(jnp.sum(q_col * s1, axis=0, keepdims=True))
            o_scr[rows, :] = jnp.concatenate(outs, axis=0)
            return carry
        lax.fori_loop(0, ts // SUBLANES, body, 0)

    def finish_step(x_val, proj_fn):
        gla = _gla_gate_out(o_scr[...], proj_fn(C_GR, C_GA), gn_ref[...])
        h_ref[0] = _finish(x_val, att_scr[...], gla, proj_fn(C_GA, C_GB), proj_fn(C_GB, W_IN_COLS),
                           w_pa_ref[...], w_pb_ref[...], w_out_ref[...], g1_ref[...], b1_ref[...], alpha)

    sprev_scr[...] = s_scr[...]
    gla_fast()
    finish_step(x, proj)

    @pl.when(jnp.logical_not(fast_ok))
    def _():
        s_scr[...] = sprev_scr[...]
        gla_slow()
        x_again = load_x()
        finish_step(x_again, functools.partial(_in_proj, x_again.astype(BF16), w_in_ref))

    @pl.when(s == n_steps - 1)
    def _():
        for h in range(GLA_HEADS):
            sfin_ref[0, h] = s_scr[h * GLA_DK:(h + 1) * GLA_DK, h * GLA_DV:(h + 1) * GLA_DV]


def _ffn_kernel(*refs, alpha, col_chunk):
    h_refs, (w_up_ref, w_dn_ref, g_ref, b_ref, o_ref) = refs[:-5], refs[-5:]
    if len(h_refs) == 1:
        h = h_refs[0][...]
    else:
        h = jnp.concatenate([r[0] for r in h_refs], axis=0)
    hb = h.astype(BF16)
    acc = jnp.zeros(h.shape, F32)
    for c in range(D_FF // col_chunk):
        u = jnp.dot(hb, w_up_ref[:, c * col_chunk:(c + 1) * col_chunk], preferred_element_type=F32)
        u = jnp.maximum(u, 0.0)
        acc = acc + jnp.dot((u * u).astype(BF16), w_dn_ref[c * col_chunk:(c + 1) * col_chunk, :],
                            preferred_element_type=F32)
    o_ref[...] = _layer_norm(alpha * h + acc, g_ref[...], b_ref[...]).reshape(o_ref.shape)


def _sample_proj_kernel(x_ref, cos_ref, slo_ref, shi_ref, w_in_ref, w_a2_ref, b_a_ref,
                        qkv_ref, gv_ref, gate_ref, kvt_ref, gcol_ref):
    proj = functools.partial(_in_proj, x_ref[...].astype(BF16), w_in_ref)

    cos, slo, shi = cos_ref[...], slo_ref[...], shi_ref[...]
    k = _rope(proj(C_K, C_V), cos, slo, shi)
    v = proj(C_V, C_GQ)
    qkv_ref[:, C_Q:C_K] = _rope(proj(C_Q, C_K), cos, slo, shi)
    qkv_ref[:, C_K:C_V] = k
    qkv_ref[:, C_V:C_GQ] = v
    gv_ref[...] = proj(C_GV, C_LR)
    gate_ref[...] = proj(C_GR, W_IN_COLS)
    def store_planes(ref, i, t):
        for p, part in enumerate(_split3_bf16(t)):
            ref[N_SPLIT * i + p] = part.astype(F32).T

    store_planes(kvt_ref, 0, k)
    store_planes(kvt_ref, 1, v)
    store_planes(gcol_ref, 0, jnp.exp(_log_decay(proj(C_LR, C_LR_END), w_a2_ref[...], b_a_ref[...])))
    store_planes(gcol_ref, 1, proj(C_GQ, C_GK) * (GLA_DK ** -0.5))
    store_planes(gcol_ref, 2, proj(C_GK, C_GV))


def _sample_mix_kernel(qkv_ref, gv_ref, kvt_ref, gcol_ref, ck_ref, cv_ref, st_ref, sink_ref, *rest, group):
    att_ref, o_ref, nk_ref, nv_ref, nst_ref = rest[-5:]
    head_row = lax.broadcasted_iota(jnp.int32, (N_Q_HEADS, ATT_W), 0)
    head_lane = lax.broadcasted_iota(jnp.int32, (N_Q_HEADS, ATT_W), 1) // HEAD_DIM
    own = head_row == head_lane
    r8 = lax.broadcasted_iota(jnp.int32, (N_Q_HEADS, KV_W), 0)
    swap = (r8 % 2) != (r8 // Q_PER_KV)
    key_i = lax.broadcasted_iota(jnp.int32, (N_Q_HEADS, BLOCK), 1)
    last_row = lax.broadcasted_iota(jnp.int32, (KV_W, BLOCK), 1) == BLOCK - 1
    sink = sink_ref[...][:, 0:1]

    qkv8 = qkv_ref[...]
    gv8 = gv_ref[...]
    sel = (lax.broadcasted_iota(jnp.int32, (N_SPLIT * group, group * LANES), 0) % group
           == lax.broadcasted_iota(jnp.int32, (N_SPLIT * group, group * LANES), 1) // LANES)
    sel = jnp.where(sel, 1.0, 0.0).astype(BF16)
    spread = lambda ref, i: jnp.dot(ref[i].astype(BF16), sel, preferred_element_type=F32)
    k_cols, v_cols = spread(kvt_ref, 0), spread(kvt_ref, 1)
    a_cols, q_cols, k_gla_cols = (spread(gcol_ref, i) for i in range(3))
    att_rows, o_rows = [], []
    for j in range(group):
        q_row = qkv8[j:j + 1, C_Q:C_K]
        k_new = qkv8[j:j + 1, C_K:C_V]
        v_new = qkv8[j:j + 1, C_V:C_GQ]
        k_old = ck_ref[j].reshape(KV_W, BLOCK)
        v_old = cv_ref[j].reshape(KV_W, BLOCK)
        lanes_j = slice(j * LANES, (j + 1) * LANES)
        nk_ref[j] = jnp.where(last_row, k_cols[:, lanes_j], pltpu.roll(k_old, BLOCK - 1, 1)).reshape(
            N_KV_HEADS, HEAD_DIM, BLOCK)
        nv_ref[j] = jnp.where(last_row, v_cols[:, lanes_j], pltpu.roll(v_old, BLOCK - 1, 1)).reshape(
            N_KV_HEADS, HEAD_DIM, BLOCK)

        qm = jnp.where(own, jnp.broadcast_to(q_row, (N_Q_HEADS, ATT_W)), 0.0)
        fold = qm[:, 0:128] + qm[:, 128:256] + qm[:, 256:384] + qm[:, 384:512]
        q8 = (jnp.where(swap, pltpu.roll(fold, HEAD_DIM, 1), fold) * (HEAD_DIM ** -0.5)).astype(BF16)
        sc = jnp.dot(q8, k_old.astype(BF16), preferred_element_type=F32)
        sc = jnp.where(key_i >= 1, sc, -jnp.inf)
        s_new = jnp.sum(q8.astype(F32) * k_new.astype(BF16).astype(F32), axis=-1, keepdims=True)
        m = jnp.maximum(jnp.maximum(jnp.max(sc, axis=-1, keepdims=True), s_new), sink)
        p = jnp.exp(sc - m)
        p_new = jnp.exp(s_new - m)
        den = jnp.sum(p, axis=-1, keepdims=True) + p_new + jnp.exp(sink - m)
        o8 = (lax.dot_general(p.astype(BF16), v_old.astype(BF16), (((1,), (1,)), ((), ())),
                              preferred_element_type=F32)
              + p_new.astype(BF16).astype(F32) * v_new.astype(BF16).astype(F32)) / den
        o8 = jnp.where(swap, pltpu.roll(o8, HEAD_DIM, 1), o8)
        o_wide = jnp.concatenate([o8, o8, o8, o8], axis=1)
        att_rows.append(jnp.sum(jnp.where(own, o_wide, 0.0), axis=0, keepdims=True))

        a_col, q_col, k_col = a_cols[:, lanes_j], q_cols[:, lanes_j], k_gla_cols[:, lanes_j]
        v_row = gv8[j:j + 1, :]
        v_exp = jnp.concatenate([jnp.broadcast_to(v_row[:, h * GLA_DV:(h + 1) * GLA_DV], (GLA_DK, GLA_DV))
                                 for h in range(GLA_HEADS)], axis=0)
        s1 = a_col * st_ref[j].reshape(GK_W, GLA_DV) + k_col * v_exp
        nst_ref[j] = s1.reshape(GLA_HEADS, GLA_DK, GLA_DV)
        qs = q_col * s1
        o_rows.append(jnp.concatenate(
            [jnp.sum(qs[h * GLA_DK:(h + 1) * GLA_DK], axis=0, keepdims=True) for h in range(GLA_HEADS)], axis=1))
    att_ref[...] = jnp.concatenate(att_rows, axis=0)
    o_ref[...] = jnp.concatenate(o_rows, axis=0)


def _sample_finish_kernel(x_ref, att_ref, o_ref, gate_ref, gn_ref, w_pa_ref, w_pb_ref, w_out_ref, g1_ref, b1_ref,
                          h_ref, *, alpha):
    gla = _gla_gate_out(o_ref[...], gate_ref[:, 0:GV_W], gn_ref[...])
    h_ref[...] = _finish(x_ref[...], att_ref[...], gla, gate_ref[:, GV_W:GV_W + D_MODEL],
                         gate_ref[:, GV_W + D_MODEL:], w_pa_ref[...], w_pb_ref[...], w_out_ref[...],
                         g1_ref[...], b1_ref[...], alpha)


def _rope_tables(pos):
    half = ROT_DIM // 2
    inv = ROPE_THETA ** (-jnp.arange(half, dtype=F32) * 2.0 / ROT_DIM)
    d = jnp.arange(LANES) % HEAD_DIM
    ang = pos.astype(F32)[:, None] * inv[d % half][None, :]
    cos, sin = jnp.cos(ang), jnp.sin(ang)
    cos_t = jnp.where(d < ROT_DIM, cos, 1.0)
    sin_lo = jnp.where(d < half, -sin, 0.0)
    sin_hi = jnp.where((d >= half) & (d < ROT_DIM), sin, 0.0)
    return cos_t, sin_lo, sin_hi


def _const_spec(shape, layer=None):
    if layer is None:
        return pl.BlockSpec(shape, lambda *_: (0,) * len(shape), pipeline_mode=pl.Buffered(1))
    return pl.BlockSpec((None,) + shape, lambda *_: (layer,) + (0,) * len(shape), pipeline_mode=pl.Buffered(1))


def _step_rows(total):
    for t in (384, 256, 128):
        if total % t == 0:
            return t
    raise ValueError("padded prompt length must be a multiple of 128")


def kernel(x_prompt, x_sample, cache_k_win, cache_v_win, state_gla, meta_tokens, w_in, w_a2, b_a, attn_sink,
           gla_norm_g, w_proj_a, w_proj_b, w_out, ln1_g, ln1_b, w_up, w_down, ln2_g, ln2_b):
    depth = w_in.shape[0]
    bsz, seq, _ = x_prompt.shape
    nsmp, dec_seq, _ = x_sample.shape
    assert dec_seq == 1 and cache_k_win.shape[2] == BLOCK and seq % BLOCK == 0
    alpha = (2 * depth) ** 0.25
    lp = seq + BLOCK
    ts = _step_rows(lp)
    n_steps = lp // ts
    rows = bsz * lp
    ffn_tile = next(t for t in (512, 384, 256, 128) if rows % t == 0)
    last_tile = next(t for t in (512, 384, 256, 128) if seq % t == 0)
    group = SUBLANES
    assert nsmp % group == 0

    w_in_r = jnp.swapaxes(w_in, 1, 2).astype(BF16)
    w_a2_p = jnp.concatenate([w_a2, jnp.zeros((depth, LANES - GLA_RANK, GK_W), w_a2.dtype)], axis=1).astype(BF16)
    w_pa, w_pb, w_o = w_proj_a.astype(BF16), w_proj_b.astype(BF16), w_out.astype(BF16)
    w_u, w_d = w_up.astype(BF16), w_down.astype(BF16)
    b_a3 = b_a.reshape(depth, 1, GK_W)
    gn3 = gla_norm_g.reshape(depth, 1, GLA_DV)
    g1, b1 = ln1_g.reshape(depth, 1, D_MODEL), ln1_b.reshape(depth, 1, D_MODEL)
    g2, b2 = ln2_g.reshape(depth, 1, D_MODEL), ln2_b.reshape(depth, 1, D_MODEL)
    sink_lanes = jnp.broadcast_to(attn_sink[:, :, None], (depth, N_Q_HEADS, LANES))

    cos_p, slo_p, shi_p = _rope_tables(jnp.arange(lp) - META_PAD)
    cos_s, slo_s, shi_s = (jnp.broadcast_to(t, (nsmp, LANES)) for t in _rope_tables(PAST_LEN + jnp.arange(1)))

    meta_block = jnp.concatenate([jnp.zeros((META_PAD, D_MODEL), x_prompt.dtype),
                                  meta_tokens.astype(x_prompt.dtype)], axis=0)
    nblk = ts // BLOCK
    xp = None
    xs = x_sample.reshape(nsmp, D_MODEL)
    ck = jnp.transpose(cache_k_win, (0, 1, 3, 4, 2))
    cv = jnp.transpose(cache_v_win, (0, 1, 3, 4, 2))

    cparams = functools.partial(pltpu.CompilerParams, vmem_limit_bytes=VMEM_LIMIT)
    pk, pv, pst = [], [], []
    stacked = []
    for l in range(depth):
        step_spec = pl.BlockSpec((1, ts, D_MODEL), lambda b, s: (b, s, 0))
        tab_spec = pl.BlockSpec((ts, LANES), lambda b, s: (s, 0))
        if l == 0:
            x_specs = [_const_spec((BLOCK, D_MODEL))] + [
                pl.BlockSpec((1, BLOCK, D_MODEL), lambda b, s, j=j: (b, jnp.maximum(nblk * s + j - 1, 0), 0))
                for j in range(nblk)]
            x_args = [meta_block] + [x_prompt] * nblk
        else:
            x_specs, x_args = [step_spec], [xp]
        hp, kwin, vwin, sfin = pl.pallas_call(
            functools.partial(_mixer_prompt_kernel, ts=ts, n_steps=n_steps, alpha=alpha, from_tokens=(l == 0)),
            grid=(bsz, n_steps),
            in_specs=x_specs + [tab_spec, tab_spec, tab_spec,
                      _const_spec((W_IN_COLS, D_MODEL), l), _const_spec((LANES, GK_W), l), _const_spec((1, GK_W), l),
                      pl.BlockSpec(memory_space=pltpu.SMEM), _const_spec((1, GLA_DV), l),
                      _const_spec((ATT_W, D_MODEL), l), _const_spec((GV_W, D_MODEL), l),
                      _const_spec((D_MODEL, D_MODEL), l), _const_spec((1, D_MODEL), l), _const_spec((1, D_MODEL), l)],
            out_specs=[step_spec,
                       pl.BlockSpec((1, BLOCK, KV_W), lambda b, s: (b, 0, 0)),
                       pl.BlockSpec((1, BLOCK, KV_W), lambda b, s: (b, 0, 0)),
                       pl.BlockSpec((1, GLA_HEADS, GLA_DK, GLA_DV), lambda b, s: (b, 0, 0, 0))],
            out_shape=[jax.ShapeDtypeStruct((bsz, lp, D_MODEL), F32),
                       jax.ShapeDtypeStruct((bsz, BLOCK, KV_W), F32),
                       jax.ShapeDtypeStruct((bsz, BLOCK, KV_W), F32),
                       jax.ShapeDtypeStruct((bsz, GLA_HEADS, GLA_DK, GLA_DV), F32)],
            scratch_shapes=[pltpu.VMEM((BLOCK + ts, KV_W), BF16), pltpu.VMEM((BLOCK + ts, KV_W), BF16),
                            pltpu.VMEM((GK_W, GV_W), F32), pltpu.VMEM((GK_W, GV_W), F32),
                            pltpu.VMEM((ts, GK_W), F32), pltpu.VMEM((ts, GK_W), F32), pltpu.VMEM((ts, GV_W), F32),
                            pltpu.VMEM((ts, GK_W), F32), pltpu.VMEM((ts, GV_W), F32), pltpu.VMEM((ts, ATT_W), F32)],
            compiler_params=cparams(dimension_semantics=("arbitrary", "arbitrary")),
            name=f"mixer_prompt_{l}",
        )(*x_args, cos_p, slo_p, shi_p, w_in_r, w_a2_p, b_a3, attn_sink[l], gn3, w_pa, w_pb, w_o, g1, b1)
        pk.append(kwin.reshape(bsz, N_KV_HEADS, HEAD_DIM, BLOCK))
        pv.append(vwin.reshape(bsz, N_KV_HEADS, HEAD_DIM, BLOCK))
        pst.append(sfin)

        ffn = functools.partial(_ffn_kernel, alpha=alpha, col_chunk=1024)
        ffn_w = [_const_spec((D_MODEL, D_FF), l), _const_spec((D_FF, D_MODEL), l),
                 _const_spec((1, D_MODEL), l), _const_spec((1, D_MODEL), l)]
        if l < depth - 1:
            xp = pl.pallas_call(
                ffn, grid=(rows // ffn_tile,),
                in_specs=[pl.BlockSpec((ffn_tile, D_MODEL), lambda i: (i, 0))] + ffn_w,
                out_specs=pl.BlockSpec((ffn_tile, D_MODEL), lambda i: (i, 0)),
                out_shape=jax.ShapeDtypeStruct((rows, D_MODEL), F32),
                compiler_params=cparams(dimension_semantics=("arbitrary",)),
                name=f"ffn_prompt_{l}",
            )(hp.reshape(rows, D_MODEL), w_u, w_d, g2, b2).reshape(bsz, lp, D_MODEL)
        else:
            pieces = last_tile // BLOCK
            y_prompt = pl.pallas_call(
                ffn, grid=(bsz, seq // last_tile),
                in_specs=[pl.BlockSpec((1, BLOCK, D_MODEL), lambda b, i, j=j: (b, 1 + pieces * i + j, 0))
                          for j in range(pieces)] + ffn_w,
                out_specs=pl.BlockSpec((1, last_tile, D_MODEL), lambda b, i: (b, i, 0)),
                out_shape=jax.ShapeDtypeStruct((bsz, seq, D_MODEL), F32),
                compiler_params=cparams(dimension_semantics=("arbitrary", "arbitrary")),
                name=f"ffn_prompt_{l}",
            )(*([hp] * pieces), w_u, w_d, g2, b2)

        proj_out = [(nsmp, C_GQ), (nsmp, GV_W), (nsmp, W_IN_COLS - C_GR),
                    (2 * N_SPLIT, KV_W, nsmp), (3 * N_SPLIT, GK_W, nsmp)]
        qkv, gv_s, gate, kvt, gcol = pl.pallas_call(
            _sample_proj_kernel, grid=(1,),
            in_specs=[_const_spec((nsmp, D_MODEL)), _const_spec((nsmp, LANES)), _const_spec((nsmp, LANES)),
                      _const_spec((nsmp, LANES)), _const_spec((W_IN_COLS, D_MODEL), l),
                      _const_spec((LANES, GK_W), l), _const_spec((1, GK_W), l)],
            out_specs=[_const_spec(s) for s in proj_out],
            out_shape=[jax.ShapeDtypeStruct(s, F32) for s in proj_out],
            compiler_params=cparams(dimension_semantics=("arbitrary",)),
            name=f"sample_proj_{l}",
        )(xs, cos_s, slo_s, shi_s, w_in_r, w_a2_p, b_a3)
        def by_group(t):
            t = t.reshape(t.shape[0] // N_SPLIT, N_SPLIT, t.shape[1], nsmp // group, group)
            return jnp.transpose(t, (3, 0, 2, 1, 4)).reshape(nsmp // group, t.shape[0], t.shape[2], N_SPLIT * group)
        col_spec = lambda n, width: pl.BlockSpec((None, n, width, N_SPLIT * group), lambda i: (i, 0, 0, 0))

        grp = lambda width: pl.BlockSpec((group, width), lambda i: (i, 0))
        cache_spec = pl.BlockSpec((None, group, N_KV_HEADS, HEAD_DIM, BLOCK), lambda i: (l, i, 0, 0, 0))
        state_spec = pl.BlockSpec((None, group, GLA_HEADS, GLA_DK, GLA_DV), lambda i: (l, i, 0, 0, 0))
        n_mix_in = 8
        att_s, o_s, *stacked = pl.pallas_call(
            functools.partial(_sample_mix_kernel, group=group), grid=(nsmp // group,),
            in_specs=[grp(C_GQ), grp(GV_W), col_spec(2, KV_W), col_spec(3, GK_W), cache_spec, cache_spec, state_spec,
                      _const_spec((N_Q_HEADS, LANES), l)] + [pl.BlockSpec(memory_space=pl.ANY)] * len(stacked),
            out_specs=[grp(ATT_W), grp(GV_W), cache_spec, cache_spec, state_spec],
            out_shape=[jax.ShapeDtypeStruct((nsmp, ATT_W), F32), jax.ShapeDtypeStruct((nsmp, GV_W), F32),
                       jax.ShapeDtypeStruct(ck.shape, F32), jax.ShapeDtypeStruct(cv.shape, F32),
                       jax.ShapeDtypeStruct(state_gla.shape, F32)],
            input_output_aliases={n_mix_in + i: 2 + i for i in range(len(stacked))},
            compiler_params=cparams(dimension_semantics=("arbitrary",)),
            name=f"sample_mix_{l}",
        )(qkv, gv_s, by_group(kvt), by_group(gcol), ck, cv, state_gla, sink_lanes, *stacked)

        hs = pl.pallas_call(
            functools.partial(_sample_finish_kernel, alpha=alpha), grid=(1,),
            in_specs=[_const_spec((nsmp, D_MODEL)), _const_spec((nsmp, ATT_W)), _const_spec((nsmp, GV_W)),
                      _const_spec((nsmp, W_IN_COLS - C_GR)), _const_spec((1, GLA_DV), l),
                      _const_spec((ATT_W, D_MODEL), l), _const_spec((GV_W, D_MODEL), l),
                      _const_spec((D_MODEL, D_MODEL), l), _const_spec((1, D_MODEL), l), _const_spec((1, D_MODEL), l)],
            out_specs=_const_spec((nsmp, D_MODEL)),
            out_shape=jax.ShapeDtypeStruct((nsmp, D_MODEL), F32),
            compiler_params=cparams(dimension_semantics=("arbitrary",)),
            name=f"sample_finish_{l}",
        )(xs, att_s, o_s, gate, gn3, w_pa, w_pb, w_o, g1, b1)
        xs = pl.pallas_call(
            ffn, grid=(1,),
            in_specs=[_const_spec((nsmp, D_MODEL))] + ffn_w,
            out_specs=_const_spec((nsmp, D_MODEL)),
            out_shape=jax.ShapeDtypeStruct((nsmp, D_MODEL), F32),
            compiler_params=cparams(dimension_semantics=("arbitrary",)),
            name=f"ffn_sample_{l}",
        )(hs, w_u, w_d, g2, b2)

    y_sample = xs.reshape(nsmp, 1, D_MODEL)
    to_rows = lambda t: jnp.transpose(t, (0, 1, 4, 2, 3))
    return (y_prompt, y_sample, to_rows(jnp.stack(pk)), to_rows(jnp.stack(pv)), jnp.stack(pst),
            to_rows(stacked[0]), to_rows(stacked[1]), stacked[2])
```

```python
import functools

import jax
import jax.numpy as jnp
from jax import lax
from jax.experimental import pallas as pl
from jax.experimental.pallas import tpu as pltpu

F32 = jnp.float32
BF16 = jnp.bfloat16

D_MODEL = 1024
PAST_LEN = 8192
N_META = 16
BLOCK = 128
META_PAD = BLOCK - N_META
HEAD_DIM = 64
N_Q_HEADS = 8
N_KV_HEADS = 2
Q_PER_KV = N_Q_HEADS // N_KV_HEADS
ROT_DIM = HEAD_DIM // 4
ROPE_THETA = 500000.0
GLA_HEADS = 4
GLA_DK = 64
GLA_DV = 128
GLA_RANK = 16
GLA_TAU = 16.0
D_FF = 4 * D_MODEL
ATT_W = N_Q_HEADS * HEAD_DIM
KV_W = N_KV_HEADS * HEAD_DIM
GK_W = GLA_HEADS * GLA_DK
GV_W = GLA_HEADS * GLA_DV
LN_EPS = 1e-5
RMS_EPS = 1e-6
LANES = 128
SUBLANES = 8
VMEM_LIMIT = 56 * 1024 * 1024

C_Q = 0
C_K = C_Q + ATT_W
C_V = C_K + KV_W
C_GQ = C_V + KV_W
C_GK = C_GQ + GK_W
C_GV = C_GK + GK_W
C_LR = C_GV + GV_W
C_GR = C_LR + GLA_RANK
C_GA = C_GR + GV_W
C_GB = C_GA + D_MODEL
W_IN_COLS = C_GB + D_MODEL
C_LR_END = C_LR + LANES

GLA_FAST_MAX_DECAY = 40.0


def _sigmoid(x):
    return 1.0 / (1.0 + jnp.exp(-x))


def _layer_norm(y, g, b):
    mu = jnp.mean(y, axis=-1, keepdims=True)
    yc = y - mu
    var = jnp.mean(yc * yc, axis=-1, keepdims=True)
    return yc * lax.rsqrt(var + LN_EPS) * g + b


def _rope(t, cos, sin_lo, sin_hi):
    outs = []
    for j in range(t.shape[1] // LANES):
        tj = t[:, j * LANES:(j + 1) * LANES]
        outs.append(tj * cos + pltpu.roll(tj, LANES - ROT_DIM // 2, 1) * sin_lo
                    + pltpu.roll(tj, ROT_DIM // 2, 1) * sin_hi)
    return outs[0] if len(outs) == 1 else jnp.concatenate(outs, axis=1)


def _in_proj(xb, w_t_ref, lo, hi):
    return lax.dot_general(xb, w_t_ref[lo:hi, :], (((1,), (1,)), ((), ())), preferred_element_type=F32)


def _log_decay(glr, w_a2, b_a):
    z = jnp.dot(glr.astype(BF16), w_a2, preferred_element_type=F32) + b_a
    return (jnp.minimum(z, 0.0) - jnp.log1p(jnp.exp(-jnp.abs(z)))) * (1.0 / GLA_TAU)


def _row_to_col(row):
    n = row.shape[1]
    eye = lax.broadcasted_iota(jnp.int32, (n, n), 0) == lax.broadcasted_iota(jnp.int32, (n, n), 1)
    return jnp.sum(jnp.where(eye, jnp.broadcast_to(row, (n, n)), 0.0), axis=1, keepdims=True)


N_SPLIT = 3


def _split3_bf16(a):
    hi = a.astype(BF16)
    r = a - hi.astype(F32)
    mid = r.astype(BF16)
    lo = (r - mid.astype(F32)).astype(BF16)
    return hi, mid, lo


def _swish(x):
    return x * _sigmoid(x)


def _gla_gate_out(o, swish_gr, gn):
    outs = []
    for h in range(GLA_HEADS):
        oh = o[:, h * GLA_DV:(h + 1) * GLA_DV]
        ms = jnp.mean(oh * oh, axis=-1, keepdims=True)
        outs.append(oh * lax.rsqrt(ms + RMS_EPS) * gn)
    return jnp.concatenate(outs, axis=1) * swish_gr


def _finish(x, pa, gla, sig_a, sig_b, w_pb, w_out, g1, b1, alpha):
    pb = jnp.dot(gla.astype(BF16), w_pb, preferred_element_type=F32)
    m = sig_a * pa + sig_b * pb
    y = alpha * x + jnp.dot(m.astype(BF16), w_out, preferred_element_type=F32)
    return _layer_norm(y, g1, b1)


def _mixer_prompt_kernel(*refs, ts, n_steps, alpha, from_tokens):
    s = pl.program_id(1)
    nblk = ts // BLOCK
    n_x = 1 + nblk if from_tokens else 1
    x_refs = refs[:n_x]
    (cos_ref, slo_ref, shi_ref, w_in_ref, w_a2_ref, b_a_ref, sink_ref, gn_ref,
     w_pa_ref, w_pb_ref, w_out_ref, g1_ref, b1_ref,
     h_ref, kwin_ref, vwin_ref, sfin_ref,
     kprev_scr, vprev_scr, s_scr, sprev_scr, gq_scr, gk_scr, gv_scr, la_scr, o_scr, att_t_scr, gate_scr) = refs[n_x:]

    def load_x():
        if not from_tokens:
            return x_refs[0][0]
        blocks = [r[0] for r in x_refs[1:]]
        blocks[0] = jnp.where(s == 0, x_refs[0][...], blocks[0])
        return jnp.concatenate(blocks, axis=0)

    @pl.when(s == 0)
    def _():
        kprev_scr[...] = jnp.zeros((N_KV_HEADS, BLOCK, KV_W), BF16)
        vprev_scr[...] = jnp.zeros((KV_W, BLOCK), BF16)
        s_scr[...] = jnp.zeros(s_scr.shape, F32)

    x = load_x()
    xb = x.astype(BF16)
    proj = functools.partial(_in_proj, xb, w_in_ref)

    live = (s * ts + lax.broadcasted_iota(jnp.int32, (ts, 1), 0)) >= META_PAD
    la = jnp.where(live, _log_decay(proj(C_LR, C_LR_END), w_a2_ref[...], b_a_ref[...]), 0.0)
    cos, slo, shi = cos_ref[...], slo_ref[...], shi_ref[...]
    q = _rope(proj(C_Q, C_K), cos, slo, shi)
    k = _rope(proj(C_K, C_V), cos, slo, shi)
    v = proj(C_V, C_GQ)
    gq = proj(C_GQ, C_GK) * (GLA_DK ** -0.5)
    gk = jnp.where(live, proj(C_GK, C_GV), 0.0)
    gv = jnp.where(live, proj(C_GV, C_LR), 0.0)
    v_t = v.T

    @pl.when(s == n_steps - 1)
    def _():
        kwin_ref[0] = k[ts - BLOCK:, :].T
        vwin_ref[0] = v_t[:, ts - BLOCK:]

    gate_w = W_IN_COLS - C_GR
    n_gate_pieces = N_KV_HEADS * nblk
    gate_edges = [round(i * gate_w / n_gate_pieces / LANES) * LANES for i in range(n_gate_pieces + 1)]

    def gate_piece_matmul(i):
        return proj(C_GR + gate_edges[i], C_GR + gate_edges[i + 1])

    def gate_piece_store(i, val):
        lo, hi = gate_edges[i], gate_edges[i + 1]
        mid = min(max(GV_W, lo), hi)
        if mid > lo:
            gate_scr[:, lo:mid] = _swish(val[:, :mid - lo])
        if hi > mid:
            gate_scr[:, mid:hi] = _sigmoid(val[:, mid - lo:])

    q_bf = (q * (HEAD_DIM ** -0.5)).astype(BF16)
    low_half = lax.broadcasted_iota(jnp.int32, (1, KV_W), 1) < HEAD_DIM
    k_swapped = pltpu.roll(k, HEAD_DIM, 1)
    k_dup = [jnp.where(low_half, k, k_swapped).astype(BF16), jnp.where(low_half, k_swapped, k).astype(BF16)]
    k_keys = [jnp.concatenate([kprev_scr[i], k_dup[i]], axis=0) for i in range(N_KV_HEADS)]
    vt_bf = v_t.astype(BF16)
    vt_keys = jnp.concatenate([vprev_scr[...], vt_bf], axis=1)
    kj = lax.broadcasted_iota(jnp.int32, (2 * BLOCK, Q_PER_KV * BLOCK), 0)
    qi = lax.broadcasted_iota(jnp.int32, (2 * BLOCK, Q_PER_KV * BLOCK), 1) % BLOCK
    band = (kj - qi >= 1) & (kj - qi <= BLOCK)
    q_low_half = lax.broadcasted_iota(jnp.int32, (BLOCK, LANES), 1) < HEAD_DIM
    pieces_done = 0
    for blk in range(nblk):
        first_key_slot = (s * nblk + blk - 1) * BLOCK
        valid = band & (kj + first_key_slot >= META_PAD)
        r0 = blk * BLOCK
        for kv in range(N_KV_HEADS):
            heads = [kv * Q_PER_KV + g for g in range(Q_PER_KV)]
            q_rows = []
            for hq in heads:
                grp = q_bf[r0:r0 + BLOCK, (hq // 2) * LANES:(hq // 2 + 1) * LANES]
                own = q_low_half if hq % 2 == 0 else jnp.logical_not(q_low_half)
                q_rows.append(jnp.where(own, grp, jnp.zeros_like(grp)))
            st = lax.dot_general(k_keys[kv][r0:r0 + 2 * BLOCK, :], jnp.concatenate(q_rows, axis=0),
                                 (((1,), (1,)), ((), ())), preferred_element_type=F32)
            gate_val = gate_piece_matmul(pieces_done) if pieces_done < n_gate_pieces else None
            st = jnp.where(valid, st, -jnp.inf)
            sink_row = jnp.concatenate([jnp.full((1, BLOCK), sink_ref[hq], F32) for hq in heads], axis=1)
            m = jnp.maximum(jnp.max(st, axis=0, keepdims=True), sink_row)
            p = jnp.exp(st - m)
            den = jnp.sum(p, axis=0, keepdims=True) + jnp.exp(sink_row - m)
            ot = jnp.dot(vt_keys[kv * HEAD_DIM:(kv + 1) * HEAD_DIM, r0:r0 + 2 * BLOCK], p.astype(BF16),
                         preferred_element_type=F32) * (1.0 / den)
            for g, hq in enumerate(heads):
                att_t_scr[hq * HEAD_DIM:(hq + 1) * HEAD_DIM, r0:r0 + BLOCK] = ot[:, g * BLOCK:(g + 1) * BLOCK]
            if gate_val is not None:
                gate_piece_store(pieces_done, gate_val)
                pieces_done += 1
    for i in range(pieces_done, n_gate_pieces):
        gate_piece_store(i, gate_piece_matmul(i))
    for i in range(N_KV_HEADS):
        kprev_scr[i] = k_dup[i][ts - BLOCK:, :]
    vprev_scr[...] = vt_bf[:, ts - BLOCK:]

    tri = (lax.broadcasted_iota(jnp.int32, (BLOCK, BLOCK), 1)
           <= lax.broadcasted_iota(jnp.int32, (BLOCK, BLOCK), 0))
    tri_bf = jnp.where(tri, 1.0, 0.0).astype(BF16)
    cums = []
    worst = jnp.zeros((1, GK_W), F32)
    for c in range(nblk):
        parts = jnp.concatenate(_split3_bf16(la[c * BLOCK:(c + 1) * BLOCK]), axis=1)
        b3 = jnp.dot(tri_bf, parts, preferred_element_type=F32)
        b = b3[:, 0:GK_W] + b3[:, GK_W:2 * GK_W] + b3[:, 2 * GK_W:3 * GK_W]
        cums.append(b)
        worst = jnp.maximum(worst, -b[BLOCK - 1:BLOCK, :])
    fast_ok = jnp.max(worst) <= GLA_FAST_MAX_DECAY

    head_of_k = lax.broadcasted_iota(jnp.int32, (BLOCK, GK_W), 1) // GLA_DK
    head_of_v = lax.broadcasted_iota(jnp.int32, (BLOCK, GV_W), 1) // GLA_DV
    state_diag = (lax.broadcasted_iota(jnp.int32, (GK_W, GV_W), 0) // GLA_DK
                  == lax.broadcasted_iota(jnp.int32, (GK_W, GV_W), 1) // GLA_DV)

    def proj_att():
        return jnp.dot(att_t_scr[...].T.astype(BF16), w_pa_ref[...], preferred_element_type=F32)

    def gla_fast():
        t_idx = lax.broadcasted_iota(jnp.int32, (BLOCK, GLA_HEADS * BLOCK), 0)
        j_idx = lax.broadcasted_iota(jnp.int32, (BLOCK, GLA_HEADS * BLOCK), 1) % BLOCK
        causal = j_idx <= t_idx
        per_chunk = []
        for c in range(nblk):
            rows = slice(c * BLOCK, (c + 1) * BLOCK)
            b = cums[c]
            b_last = b[BLOCK - 1:BLOCK, :]
            kc = gk[rows]
            q_dec = (gq[rows] * jnp.exp(b)).astype(BF16)
            k_inv = (kc * jnp.exp(-b)).astype(BF16)
            k_end_t = (kc * jnp.exp(b_last - b)).T.astype(BF16)
            vc = gv[rows].astype(BF16)
            zk = jnp.zeros_like(k_inv)
            k_bd = jnp.concatenate([jnp.where(head_of_k == h, k_inv, zk) for h in range(GLA_HEADS)], axis=0)
            a = lax.dot_general(q_dec, k_bd, (((1,), (1,)), ((), ())), preferred_element_type=F32)
            a = jnp.where(causal, a, 0.0).astype(BF16)
            zv = jnp.zeros_like(vc)
            v_bd = jnp.concatenate([jnp.where(head_of_v == h, vc, zv) for h in range(GLA_HEADS)], axis=0)
            o_intra = jnp.dot(a, v_bd, preferred_element_type=F32)
            ds = jnp.where(state_diag, jnp.dot(k_end_t, vc, preferred_element_type=F32), 0.0)
            per_chunk.append((rows, q_dec, o_intra, ds, _row_to_col(jnp.exp(b_last))))
        pa = proj_att()
        for rows, q_dec, o_intra, ds, decay_col in per_chunk:
            s0 = s_scr[...]
            o_scr[rows, :] = o_intra + jnp.dot(q_dec, s0.astype(BF16), preferred_element_type=F32)
            s_scr[...] = decay_col * s0 + ds
        return pa

    def gla_slow():
        def body(i, carry):
            rows = pl.ds(pl.multiple_of(i * SUBLANES, SUBLANES), SUBLANES)
            la8, k8, q8, v8 = la_scr[rows, :], gk_scr[rows, :], gq_scr[rows, :], gv_scr[rows, :]
            outs = []
            for r in range(SUBLANES):
                a_col = _row_to_col(jnp.exp(la8[r:r + 1]))
                k_col = _row_to_col(k8[r:r + 1])
                q_col = _row_to_col(q8[r:r + 1])
                s1 = a_col * s_scr[...] + jnp.where(state_diag, k_col * v8[r:r + 1], 0.0)
                s_scr[...] = s1
                outs.append(jnp.sum(q_col * s1, axis=0, keepdims=True))
            o_scr[rows, :] = jnp.concatenate(outs, axis=0)
            return carry
        lax.fori_loop(0, ts // SUBLANES, body, 0)

    def finish_step(x_val, pa):
        gla = _gla_gate_out(o_scr[...], gate_scr[:, 0:GV_W], gn_ref[...])
        h_ref[0] = _finish(x_val, pa, gla, gate_scr[:, GV_W:GV_W + D_MODEL], gate_scr[:, GV_W + D_MODEL:],
                           w_pb_ref[...], w_out_ref[...], g1_ref[...], b1_ref[...], alpha)

    sprev_scr[...] = s_scr[...]
    finish_step(x, gla_fast())

    @pl.when(jnp.logical_not(fast_ok))
    def _():
        x_again = load_x()
        proj_again = functools.partial(_in_proj, x_again.astype(BF16), w_in_ref)
        gq_scr[...] = proj_again(C_GQ, C_GK) * (GLA_DK ** -0.5)
        gk_scr[...] = jnp.where(live, proj_again(C_GK, C_GV), 0.0)
        gv_scr[...] = jnp.where(live, proj_again(C_GV, C_LR), 0.0)
        la_scr[...] = jnp.where(live, _log_decay(proj_again(C_LR, C_LR_END), w_a2_ref[...], b_a_ref[...]), 0.0)
        s_scr[...] = sprev_scr[...]
        gla_slow()
        finish_step(x_again, proj_att())

    @pl.when(s == n_steps - 1)
    def _():
        for h in range(GLA_HEADS):
            sfin_ref[0, h] = s_scr[h * GLA_DK:(h + 1) * GLA_DK, h * GLA_DV:(h + 1) * GLA_DV]


def _ffn_kernel(*refs, alpha, col_chunk):
    h_refs, (w_up_ref, w_dn_ref, g_ref, b_ref, o_ref) = refs[:-5], refs[-5:]
    if len(h_refs) == 1:
        h = h_refs[0][...]
    else:
        h = jnp.concatenate([r[0] for r in h_refs], axis=0)
    hb = h.astype(BF16)
    acc = jnp.zeros(h.shape, F32)
    for c in range(D_FF // col_chunk):
        u = jnp.dot(hb, w_up_ref[:, c * col_chunk:(c + 1) * col_chunk], preferred_element_type=F32)
        u = jnp.maximum(u, 0.0)
        acc = acc + jnp.dot((u * u).astype(BF16), w_dn_ref[c * col_chunk:(c + 1) * col_chunk, :],
                            preferred_element_type=F32)
    o_ref[...] = _layer_norm(alpha * h + acc, g_ref[...], b_ref[...]).reshape(o_ref.shape)


def _sample_proj_kernel(x_ref, cos_ref, slo_ref, shi_ref, w_in_ref, w_a2_ref, b_a_ref,
                        qkv_ref, gv_ref, gate_ref, kvt_ref, gcol_ref):
    proj = functools.partial(_in_proj, x_ref[...].astype(BF16), w_in_ref)

    cos, slo, shi = cos_ref[...], slo_ref[...], shi_ref[...]
    k = _rope(proj(C_K, C_V), cos, slo, shi)
    v = proj(C_V, C_GQ)
    qkv_ref[:, C_Q:C_K] = _rope(proj(C_Q, C_K), cos, slo, shi)
    qkv_ref[:, C_K:C_V] = k
    qkv_ref[:, C_V:C_GQ] = v
    gv_ref[...] = proj(C_GV, C_LR)
    gate_ref[...] = proj(C_GR, W_IN_COLS)
    def store_planes(ref, i, t):
        for p, part in enumerate(_split3_bf16(t)):
            ref[N_SPLIT * i + p] = part.astype(F32).T

    store_planes(kvt_ref, 0, k)
    store_planes(kvt_ref, 1, v)
    store_planes(gcol_ref, 0, jnp.exp(_log_decay(proj(C_LR, C_LR_END), w_a2_ref[...], b_a_ref[...])))
    store_planes(gcol_ref, 1, proj(C_GQ, C_GK) * (GLA_DK ** -0.5))
    store_planes(gcol_ref, 2, proj(C_GK, C_GV))


def _sample_mix_kernel(qkv_ref, gv_ref, kvt_ref, gcol_ref, ck_ref, cv_ref, st_ref, sink_ref, *rest, group):
    att_ref, o_ref, nk_ref, nv_ref, nst_ref = rest[-5:]
    head_row = lax.broadcasted_iota(jnp.int32, (N_Q_HEADS, ATT_W), 0)
    head_lane = lax.broadcasted_iota(jnp.int32, (N_Q_HEADS, ATT_W), 1) // HEAD_DIM
    own = head_row == head_lane
    r8 = lax.broadcasted_iota(jnp.int32, (N_Q_HEADS, KV_W), 0)
    swap = (r8 % 2) != (r8 // Q_PER_KV)
    key_i = lax.broadcasted_iota(jnp.int32, (N_Q_HEADS, BLOCK), 1)
    last_row = lax.broadcasted_iota(jnp.int32, (KV_W, BLOCK), 1) == BLOCK - 1
    sink = sink_ref[...][:, 0:1]

    qkv8 = qkv_ref[...]
    gv8 = gv_ref[...]
    sel = (lax.broadcasted_iota(jnp.int32, (N_SPLIT * group, group * LANES), 0) % group
           == lax.broadcasted_iota(jnp.int32, (N_SPLIT * group, group * LANES), 1) // LANES)
    sel = jnp.where(sel, 1.0, 0.0).astype(BF16)
    spread = lambda ref, i: jnp.dot(ref[i].astype(BF16), sel, preferred_element_type=F32)
    k_cols, v_cols = spread(kvt_ref, 0), spread(kvt_ref, 1)
    a_cols, q_cols, k_gla_cols = (spread(gcol_ref, i) for i in range(3))
    att_rows, o_rows = [], []
    for j in range(group):
        q_row = qkv8[j:j + 1, C_Q:C_K]
        k_new = qkv8[j:j + 1, C_K:C_V]
        v_new = qkv8[j:j + 1, C_V:C_GQ]
        k_old = ck_ref[j].reshape(KV_W, BLOCK)
        v_old = cv_ref[j].reshape(KV_W, BLOCK)
        lanes_j = slice(j * LANES, (j + 1) * LANES)
        nk_ref[j] = jnp.where(last_row, k_cols[:, lanes_j], pltpu.roll(k_old, BLOCK - 1, 1)).reshape(
            N_KV_HEADS, HEAD_DIM, BLOCK)
        nv_ref[j] = jnp.where(last_row, v_cols[:, lanes_j], pltpu.roll(v_old, BLOCK - 1, 1)).reshape(
            N_KV_HEADS, HEAD_DIM, BLOCK)

        qm = jnp.where(own, jnp.broadcast_to(q_row, (N_Q_HEADS, ATT_W)), 0.0)
        fold = qm[:, 0:128] + qm[:, 128:256] + qm[:, 256:384] + qm[:, 384:512]
        q8 = (jnp.where(swap, pltpu.roll(fold, HEAD_DIM, 1), fold) * (HEAD_DIM ** -0.5)).astype(BF16)
        sc = jnp.dot(q8, k_old.astype(BF16), preferred_element_type=F32)
        sc = jnp.where(key_i >= 1, sc, -jnp.inf)
        s_new = jnp.sum(q8.astype(F32) * k_new.astype(BF16).astype(F32), axis=-1, keepdims=True)
        m = jnp.maximum(jnp.maximum(jnp.max(sc, axis=-1, keepdims=True), s_new), sink)
        p = jnp.exp(sc - m)
        p_new = jnp.exp(s_new - m)
        den = jnp.sum(p, axis=-1, keepdims=True) + p_new + jnp.exp(sink - m)
        o8 = (lax.dot_general(p.astype(BF16), v_old.astype(BF16), (((1,), (1,)), ((), ())),
                              preferred_element_type=F32)
              + p_new.astype(BF16).astype(F32) * v_new.astype(BF16).astype(F32)) / den
        o8 = jnp.where(swap, pltpu.roll(o8, HEAD_DIM, 1), o8)
        o_wide = jnp.concatenate([o8, o8, o8, o8], axis=1)
        att_rows.append(jnp.sum(jnp.where(own, o_wide, 0.0), axis=0, keepdims=True))

        a_col, q_col, k_col = a_cols[:, lanes_j], q_cols[:, lanes_j], k_gla_cols[:, lanes_j]
        v_row = gv8[j:j + 1, :]
        v_exp = jnp.concatenate([jnp.broadcast_to(v_row[:, h * GLA_DV:(h + 1) * GLA_DV], (GLA_DK, GLA_DV))
                                 for h in range(GLA_HEADS)], axis=0)
        s1 = a_col * st_ref[j].reshape(GK_W, GLA_DV) + k_col * v_exp
        nst_ref[j] = s1.reshape(GLA_HEADS, GLA_DK, GLA_DV)
        qs = q_col * s1
        o_rows.append(jnp.concatenate(
            [jnp.sum(qs[h * GLA_DK:(h + 1) * GLA_DK], axis=0, keepdims=True) for h in range(GLA_HEADS)], axis=1))
    att_ref[...] = jnp.concatenate(att_rows, axis=0)
    o_ref[...] = jnp.concatenate(o_rows, axis=0)


def _sample_finish_kernel(x_ref, att_ref, o_ref, gate_ref, gn_ref, w_pa_ref, w_pb_ref, w_out_ref, g1_ref, b1_ref,
                          h_ref, *, alpha):
    gla = _gla_gate_out(o_ref[...], _swish(gate_ref[:, 0:GV_W]), gn_ref[...])
    pa = jnp.dot(att_ref[...].astype(BF16), w_pa_ref[...], preferred_element_type=F32)
    h_ref[...] = _finish(x_ref[...], pa, gla, _sigmoid(gate_ref[:, GV_W:GV_W + D_MODEL]),
                         _sigmoid(gate_ref[:, GV_W + D_MODEL:]), w_pb_ref[...], w_out_ref[...],
                         g1_ref[...], b1_ref[...], alpha)


def _rope_tables(pos):
    half = ROT_DIM // 2
    inv = ROPE_THETA ** (-jnp.arange(half, dtype=F32) * 2.0 / ROT_DIM)
    d = jnp.arange(LANES) % HEAD_DIM
    ang = pos.astype(F32)[:, None] * inv[d % half][None, :]
    cos, sin = jnp.cos(ang), jnp.sin(ang)
    cos_t = jnp.where(d < ROT_DIM, cos, 1.0)
    sin_lo = jnp.where(d < half, -sin, 0.0)
    sin_hi = jnp.where((d >= half) & (d < ROT_DIM), sin, 0.0)
    return cos_t, sin_lo, sin_hi


def _const_spec(shape, layer=None):
    if layer is None:
        return pl.BlockSpec(shape, lambda *_: (0,) * len(shape), pipeline_mode=pl.Buffered(1))
    return pl.BlockSpec((None,) + shape, lambda *_: (layer,) + (0,) * len(shape), pipeline_mode=pl.Buffered(1))


def _step_rows(total):
    for t in (384, 256, 128):
        if total % t == 0:
            return t
    raise ValueError("padded prompt length must be a multiple of 128")


def kernel(x_prompt, x_sample, cache_k_win, cache_v_win, state_gla, meta_tokens, w_in, w_a2, b_a, attn_sink,
           gla_norm_g, w_proj_a, w_proj_b, w_out, ln1_g, ln1_b, w_up, w_down, ln2_g, ln2_b):
    depth = w_in.shape[0]
    bsz, seq, _ = x_prompt.shape
    nsmp, dec_seq, _ = x_sample.shape
    assert dec_seq == 1 and cache_k_win.shape[2] == BLOCK and seq % BLOCK == 0
    alpha = (2 * depth) ** 0.25
    lp = seq + BLOCK
    ts = _step_rows(lp)
    n_steps = lp // ts
    rows = bsz * lp
    ffn_tile = next(t for t in (512, 384, 256, 128) if rows % t == 0)
    last_tile = next(t for t in (512, 384, 256, 128) if seq % t == 0)
    group = SUBLANES
    assert nsmp % group == 0

    w_in_r = jnp.swapaxes(w_in, 1, 2).astype(BF16)
    w_a2_p = jnp.concatenate([w_a2, jnp.zeros((depth, LANES - GLA_RANK, GK_W), w_a2.dtype)], axis=1).astype(BF16)
    w_pa, w_pb, w_o = w_proj_a.astype(BF16), w_proj_b.astype(BF16), w_out.astype(BF16)
    w_u, w_d = w_up.astype(BF16), w_down.astype(BF16)
    b_a3 = b_a.reshape(depth, 1, GK_W)
    gn3 = gla_norm_g.reshape(depth, 1, GLA_DV)
    g1, b1 = ln1_g.reshape(depth, 1, D_MODEL), ln1_b.reshape(depth, 1, D_MODEL)
    g2, b2 = ln2_g.reshape(depth, 1, D_MODEL), ln2_b.reshape(depth, 1, D_MODEL)
    sink_lanes = jnp.broadcast_to(attn_sink[:, :, None], (depth, N_Q_HEADS, LANES))

    cos_p, slo_p, shi_p = _rope_tables(jnp.arange(lp) - META_PAD)
    cos_s, slo_s, shi_s = (jnp.broadcast_to(t, (nsmp, LANES)) for t in _rope_tables(PAST_LEN + jnp.arange(1)))

    meta_block = jnp.concatenate([jnp.zeros((META_PAD, D_MODEL), x_prompt.dtype),
                                  meta_tokens.astype(x_prompt.dtype)], axis=0)
    nblk = ts // BLOCK
    xp = None
    xs = x_sample.reshape(nsmp, D_MODEL)
    ck = jnp.transpose(cache_k_win, (0, 1, 3, 4, 2))
    cv = jnp.transpose(cache_v_win, (0, 1, 3, 4, 2))

    cparams = functools.partial(pltpu.CompilerParams, vmem_limit_bytes=VMEM_LIMIT)
    pk, pv, pst = [], [], []
    stacked = []
    for l in range(depth):
        step_spec = pl.BlockSpec((1, ts, D_MODEL), lambda b, s: (b, s, 0))
        tab_spec = pl.BlockSpec((ts, LANES), lambda b, s: (s, 0))
        if l == 0:
            x_specs = [_const_spec((BLOCK, D_MODEL))] + [
                pl.BlockSpec((1, BLOCK, D_MODEL), lambda b, s, j=j: (b, jnp.maximum(nblk * s + j - 1, 0), 0))
                for j in range(nblk)]
            x_args = [meta_block] + [x_prompt] * nblk
        else:
            x_specs, x_args = [step_spec], [xp]
        hp, kwin, vwin, sfin = pl.pallas_call(
            functools.partial(_mixer_prompt_kernel, ts=ts, n_steps=n_steps, alpha=alpha, from_tokens=(l == 0)),
            grid=(bsz, n_steps),
            in_specs=x_specs + [tab_spec, tab_spec, tab_spec,
                      _const_spec((W_IN_COLS, D_MODEL), l), _const_spec((LANES, GK_W), l), _const_spec((1, GK_W), l),
                      pl.BlockSpec(memory_space=pltpu.SMEM), _const_spec((1, GLA_DV), l),
                      _const_spec((ATT_W, D_MODEL), l), _const_spec((GV_W, D_MODEL), l),
                      _const_spec((D_MODEL, D_MODEL), l), _const_spec((1, D_MODEL), l), _const_spec((1, D_MODEL), l)],
            out_specs=[step_spec,
                       pl.BlockSpec((1, BLOCK, KV_W), lambda b, s: (b, 0, 0)),
                       pl.BlockSpec((1, BLOCK, KV_W), lambda b, s: (b, 0, 0)),
                       pl.BlockSpec((1, GLA_HEADS, GLA_DK, GLA_DV), lambda b, s: (b, 0, 0, 0))],
            out_shape=[jax.ShapeDtypeStruct((bsz, lp, D_MODEL), F32),
                       jax.ShapeDtypeStruct((bsz, BLOCK, KV_W), F32),
                       jax.ShapeDtypeStruct((bsz, BLOCK, KV_W), F32),
                       jax.ShapeDtypeStruct((bsz, GLA_HEADS, GLA_DK, GLA_DV), F32)],
            scratch_shapes=[pltpu.VMEM((N_KV_HEADS, BLOCK, KV_W), BF16), pltpu.VMEM((KV_W, BLOCK), BF16),
                            pltpu.VMEM((GK_W, GV_W), F32), pltpu.VMEM((GK_W, GV_W), F32),
                            pltpu.VMEM((ts, GK_W), F32), pltpu.VMEM((ts, GK_W), F32), pltpu.VMEM((ts, GV_W), F32),
                            pltpu.VMEM((ts, GK_W), F32), pltpu.VMEM((ts, GV_W), F32), pltpu.VMEM((ATT_W, ts), F32),
                            pltpu.VMEM((ts, W_IN_COLS - C_GR), F32)],
            compiler_params=cparams(dimension_semantics=("arbitrary", "arbitrary")),
            name=f"mixer_prompt_{l}",
        )(*x_args, cos_p, slo_p, shi_p, w_in_r, w_a2_p, b_a3, attn_sink[l], gn3, w_pa, w_pb, w_o, g1, b1)
        pk.append(kwin.reshape(bsz, N_KV_HEADS, HEAD_DIM, BLOCK))
        pv.append(vwin.reshape(bsz, N_KV_HEADS, HEAD_DIM, BLOCK))
        pst.append(sfin)

        ffn = functools.partial(_ffn_kernel, alpha=alpha, col_chunk=1024)
        ffn_w = [_const_spec((D_MODEL, D_FF), l), _const_spec((D_FF, D_MODEL), l),
                 _const_spec((1, D_MODEL), l), _const_spec((1, D_MODEL), l)]
        if l < depth - 1:
            xp = pl.pallas_call(
                ffn, grid=(rows // ffn_tile,),
                in_specs=[pl.BlockSpec((ffn_tile, D_MODEL), lambda i: (i, 0))] + ffn_w,
                out_specs=pl.BlockSpec((ffn_tile, D_MODEL), lambda i: (i, 0)),
                out_shape=jax.ShapeDtypeStruct((rows, D_MODEL), F32),
                compiler_params=cparams(dimension_semantics=("arbitrary",)),
                name=f"ffn_prompt_{l}",
            )(hp.reshape(rows, D_MODEL), w_u, w_d, g2, b2).reshape(bsz, lp, D_MODEL)
        else:
            pieces = last_tile // BLOCK
            y_prompt = pl.pallas_call(
                ffn, grid=(bsz, seq // last_tile),
                in_specs=[pl.BlockSpec((1, BLOCK, D_MODEL), lambda b, i, j=j: (b, 1 + pieces * i + j, 0))
                          for j in range(pieces)] + ffn_w,
                out_specs=pl.BlockSpec((1, last_tile, D_MODEL), lambda b, i: (b, i, 0)),
                out_shape=jax.ShapeDtypeStruct((bsz, seq, D_MODEL), F32),
                compiler_params=cparams(dimension_semantics=("arbitrary", "arbitrary")),
                name=f"ffn_prompt_{l}",
            )(*([hp] * pieces), w_u, w_d, g2, b2)

        proj_out = [(nsmp, C_GQ), (nsmp, GV_W), (nsmp, W_IN_COLS - C_GR),
                    (2 * N_SPLIT, KV_W, nsmp), (3 * N_SPLIT, GK_W, nsmp)]
        qkv, gv_s, gate, kvt, gcol = pl.pallas_call(
            _sample_proj_kernel, grid=(1,),
            in_specs=[_const_spec((nsmp, D_MODEL)), _const_spec((nsmp, LANES)), _const_spec((nsmp, LANES)),
                      _const_spec((nsmp, LANES)), _const_spec((W_IN_COLS, D_MODEL), l),
                      _const_spec((LANES, GK_W), l), _const_spec((1, GK_W), l)],
            out_specs=[_const_spec(s) for s in proj_out],
            out_shape=[jax.ShapeDtypeStruct(s, F32) for s in proj_out],
            compiler_params=cparams(dimension_semantics=("arbitrary",)),
            name=f"sample_proj_{l}",
        )(xs, cos_s, slo_s, shi_s, w_in_r, w_a2_p, b_a3)
        def by_group(t):
            t = t.reshape(t.shape[0] // N_SPLIT, N_SPLIT, t.shape[1], nsmp // group, group)
            return jnp.transpose(t, (3, 0, 2, 1, 4)).reshape(nsmp // group, t.shape[0], t.shape[2], N_SPLIT * group)
        col_spec = lambda n, width: pl.BlockSpec((None, n, width, N_SPLIT * group), lambda i: (i, 0, 0, 0))

        grp = lambda width: pl.BlockSpec((group, width), lambda i: (i, 0))
        cache_spec = pl.BlockSpec((None, group, N_KV_HEADS, HEAD_DIM, BLOCK), lambda i: (l, i, 0, 0, 0))
        state_spec = pl.BlockSpec((None, group, GLA_HEADS, GLA_DK, GLA_DV), lambda i: (l, i, 0, 0, 0))
        n_mix_in = 8
        att_s, o_s, *stacked = pl.pallas_call(
            functools.partial(_sample_mix_kernel, group=group), grid=(nsmp // group,),
            in_specs=[grp(C_GQ), grp(GV_W), col_spec(2, KV_W), col_spec(3, GK_W), cache_spec, cache_spec, state_spec,
                      _const_spec((N_Q_HEADS, LANES), l)] + [pl.BlockSpec(memory_space=pl.ANY)] * len(stacked),
            out_specs=[grp(ATT_W), grp(GV_W), cache_spec, cache_spec, state_spec],
            out_shape=[jax.ShapeDtypeStruct((nsmp, ATT_W), F32), jax.ShapeDtypeStruct((nsmp, GV_W), F32),
                       jax.ShapeDtypeStruct(ck.shape, F32), jax.ShapeDtypeStruct(cv.shape, F32),
                       jax.ShapeDtypeStruct(state_gla.shape, F32)],
            input_output_aliases={n_mix_in + i: 2 + i for i in range(len(stacked))},
            compiler_params=cparams(dimension_semantics=("arbitrary",)),
            name=f"sample_mix_{l}",
        )(qkv, gv_s, by_group(kvt), by_group(gcol), ck, cv, state_gla, sink_lanes, *stacked)

        hs = pl.pallas_call(
            functools.partial(_sample_finish_kernel, alpha=alpha), grid=(1,),
            in_specs=[_const_spec((nsmp, D_MODEL)), _const_spec((nsmp, ATT_W)), _const_spec((nsmp, GV_W)),
                      _const_spec((nsmp, W_IN_COLS - C_GR)), _const_spec((1, GLA_DV), l),
                      _const_spec((ATT_W, D_MODEL), l), _const_spec((GV_W, D_MODEL), l),
                      _const_spec((D_MODEL, D_MODEL), l), _const_spec((1, D_MODEL), l), _const_spec((1, D_MODEL), l)],
            out_specs=_const_spec((nsmp, D_MODEL)),
            out_shape=jax.ShapeDtypeStruct((nsmp, D_MODEL), F32),
            compiler_params=cparams(dimension_semantics=("arbitrary",)),
            name=f"sample_finish_{l}",
        )(xs, att_s, o_s, gate, gn3, w_pa, w_pb, w_o, g1, b1)
        xs = pl.pallas_call(
            ffn, grid=(1,),
            in_specs=[_const_spec((nsmp, D_MODEL))] + ffn_w,
            out_specs=_const_spec((nsmp, D_MODEL)),
            out_shape=jax.ShapeDtypeStruct((nsmp, D_MODEL), F32),
            compiler_params=cparams(dimension_semantics=("arbitrary",)),
            name=f"ffn_sample_{l}",
        )(hs, w_u, w_d, g2, b2)

    y_sample = xs.reshape(nsmp, 1, D_MODEL)
    to_rows = lambda t: jnp.transpose(t, (0, 1, 4, 2, 3))
    return (y_prompt, y_sample, to_rows(jnp.stack(pk)), to_rows(jnp.stack(pv)), jnp.stack(pst),
            to_rows(stacked[0]), to_rows(stacked[1]), stacked[2])
```

```python
import functools

import jax
import jax.numpy as jnp
from jax import lax
from jax.experimental import pallas as pl
from jax.experimental.pallas import tpu as pltpu

F32 = jnp.float32
BF16 = jnp.bfloat16

D_MODEL = 1024
PAST_LEN = 8192
N_META = 16
BLOCK = 128
META_PAD = BLOCK - N_META
HEAD_DIM = 64
N_Q_HEADS = 8
N_KV_HEADS = 2
Q_PER_KV = N_Q_HEADS // N_KV_HEADS
ROT_DIM = HEAD_DIM // 4
ROPE_THETA = 500000.0
GLA_HEADS = 4
GLA_DK = 64
GLA_DV = 128
GLA_RANK = 16
GLA_TAU = 16.0
D_FF = 4 * D_MODEL
ATT_W = N_Q_HEADS * HEAD_DIM
KV_W = N_KV_HEADS * HEAD_DIM
GK_W = GLA_HEADS * GLA_DK
GV_W = GLA_HEADS * GLA_DV
LN_EPS = 1e-5
RMS_EPS = 1e-6
LANES = 128
SUBLANES = 8
VMEM_LIMIT = 56 * 1024 * 1024

C_Q = 0
C_K = C_Q + ATT_W
C_V = C_K + KV_W
C_GQ = C_V + KV_W
C_GK = C_GQ + GK_W
C_GV = C_GK + GK_W
C_LR = C_GV + GV_W
C_GR = C_LR + GLA_RANK
C_GA = C_GR + GV_W
C_GB = C_GA + D_MODEL
W_IN_COLS = C_GB + D_MODEL
C_LR_END = C_LR + LANES

GLA_FAST_MAX_DECAY = 40.0


def _sigmoid(x):
    return 1.0 / (1.0 + jnp.exp(-x))


def _layer_norm(y, g, b):
    mu = jnp.mean(y, axis=-1, keepdims=True)
    yc = y - mu
    var = jnp.mean(yc * yc, axis=-1, keepdims=True)
    return yc * lax.rsqrt(var + LN_EPS) * g + b


def _rope(t, cos, sin_lo, sin_hi):
    outs = []
    for j in range(t.shape[1] // LANES):
        tj = t[:, j * LANES:(j + 1) * LANES]
        outs.append(tj * cos + pltpu.roll(tj, LANES - ROT_DIM // 2, 1) * sin_lo
                    + pltpu.roll(tj, ROT_DIM // 2, 1) * sin_hi)
    return outs[0] if len(outs) == 1 else jnp.concatenate(outs, axis=1)


def _in_proj(xb, w_t_ref, lo, hi):
    return lax.dot_general(xb, w_t_ref[lo:hi, :], (((1,), (1,)), ((), ())), preferred_element_type=F32)


def _log_decay(glr, w_a2, b_a):
    z = jnp.dot(glr.astype(BF16), w_a2, preferred_element_type=F32) + b_a
    return (jnp.minimum(z, 0.0) - jnp.log1p(jnp.exp(-jnp.abs(z)))) * (1.0 / GLA_TAU)


def _row_to_col(row):
    n = row.shape[1]
    eye = lax.broadcasted_iota(jnp.int32, (n, n), 0) == lax.broadcasted_iota(jnp.int32, (n, n), 1)
    return jnp.sum(jnp.where(eye, jnp.broadcast_to(row, (n, n)), 0.0), axis=1, keepdims=True)


N_SPLIT = 3


def _split3_bf16(a):
    hi = a.astype(BF16)
    r = a - hi.astype(F32)
    mid = r.astype(BF16)
    lo = (r - mid.astype(F32)).astype(BF16)
    return hi, mid, lo


def _swish(x):
    return x * _sigmoid(x)


def _gla_gate_out(o, swish_gr, gn):
    outs = []
    for h in range(GLA_HEADS):
        oh = o[:, h * GLA_DV:(h + 1) * GLA_DV]
        ms = jnp.mean(oh * oh, axis=-1, keepdims=True)
        outs.append(oh * lax.rsqrt(ms + RMS_EPS) * gn)
    return jnp.concatenate(outs, axis=1) * swish_gr


def _finish(x, pa, gla, sig_a, sig_b, w_pb, w_out, alpha):
    pb = jnp.dot(gla.astype(BF16), w_pb, preferred_element_type=F32)
    m = sig_a * pa + sig_b * pb
    return alpha * x + jnp.dot(m.astype(BF16), w_out, preferred_element_type=F32)


_MIXER_REF_NAMES = ("cos", "slo", "shi", "w_in", "w_a2", "b_a", "sink", "gn", "w_pa", "w_pb", "w_out", "g1", "b1",
                    "h", "kwin", "vwin", "sfin",
                    "kprev_scr", "vprev_scr", "s_scr", "sprev_scr", "gq_scr", "gk_scr", "gv_scr", "la_scr",
                    "o_scr", "att_t_scr", "gate_scr", "y_scr")


def _mixer_prompt_kernel(*refs, ts, n_steps, n_total, alpha, from_tokens):
    t = pl.program_id(0)
    named = dict(zip(_MIXER_REF_NAMES, refs[len(refs) - len(_MIXER_REF_NAMES):]))
    h_ref, y_scr, g1_ref, b1_ref = named["h"], named["y_scr"], named["g1"], named["b1"]

    @pl.when(t == 0)
    def _():
        y_scr[...] = jnp.zeros(y_scr.shape, F32)

    @pl.when(t < n_total)
    def _():
        _mixer_step(lax.rem(t, n_steps), *refs, ts=ts, n_steps=n_steps, alpha=alpha, from_tokens=from_tokens)

    @pl.when(t == n_total)
    def _():
        h_ref[0] = _layer_norm(y_scr[...], g1_ref[...], b1_ref[...])


def _mixer_step(s, *refs, ts, n_steps, alpha, from_tokens):
    nblk = ts // BLOCK
    n_x = 1 + nblk if from_tokens else 1
    x_refs = refs[:n_x]
    (cos_ref, slo_ref, shi_ref, w_in_ref, w_a2_ref, b_a_ref, sink_ref, gn_ref,
     w_pa_ref, w_pb_ref, w_out_ref, g1_ref, b1_ref,
     h_ref, kwin_ref, vwin_ref, sfin_ref,
     kprev_scr, vprev_scr, s_scr, sprev_scr, gq_scr, gk_scr, gv_scr, la_scr, o_scr, att_t_scr, gate_scr,
     y_scr) = refs[n_x:]
    assert len(refs) - n_x == len(_MIXER_REF_NAMES)

    def load_x():
        if not from_tokens:
            return x_refs[0][0]
        blocks = [r[0] for r in x_refs[1:]]
        blocks[0] = jnp.where(s == 0, x_refs[0][...], blocks[0])
        return jnp.concatenate(blocks, axis=0)

    @pl.when(s == 0)
    def _():
        kprev_scr[...] = jnp.zeros((N_KV_HEADS, BLOCK, KV_W), BF16)
        vprev_scr[...] = jnp.zeros((KV_W, BLOCK), BF16)
        s_scr[...] = jnp.zeros(s_scr.shape, F32)

    x = load_x()
    xb = x.astype(BF16)
    proj = functools.partial(_in_proj, xb, w_in_ref)

    live = (s * ts + lax.broadcasted_iota(jnp.int32, (ts, 1), 0)) >= META_PAD
    la = jnp.where(live, _log_decay(proj(C_LR, C_LR_END), w_a2_ref[...], b_a_ref[...]), 0.0)
    cos, slo, shi = cos_ref[...], slo_ref[...], shi_ref[...]
    q = _rope(proj(C_Q, C_K), cos, slo, shi)
    kv_proj = proj(C_K, C_GQ)
    k = _rope(kv_proj[:, 0:KV_W], cos, slo, shi)
    v = kv_proj[:, KV_W:]
    gq = proj(C_GQ, C_GK) * (GLA_DK ** -0.5)
    gk = jnp.where(live, proj(C_GK, C_GV), 0.0)
    gv = jnp.where(live, proj(C_GV, C_LR), 0.0)
    v_t = v.T

    @pl.when(s == n_steps - 1)
    def _():
        kwin_ref[0] = k[ts - BLOCK:, :].T
        vwin_ref[0] = v_t[:, ts - BLOCK:]

    gate_w = W_IN_COLS - C_GR
    n_gate_pieces = N_KV_HEADS * nblk
    mxu_cols = 2 * LANES
    tiles = gate_w // mxu_cols
    gate_edges = [min(gate_w, mxu_cols * ((tiles * i + n_gate_pieces - 1) // n_gate_pieces))
                  for i in range(n_gate_pieces)] + [gate_w]

    def gate_piece_matmul(i):
        return proj(C_GR + gate_edges[i], C_GR + gate_edges[i + 1])

    def gate_piece_store(i, val):
        lo, hi = gate_edges[i], gate_edges[i + 1]
        mid = min(max(GV_W, lo), hi)
        if mid > lo:
            gate_scr[:, lo:mid] = _swish(val[:, :mid - lo])
        if hi > mid:
            gate_scr[:, mid:hi] = _sigmoid(val[:, mid - lo:])

    q_bf = (q * (HEAD_DIM ** -0.5)).astype(BF16)
    low_half = lax.broadcasted_iota(jnp.int32, (1, KV_W), 1) < HEAD_DIM
    k_swapped = pltpu.roll(k, HEAD_DIM, 1)
    k_dup = [jnp.where(low_half, k, k_swapped).astype(BF16), jnp.where(low_half, k_swapped, k).astype(BF16)]
    k_keys = [jnp.concatenate([kprev_scr[i], k_dup[i]], axis=0) for i in range(N_KV_HEADS)]
    vt_bf = v_t.astype(BF16)
    vt_keys = jnp.concatenate([vprev_scr[...], vt_bf], axis=1)
    kj = lax.broadcasted_iota(jnp.int32, (2 * BLOCK, Q_PER_KV * BLOCK), 0)
    qi = lax.broadcasted_iota(jnp.int32, (2 * BLOCK, Q_PER_KV * BLOCK), 1) % BLOCK
    band = (kj - qi >= 1) & (kj - qi <= BLOCK)
    q_low_half = lax.broadcasted_iota(jnp.int32, (BLOCK, LANES), 1) < HEAD_DIM
    pieces_done = 0
    for blk in range(nblk):
        first_key_slot = (s * nblk + blk - 1) * BLOCK
        valid = band & (kj + first_key_slot >= META_PAD)
        r0 = blk * BLOCK
        for kv in range(N_KV_HEADS):
            heads = [kv * Q_PER_KV + g for g in range(Q_PER_KV)]
            q_rows = []
            for hq in heads:
                grp = q_bf[r0:r0 + BLOCK, (hq // 2) * LANES:(hq // 2 + 1) * LANES]
                own = q_low_half if hq % 2 == 0 else jnp.logical_not(q_low_half)
                q_rows.append(jnp.where(own, grp, jnp.zeros_like(grp)))
            st = lax.dot_general(k_keys[kv][r0:r0 + 2 * BLOCK, :], jnp.concatenate(q_rows, axis=0),
                                 (((1,), (1,)), ((), ())), preferred_element_type=F32)
            gate_val = gate_piece_matmul(pieces_done) if pieces_done < n_gate_pieces else None
            st = jnp.where(valid, st, -jnp.inf)
            sink_row = jnp.concatenate([jnp.full((1, BLOCK), sink_ref[hq], F32) for hq in heads], axis=1)
            m = jnp.maximum(jnp.max(st, axis=0, keepdims=True), sink_row)
            p = jnp.exp(st - m)
            den = jnp.sum(p, axis=0, keepdims=True) + jnp.exp(sink_row - m)
            ot = jnp.dot(vt_keys[kv * HEAD_DIM:(kv + 1) * HEAD_DIM, r0:r0 + 2 * BLOCK], p.astype(BF16),
                         preferred_element_type=F32) * (1.0 / den)
            for g, hq in enumerate(heads):
                att_t_scr[hq * HEAD_DIM:(hq + 1) * HEAD_DIM, r0:r0 + BLOCK] = ot[:, g * BLOCK:(g + 1) * BLOCK]
            if gate_val is not None:
                gate_piece_store(pieces_done, gate_val)
                pieces_done += 1
    for i in range(pieces_done, n_gate_pieces):
        gate_piece_store(i, gate_piece_matmul(i))
    for i in range(N_KV_HEADS):
        kprev_scr[i] = k_dup[i][ts - BLOCK:, :]
    vprev_scr[...] = vt_bf[:, ts - BLOCK:]

    tri = (lax.broadcasted_iota(jnp.int32, (BLOCK, BLOCK), 1)
           <= lax.broadcasted_iota(jnp.int32, (BLOCK, BLOCK), 0))
    tri_bf = jnp.where(tri, 1.0, 0.0).astype(BF16)
    cums = []
    worst = jnp.zeros((1, GK_W), F32)
    for c in range(nblk):
        parts = jnp.concatenate(_split3_bf16(la[c * BLOCK:(c + 1) * BLOCK]), axis=1)
        b3 = jnp.dot(tri_bf, parts, preferred_element_type=F32)
        b = b3[:, 0:GK_W] + b3[:, GK_W:2 * GK_W] + b3[:, 2 * GK_W:3 * GK_W]
        cums.append(b)
        worst = jnp.maximum(worst, -b[BLOCK - 1:BLOCK, :])
    fast_ok = jnp.max(worst) <= GLA_FAST_MAX_DECAY

    head_of_k = lax.broadcasted_iota(jnp.int32, (BLOCK, GK_W), 1) // GLA_DK
    head_of_v = lax.broadcasted_iota(jnp.int32, (BLOCK, GV_W), 1) // GLA_DV
    state_diag = (lax.broadcasted_iota(jnp.int32, (GK_W, GV_W), 0) // GLA_DK
                  == lax.broadcasted_iota(jnp.int32, (GK_W, GV_W), 1) // GLA_DV)

    def proj_att():
        return jnp.dot(att_t_scr[...].T.astype(BF16), w_pa_ref[...], preferred_element_type=F32)

    def gla_fast():
        t_idx = lax.broadcasted_iota(jnp.int32, (BLOCK, GLA_HEADS * BLOCK), 0)
        j_idx = lax.broadcasted_iota(jnp.int32, (BLOCK, GLA_HEADS * BLOCK), 1) % BLOCK
        causal = j_idx <= t_idx
        per_chunk = []
        for c in range(nblk):
            rows = slice(c * BLOCK, (c + 1) * BLOCK)
            b = cums[c]
            b_last = b[BLOCK - 1:BLOCK, :]
            kc = gk[rows]
            q_dec = (gq[rows] * jnp.exp(b)).astype(BF16)
            k_inv = (kc * jnp.exp(-b)).astype(BF16)
            k_end_t = (kc * jnp.exp(b_last - b)).T.astype(BF16)
            vc = gv[rows].astype(BF16)
            zk = jnp.zeros_like(k_inv)
            k_bd = jnp.concatenate([jnp.where(head_of_k == h, k_inv, zk) for h in range(GLA_HEADS)], axis=0)
            a = lax.dot_general(q_dec, k_bd, (((1,), (1,)), ((), ())), preferred_element_type=F32)
            a = jnp.where(causal, a, 0.0).astype(BF16)
            zv = jnp.zeros_like(vc)
            v_bd = jnp.concatenate([jnp.where(head_of_v == h, vc, zv) for h in range(GLA_HEADS)], axis=0)
            o_intra = jnp.dot(a, v_bd, preferred_element_type=F32)
            ds = jnp.where(state_diag, jnp.dot(k_end_t, vc, preferred_element_type=F32), 0.0)
            per_chunk.append((rows, q_dec, o_intra, ds, _row_to_col(jnp.exp(b_last))))
        pa = proj_att()
        for rows, q_dec, o_intra, ds, decay_col in per_chunk:
            s0 = s_scr[...]
            o_scr[rows, :] = o_intra + jnp.dot(q_dec, s0.astype(BF16), preferred_element_type=F32)
            s_scr[...] = decay_col * s0 + ds
        return pa

    def gla_slow():
        def body(i, carry):
            rows = pl.ds(pl.multiple_of(i * SUBLANES, SUBLANES), SUBLANES)
            la8, k8, q8, v8 = la_scr[rows, :], gk_scr[rows, :], gq_scr[rows, :], gv_scr[rows, :]
            outs = []
            for r in range(SUBLANES):
                a_col = _row_to_col(jnp.exp(la8[r:r + 1]))
                k_col = _row_to_col(k8[r:r + 1])
                q_col = _row_to_col(q8[r:r + 1])
                s1 = a_col * s_scr[...] + jnp.where(state_diag, k_col * v8[r:r + 1], 0.0)
                s_scr[...] = s1
                outs.append(jnp.sum(q_col * s1, axis=0, keepdims=True))
            o_scr[rows, :] = jnp.concatenate(outs, axis=0)
            return carry
        lax.fori_loop(0, ts // SUBLANES, body, 0)

    def finish_step(x_val, pa):
        gla = _gla_gate_out(o_scr[...], gate_scr[:, 0:GV_W], gn_ref[...])
        y_scr[...] = _finish(x_val, pa, gla, gate_scr[:, GV_W:GV_W + D_MODEL], gate_scr[:, GV_W + D_MODEL:],
                             w_pb_ref[...], w_out_ref[...], alpha)

    sprev_scr[...] = s_scr[...]
    pa = gla_fast()
    h_ref[0] = _layer_norm(y_scr[...], g1_ref[...], b1_ref[...])
    finish_step(x, pa)

    @pl.when(jnp.logical_not(fast_ok))
    def _():
        x_again = load_x()
        proj_again = functools.partial(_in_proj, x_again.astype(BF16), w_in_ref)
        gq_scr[...] = proj_again(C_GQ, C_GK) * (GLA_DK ** -0.5)
        gk_scr[...] = jnp.where(live, proj_again(C_GK, C_GV), 0.0)
        gv_scr[...] = jnp.where(live, proj_again(C_GV, C_LR), 0.0)
        la_scr[...] = jnp.where(live, _log_decay(proj_again(C_LR, C_LR_END), w_a2_ref[...], b_a_ref[...]), 0.0)
        s_scr[...] = sprev_scr[...]
        gla_slow()
        finish_step(x_again, proj_att())

    @pl.when(s == n_steps - 1)
    def _():
        for h in range(GLA_HEADS):
            sfin_ref[0, h] = s_scr[h * GLA_DK:(h + 1) * GLA_DK, h * GLA_DV:(h + 1) * GLA_DV]


def _ffn_kernel(*refs, alpha, col_chunk):
    h_refs, (w_up_ref, w_dn_ref, g_ref, b_ref, o_ref) = refs[:-5], refs[-5:]
    if len(h_refs) == 1:
        h = h_refs[0][...]
    else:
        h = jnp.concatenate([r[0] for r in h_refs], axis=0)
    hb = h.astype(BF16)
    acc = jnp.zeros(h.shape, F32)
    for c in range(D_FF // col_chunk):
        u = jnp.dot(hb, w_up_ref[:, c * col_chunk:(c + 1) * col_chunk], preferred_element_type=F32)
        u = jnp.maximum(u, 0.0)
        acc = acc + jnp.dot((u * u).astype(BF16), w_dn_ref[c * col_chunk:(c + 1) * col_chunk, :],
                            preferred_element_type=F32)
    o_ref[...] = _layer_norm(alpha * h + acc, g_ref[...], b_ref[...]).reshape(o_ref.shape)


def _sample_proj_kernel(x_ref, cos_ref, slo_ref, shi_ref, w_in_ref, w_a2_ref, b_a_ref,
                        qkv_ref, gv_ref, gate_ref, kvt_ref, gcol_ref):
    proj = functools.partial(_in_proj, x_ref[...].astype(BF16), w_in_ref)

    cos, slo, shi = cos_ref[...], slo_ref[...], shi_ref[...]
    k = _rope(proj(C_K, C_V), cos, slo, shi)
    v = proj(C_V, C_GQ)
    qkv_ref[:, C_Q:C_K] = _rope(proj(C_Q, C_K), cos, slo, shi)
    qkv_ref[:, C_K:C_V] = k
    qkv_ref[:, C_V:C_GQ] = v
    gv_ref[...] = proj(C_GV, C_LR)
    gate_ref[...] = proj(C_GR, W_IN_COLS)
    def store_planes(ref, i, t):
        for p, part in enumerate(_split3_bf16(t)):
            ref[N_SPLIT * i + p] = part.astype(F32).T

    store_planes(kvt_ref, 0, k)
    store_planes(kvt_ref, 1, v)
    store_planes(gcol_ref, 0, jnp.exp(_log_decay(proj(C_LR, C_LR_END), w_a2_ref[...], b_a_ref[...])))
    store_planes(gcol_ref, 1, proj(C_GQ, C_GK) * (GLA_DK ** -0.5))
    store_planes(gcol_ref, 2, proj(C_GK, C_GV))


def _sample_mix_kernel(qkv_ref, gv_ref, kvt_ref, gcol_ref, ck_ref, cv_ref, st_ref, sink_ref, *rest, group):
    att_ref, o_ref, nk_ref, nv_ref, nst_ref = rest[-5:]
    head_row = lax.broadcasted_iota(jnp.int32, (N_Q_HEADS, ATT_W), 0)
    head_lane = lax.broadcasted_iota(jnp.int32, (N_Q_HEADS, ATT_W), 1) // HEAD_DIM
    own = head_row == head_lane
    r8 = lax.broadcasted_iota(jnp.int32, (N_Q_HEADS, KV_W), 0)
    swap = (r8 % 2) != (r8 // Q_PER_KV)
    key_i = lax.broadcasted_iota(jnp.int32, (N_Q_HEADS, BLOCK), 1)
    last_row = lax.broadcasted_iota(jnp.int32, (KV_W, BLOCK), 1) == BLOCK - 1
    sink = sink_ref[...][:, 0:1]

    qkv8 = qkv_ref[...]
    gv8 = gv_ref[...]
    q8s, scores = [], []
    for j in range(group):
        q_row = qkv8[j:j + 1, C_Q:C_K]
        qm = jnp.where(own, jnp.broadcast_to(q_row, (N_Q_HEADS, ATT_W)), 0.0)
        fold = qm[:, 0:128] + qm[:, 128:256] + qm[:, 256:384] + qm[:, 384:512]
        q8 = (jnp.where(swap, pltpu.roll(fold, HEAD_DIM, 1), fold) * (HEAD_DIM ** -0.5)).astype(BF16)
        q8s.append(q8)
        scores.append(jnp.dot(q8, ck_ref[j].reshape(KV_W, BLOCK).astype(BF16), preferred_element_type=F32))

    sel = (lax.broadcasted_iota(jnp.int32, (N_SPLIT * group, group * LANES), 0) % group
           == lax.broadcasted_iota(jnp.int32, (N_SPLIT * group, group * LANES), 1) // LANES)
    sel = jnp.where(sel, 1.0, 0.0).astype(BF16)
    spread = lambda ref, i: jnp.dot(ref[i].astype(BF16), sel, preferred_element_type=F32)
    k_cols, v_cols = spread(kvt_ref, 0), spread(kvt_ref, 1)
    a_cols, q_cols, k_gla_cols = (spread(gcol_ref, i) for i in range(3))

    probs = []
    for j in range(group):
        k_new = qkv8[j:j + 1, C_K:C_V]
        sc = jnp.where(key_i >= 1, scores[j], -jnp.inf)
        s_new = jnp.sum(q8s[j].astype(F32) * k_new.astype(BF16).astype(F32), axis=-1, keepdims=True)
        m = jnp.maximum(jnp.maximum(jnp.max(sc, axis=-1, keepdims=True), s_new), sink)
        p = jnp.exp(sc - m)
        p_new = jnp.exp(s_new - m)
        den = jnp.sum(p, axis=-1, keepdims=True) + p_new + jnp.exp(sink - m)
        probs.append((p.astype(BF16), p_new, den))

    att_rows, o_rows = [], []
    for j in range(group):
        p_bf, p_new, den = probs[j]
        v_new = qkv8[j:j + 1, C_V:C_GQ]
        o8 = (lax.dot_general(p_bf, cv_ref[j].reshape(KV_W, BLOCK).astype(BF16), (((1,), (1,)), ((), ())),
                              preferred_element_type=F32)
              + p_new.astype(BF16).astype(F32) * v_new.astype(BF16).astype(F32)) / den
        o8 = jnp.where(swap, pltpu.roll(o8, HEAD_DIM, 1), o8)
        o_wide = jnp.concatenate([o8, o8, o8, o8], axis=1)
        att_rows.append(jnp.sum(jnp.where(own, o_wide, 0.0), axis=0, keepdims=True))

    for j in range(group):
        lanes_j = slice(j * LANES, (j + 1) * LANES)
        k_old = ck_ref[j].reshape(KV_W, BLOCK)
        v_old = cv_ref[j].reshape(KV_W, BLOCK)
        nk_ref[j] = jnp.where(last_row, k_cols[:, lanes_j], pltpu.roll(k_old, BLOCK - 1, 1)).reshape(
            N_KV_HEADS, HEAD_DIM, BLOCK)
        nv_ref[j] = jnp.where(last_row, v_cols[:, lanes_j], pltpu.roll(v_old, BLOCK - 1, 1)).reshape(
            N_KV_HEADS, HEAD_DIM, BLOCK)
        a_col, q_col, k_col = a_cols[:, lanes_j], q_cols[:, lanes_j], k_gla_cols[:, lanes_j]
        v_row = gv8[j:j + 1, :]
        v_exp = jnp.concatenate([jnp.broadcast_to(v_row[:, h * GLA_DV:(h + 1) * GLA_DV], (GLA_DK, GLA_DV))
                                 for h in range(GLA_HEADS)], axis=0)
        s1 = a_col * st_ref[j].reshape(GK_W, GLA_DV) + k_col * v_exp
        nst_ref[j] = s1.reshape(GLA_HEADS, GLA_DK, GLA_DV)
        qs = q_col * s1
        o_rows.append(jnp.concatenate(
            [jnp.sum(qs[h * GLA_DK:(h + 1) * GLA_DK], axis=0, keepdims=True) for h in range(GLA_HEADS)], axis=1))
    att_ref[...] = jnp.concatenate(att_rows, axis=0)
    o_ref[...] = jnp.concatenate(o_rows, axis=0)


def _sample_finish_kernel(x_ref, att_ref, o_ref, gate_ref, gn_ref, w_pa_ref, w_pb_ref, w_out_ref, g1_ref, b1_ref,
                          h_ref, *, alpha):
    gla = _gla_gate_out(o_ref[...], _swish(gate_ref[:, 0:GV_W]), gn_ref[...])
    pa = jnp.dot(att_ref[...].astype(BF16), w_pa_ref[...], preferred_element_type=F32)
    y = _finish(x_ref[...], pa, gla, _sigmoid(gate_ref[:, GV_W:GV_W + D_MODEL]),
                _sigmoid(gate_ref[:, GV_W + D_MODEL:]), w_pb_ref[...], w_out_ref[...], alpha)
    h_ref[...] = _layer_norm(y, g1_ref[...], b1_ref[...])


def _rope_tables(pos):
    half = ROT_DIM // 2
    inv = ROPE_THETA ** (-jnp.arange(half, dtype=F32) * 2.0 / ROT_DIM)
    d = jnp.arange(LANES) % HEAD_DIM
    ang = pos.astype(F32)[:, None] * inv[d % half][None, :]
    cos, sin = jnp.cos(ang), jnp.sin(ang)
    cos_t = jnp.where(d < ROT_DIM, cos, 1.0)
    sin_lo = jnp.where(d < half, -sin, 0.0)
    sin_hi = jnp.where((d >= half) & (d < ROT_DIM), sin, 0.0)
    return cos_t, sin_lo, sin_hi


def _const_spec(shape, layer=None):
    if layer is None:
        return pl.BlockSpec(shape, lambda *_: (0,) * len(shape), pipeline_mode=pl.Buffered(1))
    return pl.BlockSpec((None,) + shape, lambda *_: (layer,) + (0,) * len(shape), pipeline_mode=pl.Buffered(1))


def _step_rows(total):
    for t in (384, 256, 128):
        if total % t == 0:
            return t
    raise ValueError("padded prompt length must be a multiple of 128")


def kernel(x_prompt, x_sample, cache_k_win, cache_v_win, state_gla, meta_tokens, w_in, w_a2, b_a, attn_sink,
           gla_norm_g, w_proj_a, w_proj_b, w_out, ln1_g, ln1_b, w_up, w_down, ln2_g, ln2_b):
    depth = w_in.shape[0]
    bsz, seq, _ = x_prompt.shape
    nsmp, dec_seq, _ = x_sample.shape
    assert dec_seq == 1 and cache_k_win.shape[2] == BLOCK and seq % BLOCK == 0
    alpha = (2 * depth) ** 0.25
    lp = seq + BLOCK
    ts = _step_rows(lp)
    n_steps = lp // ts
    rows = bsz * lp
    ffn_tile = next(t for t in (512, 384, 256, 128) if rows % t == 0)
    last_tile = next(t for t in (512, 384, 256, 128) if seq % t == 0)
    group = SUBLANES
    assert nsmp % group == 0

    w_in_r = jnp.swapaxes(w_in, 1, 2).astype(BF16)
    w_a2_p = jnp.concatenate([w_a2, jnp.zeros((depth, LANES - GLA_RANK, GK_W), w_a2.dtype)], axis=1).astype(BF16)
    w_pa, w_pb, w_o = w_proj_a.astype(BF16), w_proj_b.astype(BF16), w_out.astype(BF16)
    w_u, w_d = w_up.astype(BF16), w_down.astype(BF16)
    b_a3 = b_a.reshape(depth, 1, GK_W)
    gn3 = gla_norm_g.reshape(depth, 1, GLA_DV)
    g1, b1 = ln1_g.reshape(depth, 1, D_MODEL), ln1_b.reshape(depth, 1, D_MODEL)
    g2, b2 = ln2_g.reshape(depth, 1, D_MODEL), ln2_b.reshape(depth, 1, D_MODEL)
    sink_lanes = jnp.broadcast_to(attn_sink[:, :, None], (depth, N_Q_HEADS, LANES))

    cos_p, slo_p, shi_p = _rope_tables(jnp.arange(lp) - META_PAD)
    cos_s, slo_s, shi_s = (jnp.broadcast_to(t, (nsmp, LANES)) for t in _rope_tables(PAST_LEN + jnp.arange(1)))

    meta_block = jnp.concatenate([jnp.zeros((META_PAD, D_MODEL), x_prompt.dtype),
                                  meta_tokens.astype(x_prompt.dtype)], axis=0)
    nblk = ts // BLOCK
    xp = None
    xs = x_sample.reshape(nsmp, D_MODEL)
    ck = jnp.transpose(cache_k_win, (0, 1, 3, 4, 2))
    cv = jnp.transpose(cache_v_win, (0, 1, 3, 4, 2))

    cparams = functools.partial(pltpu.CompilerParams, vmem_limit_bytes=VMEM_LIMIT)
    pk, pv, pst = [], [], []
    stacked = []
    for l in range(depth):
        n_total = bsz * n_steps
        seq_of = lambda t: jnp.minimum(t, n_total - 1) // n_steps
        step_of = lambda t: jnp.minimum(t, n_total - 1) % n_steps
        step_spec = pl.BlockSpec((1, ts, D_MODEL), lambda t: (seq_of(t), step_of(t), 0))
        h_spec = pl.BlockSpec((1, ts, D_MODEL), lambda t: (seq_of(jnp.maximum(t - 1, 0)), step_of(jnp.maximum(t - 1, 0)), 0))
        tab_spec = pl.BlockSpec((ts, LANES), lambda t: (step_of(t), 0))
        if l == 0:
            x_specs = [_const_spec((BLOCK, D_MODEL))] + [
                pl.BlockSpec((1, BLOCK, D_MODEL),
                             lambda t, j=j: (seq_of(t), jnp.maximum(nblk * step_of(t) + j - 1, 0), 0))
                for j in range(nblk)]
            x_args = [meta_block] + [x_prompt] * nblk
        else:
            x_specs, x_args = [step_spec], [xp]
        hp, kwin, vwin, sfin = pl.pallas_call(
            functools.partial(_mixer_prompt_kernel, ts=ts, n_steps=n_steps, n_total=n_total, alpha=alpha,
                              from_tokens=(l == 0)),
            grid=(n_total + 1,),
            in_specs=x_specs + [tab_spec, tab_spec, tab_spec,
                      _const_spec((W_IN_COLS, D_MODEL), l), _const_spec((LANES, GK_W), l), _const_spec((1, GK_W), l),
                      pl.BlockSpec(memory_space=pltpu.SMEM), _const_spec((1, GLA_DV), l),
                      _const_spec((ATT_W, D_MODEL), l), _const_spec((GV_W, D_MODEL), l),
                      _const_spec((D_MODEL, D_MODEL), l), _const_spec((1, D_MODEL), l), _const_spec((1, D_MODEL), l)],
            out_specs=[h_spec,
                       pl.BlockSpec((1, BLOCK, KV_W), lambda t: (seq_of(t), 0, 0)),
                       pl.BlockSpec((1, BLOCK, KV_W), lambda t: (seq_of(t), 0, 0)),
                       pl.BlockSpec((1, GLA_HEADS, GLA_DK, GLA_DV), lambda t: (seq_of(t), 0, 0, 0))],
            out_shape=[jax.ShapeDtypeStruct((bsz, lp, D_MODEL), F32),
                       jax.ShapeDtypeStruct((bsz, BLOCK, KV_W), F32),
                       jax.ShapeDtypeStruct((bsz, BLOCK, KV_W), F32),
                       jax.ShapeDtypeStruct((bsz, GLA_HEADS, GLA_DK, GLA_DV), F32)],
            scratch_shapes=[pltpu.VMEM((N_KV_HEADS, BLOCK, KV_W), BF16), pltpu.VMEM((KV_W, BLOCK), BF16),
                            pltpu.VMEM((GK_W, GV_W), F32), pltpu.VMEM((GK_W, GV_W), F32),
                            pltpu.VMEM((ts, GK_W), F32), pltpu.VMEM((ts, GK_W), F32), pltpu.VMEM((ts, GV_W), F32),
                            pltpu.VMEM((ts, GK_W), F32), pltpu.VMEM((ts, GV_W), F32), pltpu.VMEM((ATT_W, ts), F32),
                            pltpu.VMEM((ts, W_IN_COLS - C_GR), F32), pltpu.VMEM((ts, D_MODEL), F32)],
            compiler_params=cparams(dimension_semantics=("arbitrary",)),
            name=f"mixer_prompt_{l}",
        )(*x_args, cos_p, slo_p, shi_p, w_in_r, w_a2_p, b_a3, attn_sink[l], gn3, w_pa, w_pb, w_o, g1, b1)
        pk.append(kwin.reshape(bsz, N_KV_HEADS, HEAD_DIM, BLOCK))
        pv.append(vwin.reshape(bsz, N_KV_HEADS, HEAD_DIM, BLOCK))
        pst.append(sfin)

        ffn = functools.partial(_ffn_kernel, alpha=alpha, col_chunk=1024)
        ffn_w = [_const_spec((D_MODEL, D_FF), l), _const_spec((D_FF, D_MODEL), l),
                 _const_spec((1, D_MODEL), l), _const_spec((1, D_MODEL), l)]
        if l < depth - 1:
            xp = pl.pallas_call(
                ffn, grid=(rows // ffn_tile,),
                in_specs=[pl.BlockSpec((ffn_tile, D_MODEL), lambda i: (i, 0))] + ffn_w,
                out_specs=pl.BlockSpec((ffn_tile, D_MODEL), lambda i: (i, 0)),
                out_shape=jax.ShapeDtypeStruct((rows, D_MODEL), F32),
                compiler_params=cparams(dimension_semantics=("arbitrary",)),
                name=f"ffn_prompt_{l}",
            )(hp.reshape(rows, D_MODEL), w_u, w_d, g2, b2).reshape(bsz, lp, D_MODEL)
        else:
            pieces = last_tile // BLOCK
            y_prompt = pl.pallas_call(
                ffn, grid=(bsz, seq // last_tile),
                in_specs=[pl.BlockSpec((1, BLOCK, D_MODEL), lambda b, i, j=j: (b, 1 + pieces * i + j, 0))
                          for j in range(pieces)] + ffn_w,
                out_specs=pl.BlockSpec((1, last_tile, D_MODEL), lambda b, i: (b, i, 0)),
                out_shape=jax.ShapeDtypeStruct((bsz, seq, D_MODEL), F32),
                compiler_params=cparams(dimension_semantics=("arbitrary", "arbitrary")),
                name=f"ffn_prompt_{l}",
            )(*([hp] * pieces), w_u, w_d, g2, b2)

        proj_out = [(nsmp, C_GQ), (nsmp, GV_W), (nsmp, W_IN_COLS - C_GR),
                    (2 * N_SPLIT, KV_W, nsmp), (3 * N_SPLIT, GK_W, nsmp)]
        qkv, gv_s, gate, kvt, gcol = pl.pallas_call(
            _sample_proj_kernel, grid=(1,),
            in_specs=[_const_spec((nsmp, D_MODEL)), _const_spec((nsmp, LANES)), _const_spec((nsmp, LANES)),
                      _const_spec((nsmp, LANES)), _const_spec((W_IN_COLS, D_MODEL), l),
                      _const_spec((LANES, GK_W), l), _const_spec((1, GK_W), l)],
            out_specs=[_const_spec(s) for s in proj_out],
            out_shape=[jax.ShapeDtypeStruct(s, F32) for s in proj_out],
            compiler_params=cparams(dimension_semantics=("arbitrary",)),
            name=f"sample_proj_{l}",
        )(xs, cos_s, slo_s, shi_s, w_in_r, w_a2_p, b_a3)
        def by_group(t):
            t = t.reshape(t.shape[0] // N_SPLIT, N_SPLIT, t.shape[1], nsmp // group, group)
            return jnp.transpose(t, (3, 0, 2, 1, 4)).reshape(nsmp // group, t.shape[0], t.shape[2], N_SPLIT * group)
        col_spec = lambda n, width: pl.BlockSpec((None, n, width, N_SPLIT * group), lambda i: (i, 0, 0, 0))

        grp = lambda width: pl.BlockSpec((group, width), lambda i: (i, 0))
        cache_spec = pl.BlockSpec((None, group, N_KV_HEADS, HEAD_DIM, BLOCK), lambda i: (l, i, 0, 0, 0))
        state_spec = pl.BlockSpec((None, group, GLA_HEADS, GLA_DK, GLA_DV), lambda i: (l, i, 0, 0, 0))
        n_mix_in = 8
        att_s, o_s, *stacked = pl.pallas_call(
            functools.partial(_sample_mix_kernel, group=group), grid=(nsmp // group,),
            in_specs=[grp(C_GQ), grp(GV_W), col_spec(2, KV_W), col_spec(3, GK_W), cache_spec, cache_spec, state_spec,
                      _const_spec((N_Q_HEADS, LANES), l)] + [pl.BlockSpec(memory_space=pl.ANY)] * len(stacked),
            out_specs=[grp(ATT_W), grp(GV_W), cache_spec, cache_spec, state_spec],
            out_shape=[jax.ShapeDtypeStruct((nsmp, ATT_W), F32), jax.ShapeDtypeStruct((nsmp, GV_W), F32),
                       jax.ShapeDtypeStruct(ck.shape, F32), jax.ShapeDtypeStruct(cv.shape, F32),
                       jax.ShapeDtypeStruct(state_gla.shape, F32)],
            input_output_aliases={n_mix_in + i: 2 + i for i in range(len(stacked))},
            compiler_params=cparams(dimension_semantics=("arbitrary",)),
            name=f"sample_mix_{l}",
        )(qkv, gv_s, by_group(kvt), by_group(gcol), ck, cv, state_gla, sink_lanes, *stacked)

        hs = pl.pallas_call(
            functools.partial(_sample_finish_kernel, alpha=alpha), grid=(1,),
            in_specs=[_const_spec((nsmp, D_MODEL)), _const_spec((nsmp, ATT_W)), _const_spec((nsmp, GV_W)),
                      _const_spec((nsmp, W_IN_COLS - C_GR)), _const_spec((1, GLA_DV), l),
                      _const_spec((ATT_W, D_MODEL), l), _const_spec((GV_W, D_MODEL), l),
                      _const_spec((D_MODEL, D_MODEL), l), _const_spec((1, D_MODEL), l), _const_spec((1, D_MODEL), l)],
            out_specs=_const_spec((nsmp, D_MODEL)),
            out_shape=jax.ShapeDtypeStruct((nsmp, D_MODEL), F32),
            compiler_params=cparams(dimension_semantics=("arbitrary",)),
            name=f"sample_finish_{l}",
        )(xs, att_s, o_s, gate, gn3, w_pa, w_pb, w_o, g1, b1)
        xs = pl.pallas_call(
            ffn, grid=(1,),
            in_specs=[_const_spec((nsmp, D_MODEL))] + ffn_w,
            out_specs=_const_spec((nsmp, D_MODEL)),
            out_shape=jax.ShapeDtypeStruct((nsmp, D_MODEL), F32),
            compiler_params=cparams(dimension_semantics=("arbitrary",)),
            name=f"ffn_sample_{l}",
        )(hs, w_u, w_d, g2, b2)

    y_sample = xs.reshape(nsmp, 1, D_MODEL)
    to_rows = lambda t: jnp.transpose(t, (0, 1, 4, 2, 3))
    return (y_prompt, y_sample, to_rows(jnp.stack(pk)), to_rows(jnp.stack(pv)), jnp.stack(pst),
            to_rows(stacked[0]), to_rows(stacked[1]), stacked[2])
```

```python
import functools

import jax
import jax.numpy as jnp
from jax import lax
from jax.experimental import pallas as pl
from jax.experimental.pallas import tpu as pltpu

F32 = jnp.float32
BF16 = jnp.bfloat16

D_MODEL = 1024
PAST_LEN = 8192
N_META = 16
BLOCK = 128
META_PAD = BLOCK - N_META
HEAD_DIM = 64
N_Q_HEADS = 8
N_KV_HEADS = 2
Q_PER_KV = N_Q_HEADS // N_KV_HEADS
ROT_DIM = HEAD_DIM // 4
ROPE_THETA = 500000.0
GLA_HEADS = 4
GLA_DK = 64
GLA_DV = 128
GLA_RANK = 16
GLA_TAU = 16.0
D_FF = 4 * D_MODEL
ATT_W = N_Q_HEADS * HEAD_DIM
KV_W = N_KV_HEADS * HEAD_DIM
GK_W = GLA_HEADS * GLA_DK
GV_W = GLA_HEADS * GLA_DV
LN_EPS = 1e-5
RMS_EPS = 1e-6
LANES = 128
SUBLANES = 8
VMEM_LIMIT = 56 * 1024 * 1024

C_Q = 0
C_K = C_Q + ATT_W
C_V = C_K + KV_W
C_GQ = C_V + KV_W
C_GK = C_GQ + GK_W
C_GV = C_GK + GK_W
C_LR = C_GV + GV_W
C_GR = C_LR + GLA_RANK
C_GA = C_GR + GV_W
C_GB = C_GA + D_MODEL
W_IN_COLS = C_GB + D_MODEL
C_LR_END = C_LR + LANES

GLA_FAST_MAX_DECAY = 40.0


def _sigmoid(x):
    return 1.0 / (1.0 + jnp.exp(-x))


def _layer_norm(y, g, b):
    mu = jnp.mean(y, axis=-1, keepdims=True)
    yc = y - mu
    var = jnp.mean(yc * yc, axis=-1, keepdims=True)
    return yc * lax.rsqrt(var + LN_EPS) * g + b


def _rope(t, cos, sin_lo, sin_hi):
    outs = []
    for j in range(t.shape[1] // LANES):
        tj = t[:, j * LANES:(j + 1) * LANES]
        outs.append(tj * cos + pltpu.roll(tj, LANES - ROT_DIM // 2, 1) * sin_lo
                    + pltpu.roll(tj, ROT_DIM // 2, 1) * sin_hi)
    return outs[0] if len(outs) == 1 else jnp.concatenate(outs, axis=1)


def _in_proj(xb, w_t_ref, lo, hi):
    return lax.dot_general(xb, w_t_ref[lo:hi, :], (((1,), (1,)), ((), ())), preferred_element_type=F32)


def _log_decay(glr, w_a2, b_a):
    z = jnp.dot(glr.astype(BF16), w_a2, preferred_element_type=F32) + b_a
    return (jnp.minimum(z, 0.0) - jnp.log1p(jnp.exp(-jnp.abs(z)))) * (1.0 / GLA_TAU)


def _row_to_col(row):
    n = row.shape[1]
    eye = lax.broadcasted_iota(jnp.int32, (n, n), 0) == lax.broadcasted_iota(jnp.int32, (n, n), 1)
    return jnp.sum(jnp.where(eye, jnp.broadcast_to(row, (n, n)), 0.0), axis=1, keepdims=True)


N_SPLIT = 3


def _split3_bf16(a):
    hi = a.astype(BF16)
    r = a - hi.astype(F32)
    mid = r.astype(BF16)
    lo = (r - mid.astype(F32)).astype(BF16)
    return hi, mid, lo


def _swish(x):
    return x * _sigmoid(x)


def _gla_gate_out(o, swish_gr, gn):
    outs = []
    for h in range(GLA_HEADS):
        oh = o[:, h * GLA_DV:(h + 1) * GLA_DV]
        ms = jnp.mean(oh * oh, axis=-1, keepdims=True)
        outs.append(oh * lax.rsqrt(ms + RMS_EPS) * gn)
    return jnp.concatenate(outs, axis=1) * swish_gr


def _finish(x, pa, gla, sig_a, sig_b, w_pb, w_out, alpha):
    pb = jnp.dot(gla.astype(BF16), w_pb, preferred_element_type=F32)
    m = sig_a * pa + sig_b * pb
    return alpha * x + jnp.dot(m.astype(BF16), w_out, preferred_element_type=F32)


_MIXER_REF_NAMES = ("cos", "slo", "shi", "w_in", "w_a2", "b_a", "sink", "gn", "w_pa", "w_pb", "w_out", "g1", "b1",
                    "h", "kwin", "vwin", "sfin",
                    "kprev_scr", "vprev_scr", "s_scr", "sprev_scr", "gq_scr", "gk_scr", "gv_scr", "la_scr",
                    "o_scr", "att_t_scr", "gate_scr", "y_scr")


def _mixer_prompt_kernel(*refs, ts, n_steps, n_total, alpha, from_tokens):
    t = pl.program_id(0)
    named = dict(zip(_MIXER_REF_NAMES, refs[len(refs) - len(_MIXER_REF_NAMES):]))
    h_ref, y_scr, g1_ref, b1_ref = named["h"], named["y_scr"], named["g1"], named["b1"]

    @pl.when(t == 0)
    def _():
        y_scr[...] = jnp.zeros(y_scr.shape, F32)

    @pl.when(t < n_total)
    def _():
        _mixer_step(lax.rem(t, n_steps), *refs, ts=ts, n_steps=n_steps, alpha=alpha, from_tokens=from_tokens)

    @pl.when(t == n_total)
    def _():
        h_ref[0] = _layer_norm(y_scr[...], g1_ref[...], b1_ref[...])


def _mixer_step(s, *refs, ts, n_steps, alpha, from_tokens):
    nblk = ts // BLOCK
    n_x = 1 + nblk if from_tokens else 1
    x_refs = refs[:n_x]
    (cos_ref, slo_ref, shi_ref, w_in_ref, w_a2_ref, b_a_ref, sink_ref, gn_ref,
     w_pa_ref, w_pb_ref, w_out_ref, g1_ref, b1_ref,
     h_ref, kwin_ref, vwin_ref, sfin_ref,
     kprev_scr, vprev_scr, s_scr, sprev_scr, gq_scr, gk_scr, gv_scr, la_scr, o_scr, att_t_scr, gate_scr,
     y_scr) = refs[n_x:]
    assert len(refs) - n_x == len(_MIXER_REF_NAMES)

    def load_x():
        if not from_tokens:
            return x_refs[0][0]
        blocks = [r[0] for r in x_refs[1:]]
        blocks[0] = jnp.where(s == 0, x_refs[0][...], blocks[0])
        return jnp.concatenate(blocks, axis=0)

    @pl.when(s == 0)
    def _():
        kprev_scr[...] = jnp.zeros((N_KV_HEADS, BLOCK, KV_W), BF16)
        vprev_scr[...] = jnp.zeros((KV_W, BLOCK), BF16)
        s_scr[...] = jnp.zeros(s_scr.shape, F32)

    x = load_x()
    xb = x.astype(BF16)
    proj = functools.partial(_in_proj, xb, w_in_ref)

    mix_in = proj(C_Q, C_LR_END)
    live = (s * ts + lax.broadcasted_iota(jnp.int32, (ts, 1), 0)) >= META_PAD
    cos, slo, shi = cos_ref[...], slo_ref[...], shi_ref[...]
    q = _rope(mix_in[:, C_Q:C_K], cos, slo, shi)
    k = _rope(mix_in[:, C_K:C_V], cos, slo, shi)
    v = mix_in[:, C_V:C_GQ]
    gq = mix_in[:, C_GQ:C_GK] * (GLA_DK ** -0.5)
    gk = jnp.where(live, mix_in[:, C_GK:C_GV], 0.0)
    gv = jnp.where(live, mix_in[:, C_GV:C_LR], 0.0)
    la = None
    v_t = v.T

    @pl.when(s == n_steps - 1)
    def _():
        kwin_ref[0] = k[ts - BLOCK:, :].T
        vwin_ref[0] = v_t[:, ts - BLOCK:]

    gate_w = W_IN_COLS - C_GR
    piece_cols = 4 * LANES
    n_gate_pieces = min(N_KV_HEADS * nblk, -(-gate_w // piece_cols))
    gate_edges = [min(gate_w, piece_cols * (-(-gate_w // piece_cols) * i // n_gate_pieces))
                  for i in range(n_gate_pieces)] + [gate_w]

    def gate_piece_matmul(i):
        return proj(C_GR + gate_edges[i], C_GR + gate_edges[i + 1])

    def gate_piece_store(i, val):
        lo, hi = gate_edges[i], gate_edges[i + 1]
        mid = min(max(GV_W, lo), hi)
        if mid > lo:
            gate_scr[:, lo:mid] = _swish(val[:, :mid - lo])
        if hi > mid:
            gate_scr[:, mid:hi] = _sigmoid(val[:, mid - lo:])

    q_bf = (q * (HEAD_DIM ** -0.5)).astype(BF16)
    low_half = lax.broadcasted_iota(jnp.int32, (1, KV_W), 1) < HEAD_DIM
    k_swapped = pltpu.roll(k, HEAD_DIM, 1)
    k_dup = [jnp.where(low_half, k, k_swapped).astype(BF16), jnp.where(low_half, k_swapped, k).astype(BF16)]
    k_keys = [jnp.concatenate([kprev_scr[i], k_dup[i]], axis=0) for i in range(N_KV_HEADS)]
    vt_bf = v_t.astype(BF16)
    vt_keys = jnp.concatenate([vprev_scr[...], vt_bf], axis=1)
    kj = lax.broadcasted_iota(jnp.int32, (2 * BLOCK, Q_PER_KV * BLOCK), 0)
    qi = lax.broadcasted_iota(jnp.int32, (2 * BLOCK, Q_PER_KV * BLOCK), 1) % BLOCK
    band = (kj - qi >= 1) & (kj - qi <= BLOCK)
    q_low_half = lax.broadcasted_iota(jnp.int32, (BLOCK, LANES), 1) < HEAD_DIM
    pieces_done = 0
    for blk in range(nblk):
        first_key_slot = (s * nblk + blk - 1) * BLOCK
        valid = band & (kj + first_key_slot >= META_PAD)
        r0 = blk * BLOCK
        for kv in range(N_KV_HEADS):
            heads = [kv * Q_PER_KV + g for g in range(Q_PER_KV)]
            q_rows = []
            for hq in heads:
                grp = q_bf[r0:r0 + BLOCK, (hq // 2) * LANES:(hq // 2 + 1) * LANES]
                own = q_low_half if hq % 2 == 0 else jnp.logical_not(q_low_half)
                q_rows.append(jnp.where(own, grp, jnp.zeros_like(grp)))
            st = lax.dot_general(k_keys[kv][r0:r0 + 2 * BLOCK, :], jnp.concatenate(q_rows, axis=0),
                                 (((1,), (1,)), ((), ())), preferred_element_type=F32)
            gate_val = gate_piece_matmul(pieces_done) if pieces_done < n_gate_pieces else None
            if la is None:
                la = jnp.where(live, _log_decay(mix_in[:, C_LR:C_LR_END], w_a2_ref[...], b_a_ref[...]), 0.0)
            st = jnp.where(valid, st, -jnp.inf)
            sink_row = jnp.concatenate([jnp.full((1, BLOCK), sink_ref[hq], F32) for hq in heads], axis=1)
            m = jnp.maximum(jnp.max(st, axis=0, keepdims=True), sink_row)
            p = jnp.exp(st - m)
            den = jnp.sum(p, axis=0, keepdims=True) + jnp.exp(sink_row - m)
            ot = jnp.dot(vt_keys[kv * HEAD_DIM:(kv + 1) * HEAD_DIM, r0:r0 + 2 * BLOCK], p.astype(BF16),
                         preferred_element_type=F32) * (1.0 / den)
            for g, hq in enumerate(heads):
                att_t_scr[hq * HEAD_DIM:(hq + 1) * HEAD_DIM, r0:r0 + BLOCK] = ot[:, g * BLOCK:(g + 1) * BLOCK]
            if gate_val is not None:
                gate_piece_store(pieces_done, gate_val)
                pieces_done += 1
    for i in range(pieces_done, n_gate_pieces):
        gate_piece_store(i, gate_piece_matmul(i))
    for i in range(N_KV_HEADS):
        kprev_scr[i] = k_dup[i][ts - BLOCK:, :]
    vprev_scr[...] = vt_bf[:, ts - BLOCK:]

    tri = (lax.broadcasted_iota(jnp.int32, (BLOCK, BLOCK), 1)
           <= lax.broadcasted_iota(jnp.int32, (BLOCK, BLOCK), 0))
    tri_bf = jnp.where(tri, 1.0, 0.0).astype(BF16)
    cums = []
    worst = jnp.zeros((1, GK_W), F32)
    for c in range(nblk):
        parts = jnp.concatenate(_split3_bf16(la[c * BLOCK:(c + 1) * BLOCK]), axis=1)
        b3 = jnp.dot(tri_bf, parts, preferred_element_type=F32)
        b = b3[:, 0:GK_W] + b3[:, GK_W:2 * GK_W] + b3[:, 2 * GK_W:3 * GK_W]
        cums.append(b)
        worst = jnp.maximum(worst, -b[BLOCK - 1:BLOCK, :])
    fast_ok = jnp.max(worst) <= GLA_FAST_MAX_DECAY

    head_of_k = lax.broadcasted_iota(jnp.int32, (BLOCK, GK_W), 1) // GLA_DK
    head_of_v = lax.broadcasted_iota(jnp.int32, (BLOCK, GV_W), 1) // GLA_DV
    state_diag = (lax.broadcasted_iota(jnp.int32, (GK_W, GV_W), 0) // GLA_DK
                  == lax.broadcasted_iota(jnp.int32, (GK_W, GV_W), 1) // GLA_DV)

    def proj_att():
        return jnp.dot(att_t_scr[...].T.astype(BF16), w_pa_ref[...], preferred_element_type=F32)

    def gla_fast():
        t_idx = lax.broadcasted_iota(jnp.int32, (BLOCK, GLA_HEADS * BLOCK), 0)
        j_idx = lax.broadcasted_iota(jnp.int32, (BLOCK, GLA_HEADS * BLOCK), 1) % BLOCK
        causal = j_idx <= t_idx
        per_chunk = []
        for c in range(nblk):
            rows = slice(c * BLOCK, (c + 1) * BLOCK)
            b = cums[c]
            b_last = b[BLOCK - 1:BLOCK, :]
            kc = gk[rows]
            q_dec = (gq[rows] * jnp.exp(b)).astype(BF16)
            k_inv = (kc * jnp.exp(-b)).astype(BF16)
            k_end_t = (kc * jnp.exp(b_last - b)).T.astype(BF16)
            vc = gv[rows].astype(BF16)
            zk = jnp.zeros_like(k_inv)
            k_bd = jnp.concatenate([jnp.where(head_of_k == h, k_inv, zk) for h in range(GLA_HEADS)], axis=0)
            a = lax.dot_general(q_dec, k_bd, (((1,), (1,)), ((), ())), preferred_element_type=F32)
            a = jnp.where(causal, a, 0.0).astype(BF16)
            zv = jnp.zeros_like(vc)
            v_bd = jnp.concatenate([jnp.where(head_of_v == h, vc, zv) for h in range(GLA_HEADS)], axis=0)
            o_intra = jnp.dot(a, v_bd, preferred_element_type=F32)
            ds = jnp.where(state_diag, jnp.dot(k_end_t, vc, preferred_element_type=F32), 0.0)
            per_chunk.append((rows, q_dec, o_intra, ds, _row_to_col(jnp.exp(b_last))))
        pa = proj_att()
        for rows, q_dec, o_intra, ds, decay_col in per_chunk:
            s0 = s_scr[...]
            o_scr[rows, :] = o_intra + jnp.dot(q_dec, s0.astype(BF16), preferred_element_type=F32)
            s_scr[...] = decay_col * s0 + ds
        return pa

    def gla_slow():
        def body(i, carry):
            rows = pl.ds(pl.multiple_of(i * SUBLANES, SUBLANES), SUBLANES)
            la8, k8, q8, v8 = la_scr[rows, :], gk_scr[rows, :], gq_scr[rows, :], gv_scr[rows, :]
            outs = []
            for r in range(SUBLANES):
                a_col = _row_to_col(jnp.exp(la8[r:r + 1]))
                k_col = _row_to_col(k8[r:r + 1])
                q_col = _row_to_col(q8[r:r + 1])
                s1 = a_col * s_scr[...] + jnp.where(state_diag, k_col * v8[r:r + 1], 0.0)
                s_scr[...] = s1
                outs.append(jnp.sum(q_col * s1, axis=0, keepdims=True))
            o_scr[rows, :] = jnp.concatenate(outs, axis=0)
            return carry
        lax.fori_loop(0, ts // SUBLANES, body, 0)

    def finish_step(x_val, pa):
        gla = _gla_gate_out(o_scr[...], gate_scr[:, 0:GV_W], gn_ref[...])
        y_scr[...] = _finish(x_val, pa, gla, gate_scr[:, GV_W:GV_W + D_MODEL], gate_scr[:, GV_W + D_MODEL:],
                             w_pb_ref[...], w_out_ref[...], alpha)

    sprev_scr[...] = s_scr[...]
    pa = gla_fast()
    h_ref[0] = _layer_norm(y_scr[...], g1_ref[...], b1_ref[...])
    finish_step(x, pa)

    @pl.when(jnp.logical_not(fast_ok))
    def _():
        x_again = load_x()
        proj_again = functools.partial(_in_proj, x_again.astype(BF16), w_in_ref)
        gq_scr[...] = proj_again(C_GQ, C_GK) * (GLA_DK ** -0.5)
        gk_scr[...] = jnp.where(live, proj_again(C_GK, C_GV), 0.0)
        gv_scr[...] = jnp.where(live, proj_again(C_GV, C_LR), 0.0)
        la_scr[...] = jnp.where(live, _log_decay(proj_again(C_LR, C_LR_END), w_a2_ref[...], b_a_ref[...]), 0.0)
        s_scr[...] = sprev_scr[...]
        gla_slow()
        finish_step(x_again, proj_att())

    @pl.when(s == n_steps - 1)
    def _():
        for h in range(GLA_HEADS):
            sfin_ref[0, h] = s_scr[h * GLA_DK:(h + 1) * GLA_DK, h * GLA_DV:(h + 1) * GLA_DV]


def _ffn_kernel(*refs, alpha, col_chunk):
    h_refs, (w_up_ref, w_dn_ref, g_ref, b_ref, o_ref) = refs[:-5], refs[-5:]
    if len(h_refs) == 1:
        h = h_refs[0][...]
    else:
        h = jnp.concatenate([r[0] for r in h_refs], axis=0)
    hb = h.astype(BF16)
    acc = jnp.zeros(h.shape, F32)
    for c in range(D_FF // col_chunk):
        u = jnp.dot(hb, w_up_ref[:, c * col_chunk:(c + 1) * col_chunk], preferred_element_type=F32)
        u = jnp.maximum(u, 0.0)
        acc = acc + jnp.dot((u * u).astype(BF16), w_dn_ref[c * col_chunk:(c + 1) * col_chunk, :],
                            preferred_element_type=F32)
    o_ref[...] = _layer_norm(alpha * h + acc, g_ref[...], b_ref[...]).reshape(o_ref.shape)


def _sample_proj_kernel(x_ref, cos_ref, slo_ref, shi_ref, w_in_ref, w_a2_ref, b_a_ref,
                        qkv_ref, gv_ref, gate_ref, kvt_ref, gcol_ref):
    proj = functools.partial(_in_proj, x_ref[...].astype(BF16), w_in_ref)

    cos, slo, shi = cos_ref[...], slo_ref[...], shi_ref[...]
    k = _rope(proj(C_K, C_V), cos, slo, shi)
    v = proj(C_V, C_GQ)
    qkv_ref[:, C_Q:C_K] = _rope(proj(C_Q, C_K), cos, slo, shi)
    qkv_ref[:, C_K:C_V] = k
    qkv_ref[:, C_V:C_GQ] = v
    gv_ref[...] = proj(C_GV, C_LR)
    gate_ref[...] = proj(C_GR, W_IN_COLS)
    def store_planes(ref, i, t):
        for p, part in enumerate(_split3_bf16(t)):
            ref[N_SPLIT * i + p] = part.astype(F32).T

    store_planes(kvt_ref, 0, k)
    store_planes(kvt_ref, 1, v)
    store_planes(gcol_ref, 0, jnp.exp(_log_decay(proj(C_LR, C_LR_END), w_a2_ref[...], b_a_ref[...])))
    store_planes(gcol_ref, 1, proj(C_GQ, C_GK) * (GLA_DK ** -0.5))
    store_planes(gcol_ref, 2, proj(C_GK, C_GV))


def _sample_mix_kernel(qkv_ref, gv_ref, kvt_ref, gcol_ref, ck_ref, cv_ref, st_ref, sink_ref, *rest, group):
    att_ref, o_ref, nk_ref, nv_ref, nst_ref = rest[-5:]
    head_row = lax.broadcasted_iota(jnp.int32, (N_Q_HEADS, ATT_W), 0)
    head_lane = lax.broadcasted_iota(jnp.int32, (N_Q_HEADS, ATT_W), 1) // HEAD_DIM
    own = head_row == head_lane
    r8 = lax.broadcasted_iota(jnp.int32, (N_Q_HEADS, KV_W), 0)
    swap = (r8 % 2) != (r8 // Q_PER_KV)
    key_i = lax.broadcasted_iota(jnp.int32, (N_Q_HEADS, BLOCK), 1)
    last_row = lax.broadcasted_iota(jnp.int32, (KV_W, BLOCK), 1) == BLOCK - 1
    sink = sink_ref[...][:, 0:1]

    qkv8 = qkv_ref[...]
    gv8 = gv_ref[...]
    q8s, scores = [], []
    for j in range(group):
        q_row = qkv8[j:j + 1, C_Q:C_K]
        qm = jnp.where(own, jnp.broadcast_to(q_row, (N_Q_HEADS, ATT_W)), 0.0)
        fold = qm[:, 0:128] + qm[:, 128:256] + qm[:, 256:384] + qm[:, 384:512]
        q8 = (jnp.where(swap, pltpu.roll(fold, HEAD_DIM, 1), fold) * (HEAD_DIM ** -0.5)).astype(BF16)
        q8s.append(q8)
        scores.append(jnp.dot(q8, ck_ref[j].reshape(KV_W, BLOCK).astype(BF16), preferred_element_type=F32))

    sel = (lax.broadcasted_iota(jnp.int32, (N_SPLIT * group, group * LANES), 0) % group
           == lax.broadcasted_iota(jnp.int32, (N_SPLIT * group, group * LANES), 1) // LANES)
    sel = jnp.where(sel, 1.0, 0.0).astype(BF16)
    spread = lambda ref, i: jnp.dot(ref[i].astype(BF16), sel, preferred_element_type=F32)
    k_cols, v_cols = spread(kvt_ref, 0), spread(kvt_ref, 1)
    a_cols, q_cols, k_gla_cols = (spread(gcol_ref, i) for i in range(3))

    probs = []
    for j in range(group):
        k_new = qkv8[j:j + 1, C_K:C_V]
        sc = jnp.where(key_i >= 1, scores[j], -jnp.inf)
        s_new = jnp.sum(q8s[j].astype(F32) * k_new.astype(BF16).astype(F32), axis=-1, keepdims=True)
        m = jnp.maximum(jnp.maximum(jnp.max(sc, axis=-1, keepdims=True), s_new), sink)
        p = jnp.exp(sc - m)
        p_new = jnp.exp(s_new - m)
        den = jnp.sum(p, axis=-1, keepdims=True) + p_new + jnp.exp(sink - m)
        probs.append((p.astype(BF16), p_new, den))

    att_rows, o_rows = [], []
    for j in range(group):
        p_bf, p_new, den = probs[j]
        v_new = qkv8[j:j + 1, C_V:C_GQ]
        o8 = (lax.dot_general(p_bf, cv_ref[j].reshape(KV_W, BLOCK).astype(BF16), (((1,), (1,)), ((), ())),
                              preferred_element_type=F32)
              + p_new.astype(BF16).astype(F32) * v_new.astype(BF16).astype(F32)) / den
        o8 = jnp.where(swap, pltpu.roll(o8, HEAD_DIM, 1), o8)
        o_wide = jnp.concatenate([o8, o8, o8, o8], axis=1)
        att_rows.append(jnp.sum(jnp.where(own, o_wide, 0.0), axis=0, keepdims=True))

    for j in range(group):
        lanes_j = slice(j * LANES, (j + 1) * LANES)
        k_old = ck_ref[j].reshape(KV_W, BLOCK)
        v_old = cv_ref[j].reshape(KV_W, BLOCK)
        nk_ref[j] = jnp.where(last_row, k_cols[:, lanes_j], pltpu.roll(k_old, BLOCK - 1, 1)).reshape(
            N_KV_HEADS, HEAD_DIM, BLOCK)
        nv_ref[j] = jnp.where(last_row, v_cols[:, lanes_j], pltpu.roll(v_old, BLOCK - 1, 1)).reshape(
            N_KV_HEADS, HEAD_DIM, BLOCK)
        a_col, q_col, k_col = a_cols[:, lanes_j], q_cols[:, lanes_j], k_gla_cols[:, lanes_j]
        v_row = gv8[j:j + 1, :]
        v_exp = jnp.concatenate([jnp.broadcast_to(v_row[:, h * GLA_DV:(h + 1) * GLA_DV], (GLA_DK, GLA_DV))
                                 for h in range(GLA_HEADS)], axis=0)
        s1 = a_col * st_ref[j].reshape(GK_W, GLA_DV) + k_col * v_exp
        nst_ref[j] = s1.reshape(GLA_HEADS, GLA_DK, GLA_DV)
        qs = q_col * s1
        o_rows.append(jnp.concatenate(
            [jnp.sum(qs[h * GLA_DK:(h + 1) * GLA_DK], axis=0, keepdims=True) for h in range(GLA_HEADS)], axis=1))
    att_ref[...] = jnp.concatenate(att_rows, axis=0)
    o_ref[...] = jnp.concatenate(o_rows, axis=0)


def _sample_finish_kernel(x_ref, att_ref, o_ref, gate_ref, gn_ref, w_pa_ref, w_pb_ref, w_out_ref, g1_ref, b1_ref,
                          h_ref, *, alpha):
    gla = _gla_gate_out(o_ref[...], _swish(gate_ref[:, 0:GV_W]), gn_ref[...])
    pa = jnp.dot(att_ref[...].astype(BF16), w_pa_ref[...], preferred_element_type=F32)
    y = _finish(x_ref[...], pa, gla, _sigmoid(gate_ref[:, GV_W:GV_W + D_MODEL]),
                _sigmoid(gate_ref[:, GV_W + D_MODEL:]), w_pb_ref[...], w_out_ref[...], alpha)
    h_ref[...] = _layer_norm(y, g1_ref[...], b1_ref[...])


def _rope_tables(pos):
    half = ROT_DIM // 2
    inv = ROPE_THETA ** (-jnp.arange(half, dtype=F32) * 2.0 / ROT_DIM)
    d = jnp.arange(LANES) % HEAD_DIM
    ang = pos.astype(F32)[:, None] * inv[d % half][None, :]
    cos, sin = jnp.cos(ang), jnp.sin(ang)
    cos_t = jnp.where(d < ROT_DIM, cos, 1.0)
    sin_lo = jnp.where(d < half, -sin, 0.0)
    sin_hi = jnp.where((d >= half) & (d < ROT_DIM), sin, 0.0)
    return cos_t, sin_lo, sin_hi


def _const_spec(shape, layer=None):
    if layer is None:
        return pl.BlockSpec(shape, lambda *_: (0,) * len(shape), pipeline_mode=pl.Buffered(1))
    return pl.BlockSpec((None,) + shape, lambda *_: (layer,) + (0,) * len(shape), pipeline_mode=pl.Buffered(1))


def _step_rows(total):
    for t in (384, 256, 128):
        if total % t == 0:
            return t
    raise ValueError("padded prompt length must be a multiple of 128")


def kernel(x_prompt, x_sample, cache_k_win, cache_v_win, state_gla, meta_tokens, w_in, w_a2, b_a, attn_sink,
           gla_norm_g, w_proj_a, w_proj_b, w_out, ln1_g, ln1_b, w_up, w_down, ln2_g, ln2_b):
    depth = w_in.shape[0]
    bsz, seq, _ = x_prompt.shape
    nsmp, dec_seq, _ = x_sample.shape
    assert dec_seq == 1 and cache_k_win.shape[2] == BLOCK and seq % BLOCK == 0
    alpha = (2 * depth) ** 0.25
    lp = seq + BLOCK
    ts = _step_rows(lp)
    n_steps = lp // ts
    rows = bsz * lp
    ffn_tile = next(t for t in (512, 384, 256, 128) if rows % t == 0)
    last_tile = next(t for t in (512, 384, 256, 128) if seq % t == 0)
    group = next(g for g in (2 * SUBLANES, SUBLANES) if nsmp % g == 0)

    w_in_r = jnp.swapaxes(w_in, 1, 2).astype(BF16)
    w_a2_p = jnp.concatenate([w_a2, jnp.zeros((depth, LANES - GLA_RANK, GK_W), w_a2.dtype)], axis=1).astype(BF16)
    w_pa, w_pb, w_o = w_proj_a.astype(BF16), w_proj_b.astype(BF16), w_out.astype(BF16)
    w_u, w_d = w_up.astype(BF16), w_down.astype(BF16)
    b_a3 = b_a.reshape(depth, 1, GK_W)
    gn3 = gla_norm_g.reshape(depth, 1, GLA_DV)
    g1, b1 = ln1_g.reshape(depth, 1, D_MODEL), ln1_b.reshape(depth, 1, D_MODEL)
    g2, b2 = ln2_g.reshape(depth, 1, D_MODEL), ln2_b.reshape(depth, 1, D_MODEL)
    sink_lanes = jnp.broadcast_to(attn_sink[:, :, None], (depth, N_Q_HEADS, LANES))

    cos_p, slo_p, shi_p = _rope_tables(jnp.arange(lp) - META_PAD)
    cos_s, slo_s, shi_s = (jnp.broadcast_to(t, (nsmp, LANES)) for t in _rope_tables(PAST_LEN + jnp.arange(1)))

    meta_block = jnp.concatenate([jnp.zeros((META_PAD, D_MODEL), x_prompt.dtype),
                                  meta_tokens.astype(x_prompt.dtype)], axis=0)
    nblk = ts // BLOCK
    xp = None
    xs = x_sample.reshape(nsmp, D_MODEL)
    ck = jnp.transpose(cache_k_win, (0, 1, 3, 4, 2))
    cv = jnp.transpose(cache_v_win, (0, 1, 3, 4, 2))

    cparams = functools.partial(pltpu.CompilerParams, vmem_limit_bytes=VMEM_LIMIT)
    pk, pv, pst = [], [], []
    stacked = []
    for l in range(depth):
        n_total = bsz * n_steps
        seq_of = lambda t: jnp.minimum(t, n_total - 1) // n_steps
        step_of = lambda t: jnp.minimum(t, n_total - 1) % n_steps
        step_spec = pl.BlockSpec((1, ts, D_MODEL), lambda t: (seq_of(t), step_of(t), 0))
        h_spec = pl.BlockSpec((1, ts, D_MODEL), lambda t: (seq_of(jnp.maximum(t - 1, 0)), step_of(jnp.maximum(t - 1, 0)), 0))
        tab_spec = pl.BlockSpec((ts, LANES), lambda t: (step_of(t), 0))
        if l == 0:
            x_specs = [_const_spec((BLOCK, D_MODEL))] + [
                pl.BlockSpec((1, BLOCK, D_MODEL),
                             lambda t, j=j: (seq_of(t), jnp.maximum(nblk * step_of(t) + j - 1, 0), 0))
                for j in range(nblk)]
            x_args = [meta_block] + [x_prompt] * nblk
        else:
            x_specs, x_args = [step_spec], [xp]
        hp, kwin, vwin, sfin = pl.pallas_call(
            functools.partial(_mixer_prompt_kernel, ts=ts, n_steps=n_steps, n_total=n_total, alpha=alpha,
                              from_tokens=(l == 0)),
            grid=(n_total + 1,),
            in_specs=x_specs + [tab_spec, tab_spec, tab_spec,
                      _const_spec((W_IN_COLS, D_MODEL), l), _const_spec((LANES, GK_W), l), _const_spec((1, GK_W), l),
                      pl.BlockSpec(memory_space=pltpu.SMEM), _const_spec((1, GLA_DV), l),
                      _const_spec((ATT_W, D_MODEL), l), _const_spec((GV_W, D_MODEL), l),
                      _const_spec((D_MODEL, D_MODEL), l), _const_spec((1, D_MODEL), l), _const_spec((1, D_MODEL), l)],
            out_specs=[h_spec,
                       pl.BlockSpec((1, BLOCK, KV_W), lambda t: (seq_of(t), 0, 0)),
                       pl.BlockSpec((1, BLOCK, KV_W), lambda t: (seq_of(t), 0, 0)),
                       pl.BlockSpec((1, GLA_HEADS, GLA_DK, GLA_DV), lambda t: (seq_of(t), 0, 0, 0))],
            out_shape=[jax.ShapeDtypeStruct((bsz, lp, D_MODEL), F32),
                       jax.ShapeDtypeStruct((bsz, BLOCK, KV_W), F32),
                       jax.ShapeDtypeStruct((bsz, BLOCK, KV_W), F32),
                       jax.ShapeDtypeStruct((bsz, GLA_HEADS, GLA_DK, GLA_DV), F32)],
            scratch_shapes=[pltpu.VMEM((N_KV_HEADS, BLOCK, KV_W), BF16), pltpu.VMEM((KV_W, BLOCK), BF16),
                            pltpu.VMEM((GK_W, GV_W), F32), pltpu.VMEM((GK_W, GV_W), F32),
                            pltpu.VMEM((ts, GK_W), F32), pltpu.VMEM((ts, GK_W), F32), pltpu.VMEM((ts, GV_W), F32),
                            pltpu.VMEM((ts, GK_W), F32), pltpu.VMEM((ts, GV_W), F32), pltpu.VMEM((ATT_W, ts), F32),
                            pltpu.VMEM((ts, W_IN_COLS - C_GR), F32), pltpu.VMEM((ts, D_MODEL), F32)],
            compiler_params=cparams(dimension_semantics=("arbitrary",)),
            name=f"mixer_prompt_{l}",
        )(*x_args, cos_p, slo_p, shi_p, w_in_r, w_a2_p, b_a3, attn_sink[l], gn3, w_pa, w_pb, w_o, g1, b1)
        pk.append(kwin.reshape(bsz, N_KV_HEADS, HEAD_DIM, BLOCK))
        pv.append(vwin.reshape(bsz, N_KV_HEADS, HEAD_DIM, BLOCK))
        pst.append(sfin)

        ffn = functools.partial(_ffn_kernel, alpha=alpha, col_chunk=1024)
        ffn_w = [_const_spec((D_MODEL, D_FF), l), _const_spec((D_FF, D_MODEL), l),
                 _const_spec((1, D_MODEL), l), _const_spec((1, D_MODEL), l)]
        if l < depth - 1:
            xp = pl.pallas_call(
                ffn, grid=(rows // ffn_tile,),
                in_specs=[pl.BlockSpec((ffn_tile, D_MODEL), lambda i: (i, 0))] + ffn_w,
                out_specs=pl.BlockSpec((ffn_tile, D_MODEL), lambda i: (i, 0)),
                out_shape=jax.ShapeDtypeStruct((rows, D_MODEL), F32),
                compiler_params=cparams(dimension_semantics=("arbitrary",)),
                name=f"ffn_prompt_{l}",
            )(hp.reshape(rows, D_MODEL), w_u, w_d, g2, b2).reshape(bsz, lp, D_MODEL)
        else:
            pieces = last_tile // BLOCK
            y_prompt = pl.pallas_call(
                ffn, grid=(bsz, seq // last_tile),
                in_specs=[pl.BlockSpec((1, BLOCK, D_MODEL), lambda b, i, j=j: (b, 1 + pieces * i + j, 0))
                          for j in range(pieces)] + ffn_w,
                out_specs=pl.BlockSpec((1, last_tile, D_MODEL), lambda b, i: (b, i, 0)),
                out_shape=jax.ShapeDtypeStruct((bsz, seq, D_MODEL), F32),
                compiler_params=cparams(dimension_semantics=("arbitrary", "arbitrary")),
                name=f"ffn_prompt_{l}",
            )(*([hp] * pieces), w_u, w_d, g2, b2)

        proj_out = [(nsmp, C_GQ), (nsmp, GV_W), (nsmp, W_IN_COLS - C_GR),
                    (2 * N_SPLIT, KV_W, nsmp), (3 * N_SPLIT, GK_W, nsmp)]
        qkv, gv_s, gate, kvt, gcol = pl.pallas_call(
            _sample_proj_kernel, grid=(1,),
            in_specs=[_const_spec((nsmp, D_MODEL)), _const_spec((nsmp, LANES)), _const_spec((nsmp, LANES)),
                      _const_spec((nsmp, LANES)), _const_spec((W_IN_COLS, D_MODEL), l),
                      _const_spec((LANES, GK_W), l), _const_spec((1, GK_W), l)],
            out_specs=[_const_spec(s) for s in proj_out],
            out_shape=[jax.ShapeDtypeStruct(s, F32) for s in proj_out],
            compiler_params=cparams(dimension_semantics=("arbitrary",)),
            name=f"sample_proj_{l}",
        )(xs, cos_s, slo_s, shi_s, w_in_r, w_a2_p, b_a3)
        def by_group(t):
            t = t.reshape(t.shape[0] // N_SPLIT, N_SPLIT, t.shape[1], nsmp // group, group)
            return jnp.transpose(t, (3, 0, 2, 1, 4)).reshape(nsmp // group, t.shape[0], t.shape[2], N_SPLIT * group)
        col_spec = lambda n, width: pl.BlockSpec((None, n, width, N_SPLIT * group), lambda i: (i, 0, 0, 0))

        grp = lambda width: pl.BlockSpec((group, width), lambda i: (i, 0))
        cache_spec = pl.BlockSpec((None, group, N_KV_HEADS, HEAD_DIM, BLOCK), lambda i: (l, i, 0, 0, 0))
        state_spec = pl.BlockSpec((None, group, GLA_HEADS, GLA_DK, GLA_DV), lambda i: (l, i, 0, 0, 0))
        n_mix_in = 8
        att_s, o_s, *stacked = pl.pallas_call(
            functools.partial(_sample_mix_kernel, group=group), grid=(nsmp // group,),
            in_specs=[grp(C_GQ), grp(GV_W), col_spec(2, KV_W), col_spec(3, GK_W), cache_spec, cache_spec, state_spec,
                      _const_spec((N_Q_HEADS, LANES), l)] + [pl.BlockSpec(memory_space=pl.ANY)] * len(stacked),
            out_specs=[grp(ATT_W), grp(GV_W), cache_spec, cache_spec, state_spec],
            out_shape=[jax.ShapeDtypeStruct((nsmp, ATT_W), F32), jax.ShapeDtypeStruct((nsmp, GV_W), F32),
                       jax.ShapeDtypeStruct(ck.shape, F32), jax.ShapeDtypeStruct(cv.shape, F32),
                       jax.ShapeDtypeStruct(state_gla.shape, F32)],
            input_output_aliases={n_mix_in + i: 2 + i for i in range(len(stacked))},
            compiler_params=cparams(dimension_semantics=("arbitrary",)),
            name=f"sample_mix_{l}",
        )(qkv, gv_s, by_group(kvt), by_group(gcol), ck, cv, state_gla, sink_lanes, *stacked)

        hs = pl.pallas_call(
            functools.partial(_sample_finish_kernel, alpha=alpha), grid=(1,),
            in_specs=[_const_spec((nsmp, D_MODEL)), _const_spec((nsmp, ATT_W)), _const_spec((nsmp, GV_W)),
                      _const_spec((nsmp, W_IN_COLS - C_GR)), _const_spec((1, GLA_DV), l),
                      _const_spec((ATT_W, D_MODEL), l), _const_spec((GV_W, D_MODEL), l),
                      _const_spec((D_MODEL, D_MODEL), l), _const_spec((1, D_MODEL), l), _const_spec((1, D_MODEL), l)],
            out_specs=_const_spec((nsmp, D_MODEL)),
            out_shape=jax.ShapeDtypeStruct((nsmp, D_MODEL), F32),
            compiler_params=cparams(dimension_semantics=("arbitrary",)),
            name=f"sample_finish_{l}",
        )(xs, att_s, o_s, gate, gn3, w_pa, w_pb, w_o, g1, b1)
        xs = pl.pallas_call(
            ffn, grid=(1,),
            in_specs=[_const_spec((nsmp, D_MODEL))] + ffn_w,
            out_specs=_const_spec((nsmp, D_MODEL)),
            out_shape=jax.ShapeDtypeStruct((nsmp, D_MODEL), F32),
            compiler_params=cparams(dimension_semantics=("arbitrary",)),
            name=f"ffn_sample_{l}",
        )(hs, w_u, w_d, g2, b2)

    y_sample = xs.reshape(nsmp, 1, D_MODEL)
    to_rows = lambda t: jnp.transpose(t, (0, 1, 4, 2, 3))
    return (y_prompt, y_sample, to_rows(jnp.stack(pk)), to_rows(jnp.stack(pv)), jnp.stack(pst),
            to_rows(stacked[0]), to_rows(stacked[1]), stacked[2])
```

```python
import functools

import jax
import jax.numpy as jnp
from jax import lax
from jax.experimental import pallas as pl
from jax.experimental.pallas import tpu as pltpu

F32 = jnp.float32
BF16 = jnp.bfloat16

D_MODEL = 1024
PAST_LEN = 8192
N_META = 16
BLOCK = 128
META_PAD = BLOCK - N_META
HEAD_DIM = 64
N_Q_HEADS = 8
N_KV_HEADS = 2
Q_PER_KV = N_Q_HEADS // N_KV_HEADS
ROT_DIM = HEAD_DIM // 4
ROPE_THETA = 500000.0
GLA_HEADS = 4
GLA_DK = 64
GLA_DV = 128
GLA_RANK = 16
GLA_TAU = 16.0
D_FF = 4 * D_MODEL
ATT_W = N_Q_HEADS * HEAD_DIM
KV_W = N_KV_HEADS * HEAD_DIM
GK_W = GLA_HEADS * GLA_DK
GV_W = GLA_HEADS * GLA_DV
LOG2_E = 1.4426950408889634
LN_EPS = 1e-5
RMS_EPS = 1e-6
LANES = 128
SUBLANES = 8
VMEM_LIMIT = 56 * 1024 * 1024

C_Q = 0
C_K = C_Q + ATT_W
C_V = C_K + KV_W
C_GQ = C_V + KV_W
C_GK = C_GQ + GK_W
C_GV = C_GK + GK_W
C_LR = C_GV + GV_W
C_GR = C_LR + GLA_RANK
C_GA = C_GR + GV_W
C_GB = C_GA + D_MODEL
W_IN_COLS = C_GB + D_MODEL
C_LR_END = C_LR + LANES

GLA_FAST_MAX_DECAY = 40.0


def _sigmoid(x):
    return 1.0 / (1.0 + jnp.exp(-x))


def _layer_norm(y, g, b):
    mu = jnp.mean(y, axis=-1, keepdims=True)
    yc = y - mu
    var = jnp.mean(yc * yc, axis=-1, keepdims=True)
    return yc * lax.rsqrt(var + LN_EPS) * g + b


def _rope(t, cos, sin_lo, sin_hi):
    outs = []
    for j in range(t.shape[1] // LANES):
        tj = t[:, j * LANES:(j + 1) * LANES]
        outs.append(tj * cos + pltpu.roll(tj, LANES - ROT_DIM // 2, 1) * sin_lo
                    + pltpu.roll(tj, ROT_DIM // 2, 1) * sin_hi)
    return outs[0] if len(outs) == 1 else jnp.concatenate(outs, axis=1)


def _in_proj(xb, w_t_ref, lo, hi):
    return lax.dot_general(xb, w_t_ref[lo:hi, :], (((1,), (1,)), ((), ())), preferred_element_type=F32)


def _log_decay(glr, w_a2, b_a):
    z = jnp.dot(glr.astype(BF16), w_a2, preferred_element_type=F32) + b_a
    return (jnp.minimum(z, 0.0) - jnp.log1p(jnp.exp(-jnp.abs(z)))) * (1.0 / GLA_TAU)


def _row_to_col(row):
    n = row.shape[1]
    eye = lax.broadcasted_iota(jnp.int32, (n, n), 0) == lax.broadcasted_iota(jnp.int32, (n, n), 1)
    return jnp.sum(jnp.where(eye, jnp.broadcast_to(row, (n, n)), 0.0), axis=1, keepdims=True)


N_SPLIT = 3


def _split3_bf16(a):
    hi = a.astype(BF16)
    r = a - hi.astype(F32)
    mid = r.astype(BF16)
    lo = (r - mid.astype(F32)).astype(BF16)
    return hi, mid, lo


def _swish(x):
    return x * _sigmoid(x)


def _gla_gate_out(o, swish_gr, gn):
    outs = []
    for h in range(GLA_HEADS):
        oh = o[:, h * GLA_DV:(h + 1) * GLA_DV]
        ms = jnp.mean(oh * oh, axis=-1, keepdims=True)
        outs.append(oh * lax.rsqrt(ms + RMS_EPS) * gn)
    return jnp.concatenate(outs, axis=1) * swish_gr


def _finish(x, pa, gla, sig_a, sig_b, w_pb, w_out, alpha):
    pb = jnp.dot(gla.astype(BF16), w_pb, preferred_element_type=F32)
    m = sig_a * pa + sig_b * pb
    return alpha * x + jnp.dot(m.astype(BF16), w_out, preferred_element_type=F32)


_MIXER_REF_NAMES = ("cos", "slo", "shi", "w_in", "w_a2", "b_a", "sink", "gn", "w_pa", "w_pb", "w_out", "g1", "b1",
                    "h", "kwin", "vwin", "sfin",
                    "kprev_scr", "vprev_scr", "s_scr", "sprev_scr", "gq_scr", "gk_scr", "gv_scr", "la_scr",
                    "o_scr", "att_t_scr", "gate_scr", "y_scr")


def _mixer_prompt_kernel(*refs, ts, n_steps, n_total, alpha, from_tokens):
    t = pl.program_id(0)
    named = dict(zip(_MIXER_REF_NAMES, refs[len(refs) - len(_MIXER_REF_NAMES):]))
    h_ref, y_scr, g1_ref, b1_ref = named["h"], named["y_scr"], named["g1"], named["b1"]

    @pl.when(t == 0)
    def _():
        y_scr[...] = jnp.zeros(y_scr.shape, F32)

    @pl.when(t < n_total)
    def _():
        _mixer_step(lax.rem(t, n_steps), *refs, ts=ts, n_steps=n_steps, alpha=alpha, from_tokens=from_tokens)

    @pl.when(t == n_total)
    def _():
        h_ref[0] = _layer_norm(y_scr[...], g1_ref[...], b1_ref[...])


def _mixer_step(s, *refs, ts, n_steps, alpha, from_tokens):
    nblk = ts // BLOCK
    n_x = 1 + nblk if from_tokens else 1
    x_refs = refs[:n_x]
    (cos_ref, slo_ref, shi_ref, w_in_ref, w_a2_ref, b_a_ref, sink_ref, gn_ref,
     w_pa_ref, w_pb_ref, w_out_ref, g1_ref, b1_ref,
     h_ref, kwin_ref, vwin_ref, sfin_ref,
     kprev_scr, vprev_scr, s_scr, sprev_scr, gq_scr, gk_scr, gv_scr, la_scr, o_scr, att_t_scr, gate_scr,
     y_scr) = refs[n_x:]
    assert len(refs) - n_x == len(_MIXER_REF_NAMES)

    def load_x():
        if not from_tokens:
            return x_refs[0][0]
        blocks = [r[0] for r in x_refs[1:]]
        blocks[0] = jnp.where(s == 0, x_refs[0][...], blocks[0])
        return jnp.concatenate(blocks, axis=0)

    @pl.when(s == 0)
    def _():
        kprev_scr[...] = jnp.zeros((N_KV_HEADS, BLOCK, KV_W), BF16)
        vprev_scr[...] = jnp.zeros((KV_W, BLOCK), BF16)
        s_scr[...] = jnp.zeros(s_scr.shape, F32)

    x = load_x()
    xb = x.astype(BF16)
    proj = functools.partial(_in_proj, xb, w_in_ref)

    mix_in = proj(C_Q, C_LR_END)
    live = (s * ts + lax.broadcasted_iota(jnp.int32, (ts, 1), 0)) >= META_PAD
    cos, slo, shi = cos_ref[...], slo_ref[...], shi_ref[...]
    q = _rope(mix_in[:, C_Q:C_K], cos, slo, shi)
    k = _rope(mix_in[:, C_K:C_V], cos, slo, shi)
    v = mix_in[:, C_V:C_GQ]
    gq = mix_in[:, C_GQ:C_GK] * (GLA_DK ** -0.5)
    gk = jnp.where(live, mix_in[:, C_GK:C_GV], 0.0)
    gv = jnp.where(live, mix_in[:, C_GV:C_LR], 0.0)
    v_t = v.T

    @pl.when(s == n_steps - 1)
    def _():
        kwin_ref[0] = k[ts - BLOCK:, :].T
        vwin_ref[0] = v_t[:, ts - BLOCK:]

    gate_w = W_IN_COLS - C_GR
    piece_cols = 4 * LANES
    n_gate_pieces = min(N_KV_HEADS * nblk, -(-gate_w // piece_cols))
    gate_edges = [min(gate_w, piece_cols * (-(-gate_w // piece_cols) * i // n_gate_pieces))
                  for i in range(n_gate_pieces)] + [gate_w]

    def gate_piece_matmul(i):
        return proj(C_GR + gate_edges[i], C_GR + gate_edges[i + 1])

    def gate_piece_store(i, val):
        lo, hi = gate_edges[i], gate_edges[i + 1]
        mid = min(max(GV_W, lo), hi)
        if mid > lo:
            gate_scr[:, lo:mid] = _swish(val[:, :mid - lo])
        if hi > mid:
            gate_scr[:, mid:hi] = _sigmoid(val[:, mid - lo:])

    tri = (lax.broadcasted_iota(jnp.int32, (BLOCK, BLOCK), 1)
           <= lax.broadcasted_iota(jnp.int32, (BLOCK, BLOCK), 0))
    tri_bf = jnp.where(tri, 1.0, 0.0).astype(BF16)
    head_of_k = lax.broadcasted_iota(jnp.int32, (BLOCK, GK_W), 1) // GLA_DK
    head_of_v = lax.broadcasted_iota(jnp.int32, (BLOCK, GV_W), 1) // GLA_DV
    state_diag = (lax.broadcasted_iota(jnp.int32, (GK_W, GV_W), 0) // GLA_DK
                  == lax.broadcasted_iota(jnp.int32, (GK_W, GV_W), 1) // GLA_DV)
    causal = (lax.broadcasted_iota(jnp.int32, (BLOCK, GLA_HEADS * BLOCK), 1) % BLOCK
              <= lax.broadcasted_iota(jnp.int32, (BLOCK, GLA_HEADS * BLOCK), 0))
    gla = {"cums": [], "chunks": []}

    def issue_decay():
        gla["la"] = jnp.where(live, _log_decay(mix_in[:, C_LR:C_LR_END], w_a2_ref[...], b_a_ref[...]), 0.0)

    def issue_cumsum():
        worst = jnp.zeros((1, GK_W), F32)
        for c in range(nblk):
            parts = jnp.concatenate(_split3_bf16(gla["la"][c * BLOCK:(c + 1) * BLOCK]), axis=1)
            b3 = jnp.dot(tri_bf, parts, preferred_element_type=F32)
            b = b3[:, 0:GK_W] + b3[:, GK_W:2 * GK_W] + b3[:, 2 * GK_W:3 * GK_W]
            gla["cums"].append(b)
            worst = jnp.maximum(worst, -b[BLOCK - 1:BLOCK, :])
        gla["worst"] = worst

    def issue_chunk(c):
        rows = slice(c * BLOCK, (c + 1) * BLOCK)
        b = gla["cums"][c]
        b_last = b[BLOCK - 1:BLOCK, :]
        kc = gk[rows]
        q_dec = (gq[rows] * jnp.exp(b)).astype(BF16)
        k_inv = (kc * jnp.exp(-b)).astype(BF16)
        k_end_t = (kc * jnp.exp(b_last - b)).T.astype(BF16)
        vc = gv[rows].astype(BF16)
        zk = jnp.zeros_like(k_inv)
        k_bd = jnp.concatenate([jnp.where(head_of_k == h, k_inv, zk) for h in range(GLA_HEADS)], axis=0)
        a = lax.dot_general(q_dec, k_bd, (((1,), (1,)), ((), ())), preferred_element_type=F32)
        a = jnp.where(causal, a, 0.0).astype(BF16)
        zv = jnp.zeros_like(vc)
        v_bd = jnp.concatenate([jnp.where(head_of_v == h, vc, zv) for h in range(GLA_HEADS)], axis=0)
        o_intra = jnp.dot(a, v_bd, preferred_element_type=F32)
        ds = jnp.where(state_diag, jnp.dot(k_end_t, vc, preferred_element_type=F32), 0.0)
        gla["chunks"].append((rows, q_dec, o_intra, ds, _row_to_col(jnp.exp(b_last))))

    n_pairs = N_KV_HEADS * nblk
    second = min(1, n_pairs - 1)
    side_work = {}
    for pair, issue in ([(0, issue_decay), (second, issue_cumsum)]
                        + [(max(second, n_pairs - nblk + c), functools.partial(issue_chunk, c)) for c in range(nblk)]):
        side_work.setdefault(pair, []).append(issue)

    q_bf = (q * (HEAD_DIM ** -0.5 * LOG2_E)).astype(BF16)
    low_half = lax.broadcasted_iota(jnp.int32, (1, KV_W), 1) < HEAD_DIM
    k_swapped = pltpu.roll(k, HEAD_DIM, 1)
    k_dup = [jnp.where(low_half, k, k_swapped).astype(BF16), jnp.where(low_half, k_swapped, k).astype(BF16)]
    k_keys = [jnp.concatenate([kprev_scr[i], k_dup[i]], axis=0) for i in range(N_KV_HEADS)]
    vt_bf = v_t.astype(BF16)
    vt_keys = jnp.concatenate([vprev_scr[...], vt_bf], axis=1)
    kj = lax.broadcasted_iota(jnp.int32, (2 * BLOCK, Q_PER_KV * BLOCK), 0)
    qi = lax.broadcasted_iota(jnp.int32, (2 * BLOCK, Q_PER_KV * BLOCK), 1) % BLOCK
    band = (kj - qi >= 1) & (kj - qi <= BLOCK)
    q_low_half = lax.broadcasted_iota(jnp.int32, (BLOCK, LANES), 1) < HEAD_DIM
    pieces_done = 0
    for blk in range(nblk):
        first_key_slot = (s * nblk + blk - 1) * BLOCK
        valid = band & (kj + first_key_slot >= META_PAD)
        r0 = blk * BLOCK
        for kv in range(N_KV_HEADS):
            heads = [kv * Q_PER_KV + g for g in range(Q_PER_KV)]
            q_rows = []
            for hq in heads:
                grp = q_bf[r0:r0 + BLOCK, (hq // 2) * LANES:(hq // 2 + 1) * LANES]
                own = q_low_half if hq % 2 == 0 else jnp.logical_not(q_low_half)
                q_rows.append(jnp.where(own, grp, jnp.zeros_like(grp)))
            st = lax.dot_general(k_keys[kv][r0:r0 + 2 * BLOCK, :], jnp.concatenate(q_rows, axis=0),
                                 (((1,), (1,)), ((), ())), preferred_element_type=F32)
            gate_val = gate_piece_matmul(pieces_done) if pieces_done < n_gate_pieces else None
            for issue in side_work.get(blk * N_KV_HEADS + kv, ()):
                issue()
            st = jnp.where(valid, st, -jnp.inf)
            sink_row = jnp.concatenate([jnp.full((1, BLOCK), sink_ref[hq] * LOG2_E, F32) for hq in heads], axis=1)
            m = jnp.maximum(jnp.max(st, axis=0, keepdims=True), sink_row)
            p = jnp.exp2(st - m)
            den = jnp.sum(p, axis=0, keepdims=True) + jnp.exp2(sink_row - m)
            ot = jnp.dot(vt_keys[kv * HEAD_DIM:(kv + 1) * HEAD_DIM, r0:r0 + 2 * BLOCK], p.astype(BF16),
                         preferred_element_type=F32) * (1.0 / den)
            for g, hq in enumerate(heads):
                att_t_scr[hq * HEAD_DIM:(hq + 1) * HEAD_DIM, r0:r0 + BLOCK] = ot[:, g * BLOCK:(g + 1) * BLOCK]
            if gate_val is not None:
                gate_piece_store(pieces_done, gate_val)
                pieces_done += 1
    for i in range(pieces_done, n_gate_pieces):
        gate_piece_store(i, gate_piece_matmul(i))
    for i in range(N_KV_HEADS):
        kprev_scr[i] = k_dup[i][ts - BLOCK:, :]
    vprev_scr[...] = vt_bf[:, ts - BLOCK:]

    fast_ok = jnp.max(gla["worst"]) <= GLA_FAST_MAX_DECAY

    def proj_att():
        return jnp.dot(att_t_scr[...].T.astype(BF16), w_pa_ref[...], preferred_element_type=F32)

    def gla_fast():
        pa = proj_att()
        for rows, q_dec, o_intra, ds, decay_col in gla["chunks"]:
            s0 = s_scr[...]
            o_scr[rows, :] = o_intra + jnp.dot(q_dec, s0.astype(BF16), preferred_element_type=F32)
            s_scr[...] = decay_col * s0 + ds
        return pa

    def gla_slow():
        def body(i, carry):
            rows = pl.ds(pl.multiple_of(i * SUBLANES, SUBLANES), SUBLANES)
            la8, k8, q8, v8 = la_scr[rows, :], gk_scr[rows, :], gq_scr[rows, :], gv_scr[rows, :]
            outs = []
            for r in range(SUBLANES):
                a_col = _row_to_col(jnp.exp(la8[r:r + 1]))
                k_col = _row_to_col(k8[r:r + 1])
                q_col = _row_to_col(q8[r:r + 1])
                s1 = a_col * s_scr[...] + jnp.where(state_diag, k_col * v8[r:r + 1], 0.0)
                s_scr[...] = s1
                outs.append(jnp.sum(q_col * s1, axis=0, keepdims=True))
            o_scr[rows, :] = jnp.concatenate(outs, axis=0)
            return carry
        lax.fori_loop(0, ts // SUBLANES, body, 0)

    def finish_step(x_val, pa):
        gla = _gla_gate_out(o_scr[...], gate_scr[:, 0:GV_W], gn_ref[...])
        y_scr[...] = _finish(x_val, pa, gla, gate_scr[:, GV_W:GV_W + D_MODEL], gate_scr[:, GV_W + D_MODEL:],
                             w_pb_ref[...], w_out_ref[...], alpha)

    sprev_scr[...] = s_scr[...]
    pa = gla_fast()
    h_ref[0] = _layer_norm(y_scr[...], g1_ref[...], b1_ref[...])
    finish_step(x, pa)

    @pl.when(jnp.logical_not(fast_ok))
    def _():
        x_again = load_x()
        proj_again = functools.partial(_in_proj, x_again.astype(BF16), w_in_ref)
        gq_scr[...] = proj_again(C_GQ, C_GK) * (GLA_DK ** -0.5)
        gk_scr[...] = jnp.where(live, proj_again(C_GK, C_GV), 0.0)
        gv_scr[...] = jnp.where(live, proj_again(C_GV, C_LR), 0.0)
        la_scr[...] = jnp.where(live, _log_decay(proj_again(C_LR, C_LR_END), w_a2_ref[...], b_a_ref[...]), 0.0)
        s_scr[...] = sprev_scr[...]
        gla_slow()
        finish_step(x_again, proj_att())

    @pl.when(s == n_steps - 1)
    def _():
        for h in range(GLA_HEADS):
            sfin_ref[0, h] = s_scr[h * GLA_DK:(h + 1) * GLA_DK, h * GLA_DV:(h + 1) * GLA_DV]


def _ffn_kernel(*refs, alpha, col_chunk):
    h_refs, (w_up_ref, w_dn_ref, g_ref, b_ref, o_ref) = refs[:-5], refs[-5:]
    if len(h_refs) == 1:
        h = h_refs[0][...]
    else:
        h = jnp.concatenate([r[0] for r in h_refs], axis=0)
    hb = h.astype(BF16)
    acc = jnp.zeros(h.shape, F32)
    for c in range(D_FF // col_chunk):
        u = jnp.dot(hb, w_up_ref[:, c * col_chunk:(c + 1) * col_chunk], preferred_element_type=F32)
        u = jnp.maximum(u, 0.0)
        acc = acc + jnp.dot((u * u).astype(BF16), w_dn_ref[c * col_chunk:(c + 1) * col_chunk, :],
                            preferred_element_type=F32)
    o_ref[...] = _layer_norm(alpha * h + acc, g_ref[...], b_ref[...]).reshape(o_ref.shape)


def _sample_proj_kernel(x_ref, cos_ref, slo_ref, shi_ref, w_in_ref, w_a2_ref, b_a_ref,
                        qkv_ref, gv_ref, gate_ref, kvt_ref, gcol_ref):
    proj = functools.partial(_in_proj, x_ref[...].astype(BF16), w_in_ref)

    cos, slo, shi = cos_ref[...], slo_ref[...], shi_ref[...]
    k = _rope(proj(C_K, C_V), cos, slo, shi)
    v = proj(C_V, C_GQ)
    qkv_ref[:, C_Q:C_K] = _rope(proj(C_Q, C_K), cos, slo, shi)
    qkv_ref[:, C_K:C_V] = k
    qkv_ref[:, C_V:C_GQ] = v
    gv_ref[...] = proj(C_GV, C_LR)
    gate_ref[...] = proj(C_GR, W_IN_COLS)
    def store_planes(ref, i, t):
        for p, part in enumerate(_split3_bf16(t)):
            ref[N_SPLIT * i + p] = part.astype(F32).T

    store_planes(kvt_ref, 0, k)
    store_planes(kvt_ref, 1, v)
    store_planes(gcol_ref, 0, jnp.exp(_log_decay(proj(C_LR, C_LR_END), w_a2_ref[...], b_a_ref[...])))
    store_planes(gcol_ref, 1, proj(C_GQ, C_GK) * (GLA_DK ** -0.5))
    store_planes(gcol_ref, 2, proj(C_GK, C_GV))


def _sample_mix_kernel(qkv_ref, gv_ref, kvt_ref, gcol_ref, ck_ref, cv_ref, st_ref, sink_ref, *rest, group):
    att_ref, o_ref, nk_ref, nv_ref, nst_ref = rest[-5:]
    head_row = lax.broadcasted_iota(jnp.int32, (N_Q_HEADS, ATT_W), 0)
    head_lane = lax.broadcasted_iota(jnp.int32, (N_Q_HEADS, ATT_W), 1) // HEAD_DIM
    own = head_row == head_lane
    r8 = lax.broadcasted_iota(jnp.int32, (N_Q_HEADS, KV_W), 0)
    swap = (r8 % 2) != (r8 // Q_PER_KV)
    key_i = lax.broadcasted_iota(jnp.int32, (N_Q_HEADS, BLOCK), 1)
    last_row = lax.broadcasted_iota(jnp.int32, (KV_W, BLOCK), 1) == BLOCK - 1
    sink = sink_ref[...][:, 0:1]

    qkv8 = qkv_ref[...]
    gv8 = gv_ref[...]
    q8s, scores = [], []
    for j in range(group):
        q_row = qkv8[j:j + 1, C_Q:C_K]
        qm = jnp.where(own, jnp.broadcast_to(q_row, (N_Q_HEADS, ATT_W)), 0.0)
        fold = qm[:, 0:128] + qm[:, 128:256] + qm[:, 256:384] + qm[:, 384:512]
        q8 = (jnp.where(swap, pltpu.roll(fold, HEAD_DIM, 1), fold) * (HEAD_DIM ** -0.5)).astype(BF16)
        q8s.append(q8)
        scores.append(jnp.dot(q8, ck_ref[j].reshape(KV_W, BLOCK).astype(BF16), preferred_element_type=F32))

    sel = (lax.broadcasted_iota(jnp.int32, (N_SPLIT * group, group * LANES), 0) % group
           == lax.broadcasted_iota(jnp.int32, (N_SPLIT * group, group * LANES), 1) // LANES)
    sel = jnp.where(sel, 1.0, 0.0).astype(BF16)
    spread = lambda ref, i: jnp.dot(ref[i].astype(BF16), sel, preferred_element_type=F32)
    k_cols, v_cols = spread(kvt_ref, 0), spread(kvt_ref, 1)
    a_cols, q_cols, k_gla_cols = (spread(gcol_ref, i) for i in range(3))

    probs = []
    for j in range(group):
        k_new = qkv8[j:j + 1, C_K:C_V]
        sc = jnp.where(key_i >= 1, scores[j], -jnp.inf)
        s_new = jnp.sum(q8s[j].astype(F32) * k_new.astype(BF16).astype(F32), axis=-1, keepdims=True)
        m = jnp.maximum(jnp.maximum(jnp.max(sc, axis=-1, keepdims=True), s_new), sink)
        p = jnp.exp(sc - m)
        p_new = jnp.exp(s_new - m)
        den = jnp.sum(p, axis=-1, keepdims=True) + p_new + jnp.exp(sink - m)
        probs.append((p.astype(BF16), p_new, den))

    att_rows, o_rows = [], []
    for j in range(group):
        p_bf, p_new, den = probs[j]
        v_new = qkv8[j:j + 1, C_V:C_GQ]
        o8 = (lax.dot_general(p_bf, cv_ref[j].reshape(KV_W, BLOCK).astype(BF16), (((1,), (1,)), ((), ())),
                              preferred_element_type=F32)
              + p_new.astype(BF16).astype(F32) * v_new.astype(BF16).astype(F32)) / den
        o8 = jnp.where(swap, pltpu.roll(o8, HEAD_DIM, 1), o8)
        o_wide = jnp.concatenate([o8, o8, o8, o8], axis=1)
        att_rows.append(jnp.sum(jnp.where(own, o_wide, 0.0), axis=0, keepdims=True))

    for j in range(group):
        lanes_j = slice(j * LANES, (j + 1) * LANES)
        k_old = ck_ref[j].reshape(KV_W, BLOCK)
        v_old = cv_ref[j].reshape(KV_W, BLOCK)
        nk_ref[j] = jnp.where(last_row, k_cols[:, lanes_j], pltpu.roll(k_old, BLOCK - 1, 1)).reshape(
            N_KV_HEADS, HEAD_DIM, BLOCK)
        nv_ref[j] = jnp.where(last_row, v_cols[:, lanes_j], pltpu.roll(v_old, BLOCK - 1, 1)).reshape(
            N_KV_HEADS, HEAD_DIM, BLOCK)
        a_col, q_col, k_col = a_cols[:, lanes_j], q_cols[:, lanes_j], k_gla_cols[:, lanes_j]
        v_row = gv8[j:j + 1, :]
        v_exp = jnp.concatenate([jnp.broadcast_to(v_row[:, h * GLA_DV:(h + 1) * GLA_DV], (GLA_DK, GLA_DV))
                                 for h in range(GLA_HEADS)], axis=0)
        s1 = a_col * st_ref[j].reshape(GK_W, GLA_DV) + k_col * v_exp
        nst_ref[j] = s1.reshape(GLA_HEADS, GLA_DK, GLA_DV)
        qs = q_col * s1
        o_rows.append(jnp.concatenate(
            [jnp.sum(qs[h * GLA_DK:(h + 1) * GLA_DK], axis=0, keepdims=True) for h in range(GLA_HEADS)], axis=1))
    att_ref[...] = jnp.concatenate(att_rows, axis=0)
    o_ref[...] = jnp.concatenate(o_rows, axis=0)


def _sample_finish_kernel(x_ref, att_ref, o_ref, gate_ref, gn_ref, w_pa_ref, w_pb_ref, w_out_ref, g1_ref, b1_ref,
                          h_ref, *, alpha):
    gla = _gla_gate_out(o_ref[...], _swish(gate_ref[:, 0:GV_W]), gn_ref[...])
    pa = jnp.dot(att_ref[...].astype(BF16), w_pa_ref[...], preferred_element_type=F32)
    y = _finish(x_ref[...], pa, gla, _sigmoid(gate_ref[:, GV_W:GV_W + D_MODEL]),
                _sigmoid(gate_ref[:, GV_W + D_MODEL:]), w_pb_ref[...], w_out_ref[...], alpha)
    h_ref[...] = _layer_norm(y, g1_ref[...], b1_ref[...])


def _rope_tables(pos):
    half = ROT_DIM // 2
    inv = ROPE_THETA ** (-jnp.arange(half, dtype=F32) * 2.0 / ROT_DIM)
    d = jnp.arange(LANES) % HEAD_DIM
    ang = pos.astype(F32)[:, None] * inv[d % half][None, :]
    cos, sin = jnp.cos(ang), jnp.sin(ang)
    cos_t = jnp.where(d < ROT_DIM, cos, 1.0)
    sin_lo = jnp.where(d < half, -sin, 0.0)
    sin_hi = jnp.where((d >= half) & (d < ROT_DIM), sin, 0.0)
    return cos_t, sin_lo, sin_hi


def _const_spec(shape, layer=None):
    if layer is None:
        return pl.BlockSpec(shape, lambda *_: (0,) * len(shape), pipeline_mode=pl.Buffered(1))
    return pl.BlockSpec((None,) + shape, lambda *_: (layer,) + (0,) * len(shape), pipeline_mode=pl.Buffered(1))


def _step_rows(total):
    for t in (384, 256, 128):
        if total % t == 0:
            return t
    raise ValueError("padded prompt length must be a multiple of 128")


def kernel(x_prompt, x_sample, cache_k_win, cache_v_win, state_gla, meta_tokens, w_in, w_a2, b_a, attn_sink,
           gla_norm_g, w_proj_a, w_proj_b, w_out, ln1_g, ln1_b, w_up, w_down, ln2_g, ln2_b):
    depth = w_in.shape[0]
    bsz, seq, _ = x_prompt.shape
    nsmp, dec_seq, _ = x_sample.shape
    assert dec_seq == 1 and cache_k_win.shape[2] == BLOCK and seq % BLOCK == 0
    alpha = (2 * depth) ** 0.25
    lp = seq + BLOCK
    ts = _step_rows(lp)
    n_steps = lp // ts
    rows = bsz * lp
    ffn_tile = next(t for t in (512, 384, 256, 128) if rows % t == 0)
    last_tile = next(t for t in (512, 384, 256, 128) if seq % t == 0)
    group = next(g for g in (2 * SUBLANES, SUBLANES) if nsmp % g == 0)

    w_in_r = jnp.swapaxes(w_in, 1, 2).astype(BF16)
    w_a2_p = jnp.concatenate([w_a2, jnp.zeros((depth, LANES - GLA_RANK, GK_W), w_a2.dtype)], axis=1).astype(BF16)
    w_pa, w_pb, w_o = w_proj_a.astype(BF16), w_proj_b.astype(BF16), w_out.astype(BF16)
    w_u, w_d = w_up.astype(BF16), w_down.astype(BF16)
    b_a3 = b_a.reshape(depth, 1, GK_W)
    gn3 = gla_norm_g.reshape(depth, 1, GLA_DV)
    g1, b1 = ln1_g.reshape(depth, 1, D_MODEL), ln1_b.reshape(depth, 1, D_MODEL)
    g2, b2 = ln2_g.reshape(depth, 1, D_MODEL), ln2_b.reshape(depth, 1, D_MODEL)
    sink_lanes = jnp.broadcast_to(attn_sink[:, :, None], (depth, N_Q_HEADS, LANES))

    cos_p, slo_p, shi_p = _rope_tables(jnp.arange(lp) - META_PAD)
    cos_s, slo_s, shi_s = (jnp.broadcast_to(t, (nsmp, LANES)) for t in _rope_tables(PAST_LEN + jnp.arange(1)))

    meta_block = jnp.concatenate([jnp.zeros((META_PAD, D_MODEL), x_prompt.dtype),
                                  meta_tokens.astype(x_prompt.dtype)], axis=0)
    nblk = ts // BLOCK
    xp = None
    xs = x_sample.reshape(nsmp, D_MODEL)
    ck = jnp.transpose(cache_k_win, (0, 1, 3, 4, 2))
    cv = jnp.transpose(cache_v_win, (0, 1, 3, 4, 2))

    cparams = functools.partial(pltpu.CompilerParams, vmem_limit_bytes=VMEM_LIMIT)
    pk, pv, pst = [], [], []
    stacked = []
    for l in range(depth):
        n_total = bsz * n_steps
        seq_of = lambda t: jnp.minimum(t, n_total - 1) // n_steps
        step_of = lambda t: jnp.minimum(t, n_total - 1) % n_steps
        step_spec = pl.BlockSpec((1, ts, D_MODEL), lambda t: (seq_of(t), step_of(t), 0))
        h_spec = pl.BlockSpec((1, ts, D_MODEL), lambda t: (seq_of(jnp.maximum(t - 1, 0)), step_of(jnp.maximum(t - 1, 0)), 0))
        tab_spec = pl.BlockSpec((ts, LANES), lambda t: (step_of(t), 0))
        if l == 0:
            x_specs = [_const_spec((BLOCK, D_MODEL))] + [
                pl.BlockSpec((1, BLOCK, D_MODEL),
                             lambda t, j=j: (seq_of(t), jnp.maximum(nblk * step_of(t) + j - 1, 0), 0))
                for j in range(nblk)]
            x_args = [meta_block] + [x_prompt] * nblk
        else:
            x_specs, x_args = [step_spec], [xp]
        hp, kwin, vwin, sfin = pl.pallas_call(
            functools.partial(_mixer_prompt_kernel, ts=ts, n_steps=n_steps, n_total=n_total, alpha=alpha,
                              from_tokens=(l == 0)),
            grid=(n_total + 1,),
            in_specs=x_specs + [tab_spec, tab_spec, tab_spec,
                      _const_spec((W_IN_COLS, D_MODEL), l), _const_spec((LANES, GK_W), l), _const_spec((1, GK_W), l),
                      pl.BlockSpec(memory_space=pltpu.SMEM), _const_spec((1, GLA_DV), l),
                      _const_spec((ATT_W, D_MODEL), l), _const_spec((GV_W, D_MODEL), l),
                      _const_spec((D_MODEL, D_MODEL), l), _const_spec((1, D_MODEL), l), _const_spec((1, D_MODEL), l)],
            out_specs=[h_spec,
                       pl.BlockSpec((1, BLOCK, KV_W), lambda t: (seq_of(t), 0, 0)),
                       pl.BlockSpec((1, BLOCK, KV_W), lambda t: (seq_of(t), 0, 0)),
                       pl.BlockSpec((1, GLA_HEADS, GLA_DK, GLA_DV), lambda t: (seq_of(t), 0, 0, 0))],
            out_shape=[jax.ShapeDtypeStruct((bsz, lp, D_MODEL), F32),
                       jax.ShapeDtypeStruct((bsz, BLOCK, KV_W), F32),
                       jax.ShapeDtypeStruct((bsz, BLOCK, KV_W), F32),
                       jax.ShapeDtypeStruct((bsz, GLA_HEADS, GLA_DK, GLA_DV), F32)],
            scratch_shapes=[pltpu.VMEM((N_KV_HEADS, BLOCK, KV_W), BF16), pltpu.VMEM((KV_W, BLOCK), BF16),
                            pltpu.VMEM((GK_W, GV_W), F32), pltpu.VMEM((GK_W, GV_W), F32),
                            pltpu.VMEM((ts, GK_W), F32), pltpu.VMEM((ts, GK_W), F32), pltpu.VMEM((ts, GV_W), F32),
                            pltpu.VMEM((ts, GK_W), F32), pltpu.VMEM((ts, GV_W), F32), pltpu.VMEM((ATT_W, ts), F32),
                            pltpu.VMEM((ts, W_IN_COLS - C_GR), F32), pltpu.VMEM((ts, D_MODEL), F32)],
            compiler_params=cparams(dimension_semantics=("arbitrary",)),
            name=f"mixer_prompt_{l}",
        )(*x_args, cos_p, slo_p, shi_p, w_in_r, w_a2_p, b_a3, attn_sink[l], gn3, w_pa, w_pb, w_o, g1, b1)
        pk.append(kwin.reshape(bsz, N_KV_HEADS, HEAD_DIM, BLOCK))
        pv.append(vwin.reshape(bsz, N_KV_HEADS, HEAD_DIM, BLOCK))
        pst.append(sfin)

        ffn = functools.partial(_ffn_kernel, alpha=alpha, col_chunk=1024)
        ffn_w = [_const_spec((D_MODEL, D_FF), l), _const_spec((D_FF, D_MODEL), l),
                 _const_spec((1, D_MODEL), l), _const_spec((1, D_MODEL), l)]
        if l < depth - 1:
            xp = pl.pallas_call(
                ffn, grid=(rows // ffn_tile,),
                in_specs=[pl.BlockSpec((ffn_tile, D_MODEL), lambda i: (i, 0))] + ffn_w,
                out_specs=pl.BlockSpec((ffn_tile, D_MODEL), lambda i: (i, 0)),
                out_shape=jax.ShapeDtypeStruct((rows, D_MODEL), F32),
                compiler_params=cparams(dimension_semantics=("arbitrary",)),
                name=f"ffn_prompt_{l}",
            )(hp.reshape(rows, D_MODEL), w_u, w_d, g2, b2).reshape(bsz, lp, D_MODEL)
        else:
            pieces = last_tile // BLOCK
            y_prompt = pl.pallas_call(
                ffn, grid=(bsz, seq // last_tile),
                in_specs=[pl.BlockSpec((1, BLOCK, D_MODEL), lambda b, i, j=j: (b, 1 + pieces * i + j, 0))
                          for j in range(pieces)] + ffn_w,
                out_specs=pl.BlockSpec((1, last_tile, D_MODEL), lambda b, i: (b, i, 0)),
                out_shape=jax.ShapeDtypeStruct((bsz, seq, D_MODEL), F32),
                compiler_params=cparams(dimension_semantics=("arbitrary", "arbitrary")),
                name=f"ffn_prompt_{l}",
            )(*([hp] * pieces), w_u, w_d, g2, b2)

        proj_out = [(nsmp, C_GQ), (nsmp, GV_W), (nsmp, W_IN_COLS - C_GR),
                    (2 * N_SPLIT, KV_W, nsmp), (3 * N_SPLIT, GK_W, nsmp)]
        qkv, gv_s, gate, kvt, gcol = pl.pallas_call(
            _sample_proj_kernel, grid=(1,),
            in_specs=[_const_spec((nsmp, D_MODEL)), _const_spec((nsmp, LANES)), _const_spec((nsmp, LANES)),
                      _const_spec((nsmp, LANES)), _const_spec((W_IN_COLS, D_MODEL), l),
                      _const_spec((LANES, GK_W), l), _const_spec((1, GK_W), l)],
            out_specs=[_const_spec(s) for s in proj_out],
            out_shape=[jax.ShapeDtypeStruct(s, F32) for s in proj_out],
            compiler_params=cparams(dimension_semantics=("arbitrary",)),
            name=f"sample_proj_{l}",
        )(xs, cos_s, slo_s, shi_s, w_in_r, w_a2_p, b_a3)
        def by_group(t):
            t = t.reshape(t.shape[0] // N_SPLIT, N_SPLIT, t.shape[1], nsmp // group, group)
            return jnp.transpose(t, (3, 0, 2, 1, 4)).reshape(nsmp // group, t.shape[0], t.shape[2], N_SPLIT * group)
        col_spec = lambda n, width: pl.BlockSpec((None, n, width, N_SPLIT * group), lambda i: (i, 0, 0, 0))

        grp = lambda width: pl.BlockSpec((group, width), lambda i: (i, 0))
        cache_spec = pl.BlockSpec((None, group, N_KV_HEADS, HEAD_DIM, BLOCK), lambda i: (l, i, 0, 0, 0))
        state_spec = pl.BlockSpec((None, group, GLA_HEADS, GLA_DK, GLA_DV), lambda i: (l, i, 0, 0, 0))
        n_mix_in = 8
        att_s, o_s, *stacked = pl.pallas_call(
            functools.partial(_sample_mix_kernel, group=group), grid=(nsmp // group,),
            in_specs=[grp(C_GQ), grp(GV_W), col_spec(2, KV_W), col_spec(3, GK_W), cache_spec, cache_spec, state_spec,
                      _const_spec((N_Q_HEADS, LANES), l)] + [pl.BlockSpec(memory_space=pl.ANY)] * len(stacked),
            out_specs=[grp(ATT_W), grp(GV_W), cache_spec, cache_spec, state_spec],
            out_shape=[jax.ShapeDtypeStruct((nsmp, ATT_W), F32), jax.ShapeDtypeStruct((nsmp, GV_W), F32),
                       jax.ShapeDtypeStruct(ck.shape, F32), jax.ShapeDtypeStruct(cv.shape, F32),
                       jax.ShapeDtypeStruct(state_gla.shape, F32)],
            input_output_aliases={n_mix_in + i: 2 + i for i in range(len(stacked))},
            compiler_params=cparams(dimension_semantics=("arbitrary",)),
            name=f"sample_mix_{l}",
        )(qkv, gv_s, by_group(kvt), by_group(gcol), ck, cv, state_gla, sink_lanes, *stacked)

        hs = pl.pallas_call(
            functools.partial(_sample_finish_kernel, alpha=alpha), grid=(1,),
            in_specs=[_const_spec((nsmp, D_MODEL)), _const_spec((nsmp, ATT_W)), _const_spec((nsmp, GV_W)),
                      _const_spec((nsmp, W_IN_COLS - C_GR)), _const_spec((1, GLA_DV), l),
                      _const_spec((ATT_W, D_MODEL), l), _const_spec((GV_W, D_MODEL), l),
                      _const_spec((D_MODEL, D_MODEL), l), _const_spec((1, D_MODEL), l), _const_spec((1, D_MODEL), l)],
            out_specs=_const_spec((nsmp, D_MODEL)),
            out_shape=jax.ShapeDtypeStruct((nsmp, D_MODEL), F32),
            compiler_params=cparams(dimension_semantics=("arbitrary",)),
            name=f"sample_finish_{l}",
        )(xs, att_s, o_s, gate, gn3, w_pa, w_pb, w_o, g1, b1)
        xs = pl.pallas_call(
            ffn, grid=(1,),
            in_specs=[_const_spec((nsmp, D_MODEL))] + ffn_w,
            out_specs=_const_spec((nsmp, D_MODEL)),
            out_shape=jax.ShapeDtypeStruct((nsmp, D_MODEL), F32),
            compiler_params=cparams(dimension_semantics=("arbitrary",)),
            name=f"ffn_sample_{l}",
        )(hs, w_u, w_d, g2, b2)

    y_sample = xs.reshape(nsmp, 1, D_MODEL)
    to_rows = lambda t: jnp.transpose(t, (0, 1, 4, 2, 3))
    return (y_prompt, y_sample, to_rows(jnp.stack(pk)), to_rows(jnp.stack(pv)), jnp.stack(pst),
            to_rows(stacked[0]), to_rows(stacked[1]), stacked[2])
```

```python
import functools

import jax
import jax.numpy as jnp
from jax import lax
from jax.experimental import pallas as pl
from jax.experimental.pallas import tpu as pltpu

F32 = jnp.float32
BF16 = jnp.bfloat16

D_MODEL = 1024
PAST_LEN = 8192
N_META = 16
BLOCK = 128
META_PAD = BLOCK - N_META
HEAD_DIM = 64
N_Q_HEADS = 8
N_KV_HEADS = 2
Q_PER_KV = N_Q_HEADS // N_KV_HEADS
ROT_DIM = HEAD_DIM // 4
ROPE_THETA = 500000.0
GLA_HEADS = 4
GLA_DK = 64
GLA_DV = 128
GLA_RANK = 16
GLA_TAU = 16.0
D_FF = 4 * D_MODEL
ATT_W = N_Q_HEADS * HEAD_DIM
KV_W = N_KV_HEADS * HEAD_DIM
GK_W = GLA_HEADS * GLA_DK
GV_W = GLA_HEADS * GLA_DV
LOG2_E = 1.4426950408889634
LN_EPS = 1e-5
RMS_EPS = 1e-6
LANES = 128
SUBLANES = 8
VMEM_LIMIT = 56 * 1024 * 1024

C_Q = 0
C_K = C_Q + ATT_W
C_V = C_K + KV_W
C_GQ = C_V + KV_W
C_GK = C_GQ + GK_W
C_GV = C_GK + GK_W
C_LR = C_GV + GV_W
C_GR = C_LR + GLA_RANK
C_GA = C_GR + GV_W
C_GB = C_GA + D_MODEL
W_IN_COLS = C_GB + D_MODEL
C_LR_END = C_LR + LANES

GLA_FAST_MAX_DECAY = 40.0


def _sigmoid(x):
    return 1.0 / (1.0 + jnp.exp(-x))


def _layer_norm(y, g, b):
    mu = jnp.mean(y, axis=-1, keepdims=True)
    yc = y - mu
    var = jnp.mean(yc * yc, axis=-1, keepdims=True)
    return yc * lax.rsqrt(var + LN_EPS) * g + b


def _rope(t, cos, sin_lo, sin_hi):
    outs = []
    for j in range(t.shape[1] // LANES):
        tj = t[:, j * LANES:(j + 1) * LANES]
        outs.append(tj * cos + pltpu.roll(tj, LANES - ROT_DIM // 2, 1) * sin_lo
                    + pltpu.roll(tj, ROT_DIM // 2, 1) * sin_hi)
    return outs[0] if len(outs) == 1 else jnp.concatenate(outs, axis=1)


def _in_proj(xb, w_t_ref, lo, hi):
    return lax.dot_general(xb, w_t_ref[lo:hi, :], (((1,), (1,)), ((), ())), preferred_element_type=F32)


def _log_decay(glr, w_a2, b_a):
    z = jnp.dot(glr.astype(BF16), w_a2, preferred_element_type=F32) + b_a
    return (jnp.minimum(z, 0.0) - jnp.log1p(jnp.exp(-jnp.abs(z)))) * (1.0 / GLA_TAU)


def _row_to_col(row):
    n = row.shape[1]
    eye = lax.broadcasted_iota(jnp.int32, (n, n), 0) == lax.broadcasted_iota(jnp.int32, (n, n), 1)
    return jnp.sum(jnp.where(eye, jnp.broadcast_to(row, (n, n)), 0.0), axis=1, keepdims=True)


N_SPLIT = 3


def _split3_bf16(a):
    hi = a.astype(BF16)
    r = a - hi.astype(F32)
    mid = r.astype(BF16)
    lo = (r - mid.astype(F32)).astype(BF16)
    return hi, mid, lo


def _swish(x):
    return x * _sigmoid(x)


def _gla_gate_out(o, swish_gr, gn):
    outs = []
    for h in range(GLA_HEADS):
        oh = o[:, h * GLA_DV:(h + 1) * GLA_DV]
        ms = jnp.mean(oh * oh, axis=-1, keepdims=True)
        outs.append(oh * lax.rsqrt(ms + RMS_EPS) * gn)
    return jnp.concatenate(outs, axis=1) * swish_gr


def _finish(x, pa, gla, sig_a, sig_b, w_pb, w_out, alpha):
    pb = jnp.dot(gla.astype(BF16), w_pb, preferred_element_type=F32)
    m = sig_a * pa + sig_b * pb
    return alpha * x + jnp.dot(m.astype(BF16), w_out, preferred_element_type=F32)


_MIXER_REF_NAMES = ("cos", "slo", "shi", "w_in", "w_a2", "b_a", "sink", "gn", "w_pa", "w_pb", "w_out", "g1", "b1",
                    "h", "kwin", "vwin", "sfin",
                    "kprev_scr", "vprev_scr", "s_scr", "sprev_scr", "gq_scr", "gk_scr", "gv_scr", "la_scr",
                    "o_scr", "att_t_scr", "gate_scr", "y_scr")


def _mixer_prompt_kernel(*refs, ts, n_steps, n_total, alpha, from_tokens):
    t = pl.program_id(0)
    named = dict(zip(_MIXER_REF_NAMES, refs[len(refs) - len(_MIXER_REF_NAMES):]))
    h_ref, y_scr, g1_ref, b1_ref = named["h"], named["y_scr"], named["g1"], named["b1"]

    @pl.when(t == 0)
    def _():
        y_scr[...] = jnp.zeros(y_scr.shape, F32)

    @pl.when(t < n_total)
    def _():
        _mixer_step(lax.rem(t, n_steps), *refs, ts=ts, n_steps=n_steps, alpha=alpha, from_tokens=from_tokens)

    @pl.when(t == n_total)
    def _():
        h_ref[0] = _layer_norm(y_scr[...], g1_ref[...], b1_ref[...])


def _mixer_step(s, *refs, ts, n_steps, alpha, from_tokens):
    nblk = ts // BLOCK
    n_x = 1 + nblk if from_tokens else 1
    x_refs = refs[:n_x]
    (cos_ref, slo_ref, shi_ref, w_in_ref, w_a2_ref, b_a_ref, sink_ref, gn_ref,
     w_pa_ref, w_pb_ref, w_out_ref, g1_ref, b1_ref,
     h_ref, kwin_ref, vwin_ref, sfin_ref,
     kprev_scr, vprev_scr, s_scr, sprev_scr, gq_scr, gk_scr, gv_scr, la_scr, o_scr, att_t_scr, gate_scr,
     y_scr) = refs[n_x:]
    assert len(refs) - n_x == len(_MIXER_REF_NAMES)

    def load_x():
        if not from_tokens:
            return x_refs[0][0]
        blocks = [r[0] for r in x_refs[1:]]
        blocks[0] = jnp.where(s == 0, x_refs[0][...], blocks[0])
        return jnp.concatenate(blocks, axis=0)

    @pl.when(s == 0)
    def _():
        kprev_scr[...] = jnp.zeros((N_KV_HEADS, BLOCK, KV_W), BF16)
        vprev_scr[...] = jnp.zeros((KV_W, BLOCK), BF16)
        s_scr[...] = jnp.zeros(s_scr.shape, F32)

    x = load_x()
    xb = x.astype(BF16)
    proj = functools.partial(_in_proj, xb, w_in_ref)

    mix_in = proj(C_Q, C_LR_END)
    live = (s * ts + lax.broadcasted_iota(jnp.int32, (ts, 1), 0)) >= META_PAD
    cos, slo, shi = cos_ref[...], slo_ref[...], shi_ref[...]
    q = _rope(mix_in[:, C_Q:C_K], cos, slo, shi)
    k = _rope(mix_in[:, C_K:C_V], cos, slo, shi)
    v = mix_in[:, C_V:C_GQ]
    gq = mix_in[:, C_GQ:C_GK] * (GLA_DK ** -0.5)
    gk = jnp.where(live, mix_in[:, C_GK:C_GV], 0.0)
    gv = jnp.where(live, mix_in[:, C_GV:C_LR], 0.0)
    v_t = v.T

    @pl.when(s == n_steps - 1)
    def _():
        kwin_ref[0] = k[ts - BLOCK:, :].T
        vwin_ref[0] = v_t[:, ts - BLOCK:]

    gate_w = W_IN_COLS - C_GR
    piece_cols = 4 * LANES
    n_gate_pieces = min(N_KV_HEADS * nblk, -(-gate_w // piece_cols))
    gate_edges = [min(gate_w, piece_cols * (-(-gate_w // piece_cols) * i // n_gate_pieces))
                  for i in range(n_gate_pieces)] + [gate_w]

    def gate_piece_matmul(i):
        return proj(C_GR + gate_edges[i], C_GR + gate_edges[i + 1])

    def gate_piece_store(i, val):
        lo, hi = gate_edges[i], gate_edges[i + 1]
        mid = min(max(GV_W, lo), hi)
        if mid > lo:
            gate_scr[:, lo:mid] = _swish(val[:, :mid - lo])
        if hi > mid:
            gate_scr[:, mid:hi] = _sigmoid(val[:, mid - lo:])

    tri = (lax.broadcasted_iota(jnp.int32, (BLOCK, BLOCK), 1)
           <= lax.broadcasted_iota(jnp.int32, (BLOCK, BLOCK), 0))
    tri_bf = jnp.where(tri, 1.0, 0.0).astype(BF16)
    head_of_k = lax.broadcasted_iota(jnp.int32, (BLOCK, GK_W), 1) // GLA_DK
    head_of_v = lax.broadcasted_iota(jnp.int32, (BLOCK, GV_W), 1) // GLA_DV
    state_diag = (lax.broadcasted_iota(jnp.int32, (GK_W, GV_W), 0) // GLA_DK
                  == lax.broadcasted_iota(jnp.int32, (GK_W, GV_W), 1) // GLA_DV)
    causal = (lax.broadcasted_iota(jnp.int32, (BLOCK, GLA_HEADS * BLOCK), 1) % BLOCK
              <= lax.broadcasted_iota(jnp.int32, (BLOCK, GLA_HEADS * BLOCK), 0))
    gla = {"cums": [], "chunks": []}

    def issue_decay():
        gla["la"] = jnp.where(live, _log_decay(mix_in[:, C_LR:C_LR_END], w_a2_ref[...], b_a_ref[...]), 0.0)

    def issue_cumsum():
        worst = jnp.zeros((1, GK_W), F32)
        for c in range(nblk):
            parts = jnp.concatenate(_split3_bf16(gla["la"][c * BLOCK:(c + 1) * BLOCK]), axis=1)
            b3 = jnp.dot(tri_bf, parts, preferred_element_type=F32)
            b = b3[:, 0:GK_W] + b3[:, GK_W:2 * GK_W] + b3[:, 2 * GK_W:3 * GK_W]
            gla["cums"].append(b)
            worst = jnp.maximum(worst, -b[BLOCK - 1:BLOCK, :])
        gla["worst"] = worst

    def issue_chunk(c):
        rows = slice(c * BLOCK, (c + 1) * BLOCK)
        b = gla["cums"][c]
        b_last = b[BLOCK - 1:BLOCK, :]
        kc = gk[rows]
        q_dec = (gq[rows] * jnp.exp(b)).astype(BF16)
        k_inv = (kc * jnp.exp(-b)).astype(BF16)
        k_end_t = (kc * jnp.exp(b_last - b)).T.astype(BF16)
        vc = gv[rows].astype(BF16)
        zk = jnp.zeros_like(k_inv)
        k_bd = jnp.concatenate([jnp.where(head_of_k == h, k_inv, zk) for h in range(GLA_HEADS)], axis=0)
        a = lax.dot_general(q_dec, k_bd, (((1,), (1,)), ((), ())), preferred_element_type=F32)
        a = jnp.where(causal, a, 0.0).astype(BF16)
        zv = jnp.zeros_like(vc)
        v_bd = jnp.concatenate([jnp.where(head_of_v == h, vc, zv) for h in range(GLA_HEADS)], axis=0)
        o_intra = jnp.dot(a, v_bd, preferred_element_type=F32)
        ds = jnp.where(state_diag, jnp.dot(k_end_t, vc, preferred_element_type=F32), 0.0)
        gla["chunks"].append((rows, q_dec, o_intra, ds, _row_to_col(jnp.exp(b_last))))

    n_pairs = N_KV_HEADS * nblk
    second = min(1, n_pairs - 1)
    side_work = {}
    for pair, issue in ([(0, issue_decay), (second, issue_cumsum)]
                        + [(max(second, n_pairs - nblk + c), functools.partial(issue_chunk, c)) for c in range(nblk)]):
        side_work.setdefault(pair, []).append(issue)

    q_bf = (q * (HEAD_DIM ** -0.5 * LOG2_E)).astype(BF16)
    low_half = lax.broadcasted_iota(jnp.int32, (1, KV_W), 1) < HEAD_DIM
    k_swapped = pltpu.roll(k, HEAD_DIM, 1)
    k_dup = [jnp.where(low_half, k, k_swapped).astype(BF16), jnp.where(low_half, k_swapped, k).astype(BF16)]
    k_keys = [jnp.concatenate([kprev_scr[i], k_dup[i]], axis=0) for i in range(N_KV_HEADS)]
    vt_bf = v_t.astype(BF16)
    vt_keys = jnp.concatenate([vprev_scr[...], vt_bf], axis=1)
    kj = lax.broadcasted_iota(jnp.int32, (2 * BLOCK, Q_PER_KV * BLOCK), 0)
    qi = lax.broadcasted_iota(jnp.int32, (2 * BLOCK, Q_PER_KV * BLOCK), 1) % BLOCK
    band = (kj - qi >= 1) & (kj - qi <= BLOCK)
    q_low_half = lax.broadcasted_iota(jnp.int32, (BLOCK, LANES), 1) < HEAD_DIM
    pieces_done = 0
    for blk in range(nblk):
        first_key_slot = (s * nblk + blk - 1) * BLOCK
        valid = band & (kj + first_key_slot >= META_PAD)
        r0 = blk * BLOCK
        for kv in range(N_KV_HEADS):
            heads = [kv * Q_PER_KV + g for g in range(Q_PER_KV)]
            q_rows = []
            for hq in heads:
                grp = q_bf[r0:r0 + BLOCK, (hq // 2) * LANES:(hq // 2 + 1) * LANES]
                own = q_low_half if hq % 2 == 0 else jnp.logical_not(q_low_half)
                q_rows.append(jnp.where(own, grp, jnp.zeros_like(grp)))
            st = lax.dot_general(k_keys[kv][r0:r0 + 2 * BLOCK, :], jnp.concatenate(q_rows, axis=0),
                                 (((1,), (1,)), ((), ())), preferred_element_type=F32)
            gate_val = gate_piece_matmul(pieces_done) if pieces_done < n_gate_pieces else None
            for issue in side_work.get(blk * N_KV_HEADS + kv, ()):
                issue()
            st = jnp.where(valid, st, -jnp.inf)
            sink_row = jnp.concatenate([jnp.full((1, BLOCK), sink_ref[hq] * LOG2_E, F32) for hq in heads], axis=1)
            m = jnp.maximum(jnp.max(st, axis=0, keepdims=True), sink_row)
            p = jnp.exp2(st - m)
            den = jnp.sum(p, axis=0, keepdims=True) + jnp.exp2(sink_row - m)
            ot = jnp.dot(vt_keys[kv * HEAD_DIM:(kv + 1) * HEAD_DIM, r0:r0 + 2 * BLOCK], p.astype(BF16),
                         preferred_element_type=F32) * (1.0 / den)
            for g, hq in enumerate(heads):
                att_t_scr[hq * HEAD_DIM:(hq + 1) * HEAD_DIM, r0:r0 + BLOCK] = ot[:, g * BLOCK:(g + 1) * BLOCK]
            if gate_val is not None:
                gate_piece_store(pieces_done, gate_val)
                pieces_done += 1
    for i in range(pieces_done, n_gate_pieces):
        gate_piece_store(i, gate_piece_matmul(i))
    for i in range(N_KV_HEADS):
        kprev_scr[i] = k_dup[i][ts - BLOCK:, :]
    vprev_scr[...] = vt_bf[:, ts - BLOCK:]

    fast_ok = jnp.max(gla["worst"]) <= GLA_FAST_MAX_DECAY

    def proj_att():
        return jnp.dot(att_t_scr[...].T.astype(BF16), w_pa_ref[...], preferred_element_type=F32)

    def gla_fast():
        pa = proj_att()
        for rows, q_dec, o_intra, ds, decay_col in gla["chunks"]:
            s0 = s_scr[...]
            o_scr[rows, :] = o_intra + jnp.dot(q_dec, s0.astype(BF16), preferred_element_type=F32)
            s_scr[...] = decay_col * s0 + ds
        return pa

    def gla_slow():
        def body(i, carry):
            rows = pl.ds(pl.multiple_of(i * SUBLANES, SUBLANES), SUBLANES)
            la8, k8, q8, v8 = la_scr[rows, :], gk_scr[rows, :], gq_scr[rows, :], gv_scr[rows, :]
            outs = []
            for r in range(SUBLANES):
                a_col = _row_to_col(jnp.exp(la8[r:r + 1]))
                k_col = _row_to_col(k8[r:r + 1])
                q_col = _row_to_col(q8[r:r + 1])
                s1 = a_col * s_scr[...] + jnp.where(state_diag, k_col * v8[r:r + 1], 0.0)
                s_scr[...] = s1
                outs.append(jnp.sum(q_col * s1, axis=0, keepdims=True))
            o_scr[rows, :] = jnp.concatenate(outs, axis=0)
            return carry
        lax.fori_loop(0, ts // SUBLANES, body, 0)

    def finish_step(x_val, pa):
        gla = _gla_gate_out(o_scr[...], gate_scr[:, 0:GV_W], gn_ref[...])
        y_scr[...] = _finish(x_val, pa, gla, gate_scr[:, GV_W:GV_W + D_MODEL], gate_scr[:, GV_W + D_MODEL:],
                             w_pb_ref[...], w_out_ref[...], alpha)

    sprev_scr[...] = s_scr[...]
    pa = gla_fast()
    h_ref[0] = _layer_norm(y_scr[...], g1_ref[...], b1_ref[...])
    finish_step(x, pa)

    @pl.when(jnp.logical_not(fast_ok))
    def _():
        x_again = load_x()
        proj_again = functools.partial(_in_proj, x_again.astype(BF16), w_in_ref)
        gq_scr[...] = proj_again(C_GQ, C_GK) * (GLA_DK ** -0.5)
        gk_scr[...] = jnp.where(live, proj_again(C_GK, C_GV), 0.0)
        gv_scr[...] = jnp.where(live, proj_again(C_GV, C_LR), 0.0)
        la_scr[...] = jnp.where(live, _log_decay(proj_again(C_LR, C_LR_END), w_a2_ref[...], b_a_ref[...]), 0.0)
        s_scr[...] = sprev_scr[...]
        gla_slow()
        finish_step(x_again, proj_att())

    @pl.when(s == n_steps - 1)
    def _():
        for h in range(GLA_HEADS):
            sfin_ref[0, h] = s_scr[h * GLA_DK:(h + 1) * GLA_DK, h * GLA_DV:(h + 1) * GLA_DV]


def _ffn_kernel(*refs, alpha, col_chunk, n_total=None):
    n_tail = 5 if n_total is None else 6
    h_refs, (w_up_ref, w_dn_ref, g_ref, b_ref, o_ref) = refs[:-n_tail], refs[-n_tail:][:5]
    y_scr = None if n_total is None else refs[-1]

    def write_norm(y):
        o_ref[...] = _layer_norm(y, g_ref[...], b_ref[...]).reshape(o_ref.shape)

    def tile():
        blocks = [r[0] if len(r.shape) == 3 else r[...] for r in h_refs]
        h = blocks[0] if len(blocks) == 1 else jnp.concatenate(blocks, axis=0)
        hb = h.astype(BF16)
        acc = jnp.zeros(h.shape, F32)
        for c in range(D_FF // col_chunk):
            u = jnp.dot(hb, w_up_ref[:, c * col_chunk:(c + 1) * col_chunk], preferred_element_type=F32)
            u = jnp.maximum(u, 0.0)
            acc = acc + jnp.dot((u * u).astype(BF16), w_dn_ref[c * col_chunk:(c + 1) * col_chunk, :],
                                preferred_element_type=F32)
            if y_scr is not None and c == 0:
                write_norm(y_scr[...])
        if y_scr is None:
            write_norm(alpha * h + acc)
        else:
            y_scr[...] = alpha * h + acc

    if n_total is None:
        tile()
        return
    t = pl.program_id(0)

    @pl.when(t == 0)
    def _():
        y_scr[...] = jnp.zeros(y_scr.shape, F32)

    pl.when(t < n_total)(tile)

    @pl.when(t == n_total)
    def _():
        write_norm(y_scr[...])


def _sample_proj_kernel(x_ref, cos_ref, slo_ref, shi_ref, w_in_ref, w_a2_ref, b_a_ref,
                        qkv_ref, gv_ref, gate_ref, kvt_ref, gcol_ref):
    proj = functools.partial(_in_proj, x_ref[...].astype(BF16), w_in_ref)

    cos, slo, shi = cos_ref[...], slo_ref[...], shi_ref[...]
    k = _rope(proj(C_K, C_V), cos, slo, shi)
    v = proj(C_V, C_GQ)
    qkv_ref[:, C_Q:C_K] = _rope(proj(C_Q, C_K), cos, slo, shi)
    qkv_ref[:, C_K:C_V] = k
    qkv_ref[:, C_V:C_GQ] = v
    gv_ref[...] = proj(C_GV, C_LR)
    gate_ref[...] = proj(C_GR, W_IN_COLS)
    def store_planes(ref, i, t):
        for p, part in enumerate(_split3_bf16(t)):
            ref[N_SPLIT * i + p] = part.astype(F32).T

    store_planes(kvt_ref, 0, k)
    store_planes(kvt_ref, 1, v)
    store_planes(gcol_ref, 0, jnp.exp(_log_decay(proj(C_LR, C_LR_END), w_a2_ref[...], b_a_ref[...])))
    store_planes(gcol_ref, 1, proj(C_GQ, C_GK) * (GLA_DK ** -0.5))
    store_planes(gcol_ref, 2, proj(C_GK, C_GV))


def _sample_mix_kernel(qkv_ref, gv_ref, kvt_ref, gcol_ref, ck_ref, cv_ref, st_ref, sink_ref, *rest, group):
    att_ref, o_ref, nk_ref, nv_ref, nst_ref = rest[-5:]
    head_row = lax.broadcasted_iota(jnp.int32, (N_Q_HEADS, ATT_W), 0)
    head_lane = lax.broadcasted_iota(jnp.int32, (N_Q_HEADS, ATT_W), 1) // HEAD_DIM
    own = head_row == head_lane
    r8 = lax.broadcasted_iota(jnp.int32, (N_Q_HEADS, KV_W), 0)
    swap = (r8 % 2) != (r8 // Q_PER_KV)
    key_i = lax.broadcasted_iota(jnp.int32, (N_Q_HEADS, BLOCK), 1)
    last_row = lax.broadcasted_iota(jnp.int32, (KV_W, BLOCK), 1) == BLOCK - 1
    sink = sink_ref[...][:, 0:1]

    qkv8 = qkv_ref[...]
    gv8 = gv_ref[...]
    q8s, scores = [], []
    for j in range(group):
        q_row = qkv8[j:j + 1, C_Q:C_K]
        qm = jnp.where(own, jnp.broadcast_to(q_row, (N_Q_HEADS, ATT_W)), 0.0)
        fold = qm[:, 0:128] + qm[:, 128:256] + qm[:, 256:384] + qm[:, 384:512]
        q8 = (jnp.where(swap, pltpu.roll(fold, HEAD_DIM, 1), fold) * (HEAD_DIM ** -0.5)).astype(BF16)
        q8s.append(q8)
        scores.append(jnp.dot(q8, ck_ref[j].reshape(KV_W, BLOCK).astype(BF16), preferred_element_type=F32))

    sel = (lax.broadcasted_iota(jnp.int32, (N_SPLIT * group, group * LANES), 0) % group
           == lax.broadcasted_iota(jnp.int32, (N_SPLIT * group, group * LANES), 1) // LANES)
    sel = jnp.where(sel, 1.0, 0.0).astype(BF16)
    spread = lambda ref, i: jnp.dot(ref[i].astype(BF16), sel, preferred_element_type=F32)
    k_cols, v_cols = spread(kvt_ref, 0), spread(kvt_ref, 1)
    a_cols, q_cols, k_gla_cols = (spread(gcol_ref, i) for i in range(3))

    probs = []
    for j in range(group):
        k_new = qkv8[j:j + 1, C_K:C_V]
        sc = jnp.where(key_i >= 1, scores[j], -jnp.inf)
        s_new = jnp.sum(q8s[j].astype(F32) * k_new.astype(BF16).astype(F32), axis=-1, keepdims=True)
        m = jnp.maximum(jnp.maximum(jnp.max(sc, axis=-1, keepdims=True), s_new), sink)
        p = jnp.exp(sc - m)
        p_new = jnp.exp(s_new - m)
        den = jnp.sum(p, axis=-1, keepdims=True) + p_new + jnp.exp(sink - m)
        probs.append((p.astype(BF16), p_new, den))

    att_rows, o_rows = [], []
    for j in range(group):
        p_bf, p_new, den = probs[j]
        v_new = qkv8[j:j + 1, C_V:C_GQ]
        o8 = (lax.dot_general(p_bf, cv_ref[j].reshape(KV_W, BLOCK).astype(BF16), (((1,), (1,)), ((), ())),
                              preferred_element_type=F32)
              + p_new.astype(BF16).astype(F32) * v_new.astype(BF16).astype(F32)) / den
        o8 = jnp.where(swap, pltpu.roll(o8, HEAD_DIM, 1), o8)
        o_wide = jnp.concatenate([o8, o8, o8, o8], axis=1)
        att_rows.append(jnp.sum(jnp.where(own, o_wide, 0.0), axis=0, keepdims=True))

    for j in range(group):
        lanes_j = slice(j * LANES, (j + 1) * LANES)
        k_old = ck_ref[j].reshape(KV_W, BLOCK)
        v_old = cv_ref[j].reshape(KV_W, BLOCK)
        nk_ref[j] = jnp.where(last_row, k_cols[:, lanes_j], pltpu.roll(k_old, BLOCK - 1, 1)).reshape(
            N_KV_HEADS, HEAD_DIM, BLOCK)
        nv_ref[j] = jnp.where(last_row, v_cols[:, lanes_j], pltpu.roll(v_old, BLOCK - 1, 1)).reshape(
            N_KV_HEADS, HEAD_DIM, BLOCK)
        a_col, q_col, k_col = a_cols[:, lanes_j], q_cols[:, lanes_j], k_gla_cols[:, lanes_j]
        v_row = gv8[j:j + 1, :]
        v_exp = jnp.concatenate([jnp.broadcast_to(v_row[:, h * GLA_DV:(h + 1) * GLA_DV], (GLA_DK, GLA_DV))
                                 for h in range(GLA_HEADS)], axis=0)
        s1 = a_col * st_ref[j].reshape(GK_W, GLA_DV) + k_col * v_exp
        nst_ref[j] = s1.reshape(GLA_HEADS, GLA_DK, GLA_DV)
        qs = q_col * s1
        o_rows.append(jnp.concatenate(
            [jnp.sum(qs[h * GLA_DK:(h + 1) * GLA_DK], axis=0, keepdims=True) for h in range(GLA_HEADS)], axis=1))
    att_ref[...] = jnp.concatenate(att_rows, axis=0)
    o_ref[...] = jnp.concatenate(o_rows, axis=0)


def _sample_finish_kernel(x_ref, att_ref, o_ref, gate_ref, gn_ref, w_pa_ref, w_pb_ref, w_out_ref, g1_ref, b1_ref,
                          h_ref, *, alpha):
    gla = _gla_gate_out(o_ref[...], _swish(gate_ref[:, 0:GV_W]), gn_ref[...])
    pa = jnp.dot(att_ref[...].astype(BF16), w_pa_ref[...], preferred_element_type=F32)
    y = _finish(x_ref[...], pa, gla, _sigmoid(gate_ref[:, GV_W:GV_W + D_MODEL]),
                _sigmoid(gate_ref[:, GV_W + D_MODEL:]), w_pb_ref[...], w_out_ref[...], alpha)
    h_ref[...] = _layer_norm(y, g1_ref[...], b1_ref[...])


def _rope_tables(pos):
    half = ROT_DIM // 2
    inv = ROPE_THETA ** (-jnp.arange(half, dtype=F32) * 2.0 / ROT_DIM)
    d = jnp.arange(LANES) % HEAD_DIM
    ang = pos.astype(F32)[:, None] * inv[d % half][None, :]
    cos, sin = jnp.cos(ang), jnp.sin(ang)
    cos_t = jnp.where(d < ROT_DIM, cos, 1.0)
    sin_lo = jnp.where(d < half, -sin, 0.0)
    sin_hi = jnp.where((d >= half) & (d < ROT_DIM), sin, 0.0)
    return cos_t, sin_lo, sin_hi


def _const_spec(shape, layer=None):
    if layer is None:
        return pl.BlockSpec(shape, lambda *_: (0,) * len(shape), pipeline_mode=pl.Buffered(1))
    return pl.BlockSpec((None,) + shape, lambda *_: (layer,) + (0,) * len(shape), pipeline_mode=pl.Buffered(1))


def _step_rows(total):
    for t in (384, 256, 128):
        if total % t == 0:
            return t
    raise ValueError("padded prompt length must be a multiple of 128")


def kernel(x_prompt, x_sample, cache_k_win, cache_v_win, state_gla, meta_tokens, w_in, w_a2, b_a, attn_sink,
           gla_norm_g, w_proj_a, w_proj_b, w_out, ln1_g, ln1_b, w_up, w_down, ln2_g, ln2_b):
    depth = w_in.shape[0]
    bsz, seq, _ = x_prompt.shape
    nsmp, dec_seq, _ = x_sample.shape
    assert dec_seq == 1 and cache_k_win.shape[2] == BLOCK and seq % BLOCK == 0
    alpha = (2 * depth) ** 0.25
    lp = seq + BLOCK
    ts = _step_rows(lp)
    n_steps = lp // ts
    rows = bsz * lp
    ffn_tile = next(t for t in (512, 384, 256, 128) if rows % t == 0)
    last_tile = next(t for t in (512, 384, 256, 128) if seq % t == 0)
    group = next(g for g in (2 * SUBLANES, SUBLANES) if nsmp % g == 0)

    w_in_r = jnp.swapaxes(w_in, 1, 2).astype(BF16)
    w_a2_p = jnp.concatenate([w_a2, jnp.zeros((depth, LANES - GLA_RANK, GK_W), w_a2.dtype)], axis=1).astype(BF16)
    w_pa, w_pb, w_o = w_proj_a.astype(BF16), w_proj_b.astype(BF16), w_out.astype(BF16)
    w_u, w_d = w_up.astype(BF16), w_down.astype(BF16)
    b_a3 = b_a.reshape(depth, 1, GK_W)
    gn3 = gla_norm_g.reshape(depth, 1, GLA_DV)
    g1, b1 = ln1_g.reshape(depth, 1, D_MODEL), ln1_b.reshape(depth, 1, D_MODEL)
    g2, b2 = ln2_g.reshape(depth, 1, D_MODEL), ln2_b.reshape(depth, 1, D_MODEL)
    sink_lanes = jnp.broadcast_to(attn_sink[:, :, None], (depth, N_Q_HEADS, LANES))

    cos_p, slo_p, shi_p = _rope_tables(jnp.arange(lp) - META_PAD)
    cos_s, slo_s, shi_s = (jnp.broadcast_to(t, (nsmp, LANES)) for t in _rope_tables(PAST_LEN + jnp.arange(1)))

    meta_block = jnp.concatenate([jnp.zeros((META_PAD, D_MODEL), x_prompt.dtype),
                                  meta_tokens.astype(x_prompt.dtype)], axis=0)
    nblk = ts // BLOCK
    xp = None
    xs = x_sample.reshape(nsmp, D_MODEL)
    ck = jnp.transpose(cache_k_win, (0, 1, 3, 4, 2))
    cv = jnp.transpose(cache_v_win, (0, 1, 3, 4, 2))

    cparams = functools.partial(pltpu.CompilerParams, vmem_limit_bytes=VMEM_LIMIT)
    pk, pv, pst = [], [], []
    stacked = []
    for l in range(depth):
        n_total = bsz * n_steps
        seq_of = lambda t: jnp.minimum(t, n_total - 1) // n_steps
        step_of = lambda t: jnp.minimum(t, n_total - 1) % n_steps
        step_spec = pl.BlockSpec((1, ts, D_MODEL), lambda t: (seq_of(t), step_of(t), 0))
        h_spec = pl.BlockSpec((1, ts, D_MODEL), lambda t: (seq_of(jnp.maximum(t - 1, 0)), step_of(jnp.maximum(t - 1, 0)), 0))
        tab_spec = pl.BlockSpec((ts, LANES), lambda t: (step_of(t), 0))
        if l == 0:
            x_specs = [_const_spec((BLOCK, D_MODEL))] + [
                pl.BlockSpec((1, BLOCK, D_MODEL),
                             lambda t, j=j: (seq_of(t), jnp.maximum(nblk * step_of(t) + j - 1, 0), 0))
                for j in range(nblk)]
            x_args = [meta_block] + [x_prompt] * nblk
        else:
            x_specs, x_args = [step_spec], [xp]
        hp, kwin, vwin, sfin = pl.pallas_call(
            functools.partial(_mixer_prompt_kernel, ts=ts, n_steps=n_steps, n_total=n_total, alpha=alpha,
                              from_tokens=(l == 0)),
            grid=(n_total + 1,),
            in_specs=x_specs + [tab_spec, tab_spec, tab_spec,
                      _const_spec((W_IN_COLS, D_MODEL), l), _const_spec((LANES, GK_W), l), _const_spec((1, GK_W), l),
                      pl.BlockSpec(memory_space=pltpu.SMEM), _const_spec((1, GLA_DV), l),
                      _const_spec((ATT_W, D_MODEL), l), _const_spec((GV_W, D_MODEL), l),
                      _const_spec((D_MODEL, D_MODEL), l), _const_spec((1, D_MODEL), l), _const_spec((1, D_MODEL), l)],
            out_specs=[h_spec,
                       pl.BlockSpec((1, BLOCK, KV_W), lambda t: (seq_of(t), 0, 0)),
                       pl.BlockSpec((1, BLOCK, KV_W), lambda t: (seq_of(t), 0, 0)),
                       pl.BlockSpec((1, GLA_HEADS, GLA_DK, GLA_DV), lambda t: (seq_of(t), 0, 0, 0))],
            out_shape=[jax.ShapeDtypeStruct((bsz, lp, D_MODEL), F32),
                       jax.ShapeDtypeStruct((bsz, BLOCK, KV_W), F32),
                       jax.ShapeDtypeStruct((bsz, BLOCK, KV_W), F32),
                       jax.ShapeDtypeStruct((bsz, GLA_HEADS, GLA_DK, GLA_DV), F32)],
            scratch_shapes=[pltpu.VMEM((N_KV_HEADS, BLOCK, KV_W), BF16), pltpu.VMEM((KV_W, BLOCK), BF16),
                            pltpu.VMEM((GK_W, GV_W), F32), pltpu.VMEM((GK_W, GV_W), F32),
                            pltpu.VMEM((ts, GK_W), F32), pltpu.VMEM((ts, GK_W), F32), pltpu.VMEM((ts, GV_W), F32),
                            pltpu.VMEM((ts, GK_W), F32), pltpu.VMEM((ts, GV_W), F32), pltpu.VMEM((ATT_W, ts), F32),
                            pltpu.VMEM((ts, W_IN_COLS - C_GR), F32), pltpu.VMEM((ts, D_MODEL), F32)],
            compiler_params=cparams(dimension_semantics=("arbitrary",)),
            name=f"mixer_prompt_{l}",
        )(*x_args, cos_p, slo_p, shi_p, w_in_r, w_a2_p, b_a3, attn_sink[l], gn3, w_pa, w_pb, w_o, g1, b1)
        pk.append(kwin.reshape(bsz, N_KV_HEADS, HEAD_DIM, BLOCK))
        pv.append(vwin.reshape(bsz, N_KV_HEADS, HEAD_DIM, BLOCK))
        pst.append(sfin)

        ffn = functools.partial(_ffn_kernel, alpha=alpha, col_chunk=1024)
        ffn_w = [_const_spec((D_MODEL, D_FF), l), _const_spec((D_FF, D_MODEL), l),
                 _const_spec((1, D_MODEL), l), _const_spec((1, D_MODEL), l)]
        if l < depth - 1:
            n_tiles = rows // ffn_tile
            xp = pl.pallas_call(
                functools.partial(ffn, n_total=n_tiles), grid=(n_tiles + 1,),
                in_specs=[pl.BlockSpec((ffn_tile, D_MODEL), lambda t: (jnp.minimum(t, n_tiles - 1), 0))] + ffn_w,
                out_specs=pl.BlockSpec((ffn_tile, D_MODEL), lambda t: (jnp.maximum(t - 1, 0), 0)),
                out_shape=jax.ShapeDtypeStruct((rows, D_MODEL), F32),
                scratch_shapes=[pltpu.VMEM((ffn_tile, D_MODEL), F32)],
                compiler_params=cparams(dimension_semantics=("arbitrary",)),
                name=f"ffn_prompt_{l}",
            )(hp.reshape(rows, D_MODEL), w_u, w_d, g2, b2).reshape(bsz, lp, D_MODEL)
        else:
            pieces = last_tile // BLOCK
            per_seq = seq // last_tile
            n_tiles = bsz * per_seq
            rd = lambda t: jnp.minimum(t, n_tiles - 1)
            wr = lambda t: jnp.maximum(t - 1, 0)
            y_prompt = pl.pallas_call(
                functools.partial(ffn, n_total=n_tiles), grid=(n_tiles + 1,),
                in_specs=[pl.BlockSpec((1, BLOCK, D_MODEL),
                                       lambda t, j=j: (rd(t) // per_seq, 1 + pieces * (rd(t) % per_seq) + j, 0))
                          for j in range(pieces)] + ffn_w,
                out_specs=pl.BlockSpec((1, last_tile, D_MODEL), lambda t: (wr(t) // per_seq, wr(t) % per_seq, 0)),
                out_shape=jax.ShapeDtypeStruct((bsz, seq, D_MODEL), F32),
                scratch_shapes=[pltpu.VMEM((last_tile, D_MODEL), F32)],
                compiler_params=cparams(dimension_semantics=("arbitrary",)),
                name=f"ffn_prompt_{l}",
            )(*([hp] * pieces), w_u, w_d, g2, b2)

        proj_out = [(nsmp, C_GQ), (nsmp, GV_W), (nsmp, W_IN_COLS - C_GR),
                    (2 * N_SPLIT, KV_W, nsmp), (3 * N_SPLIT, GK_W, nsmp)]
        qkv, gv_s, gate, kvt, gcol = pl.pallas_call(
            _sample_proj_kernel, grid=(1,),
            in_specs=[_const_spec((nsmp, D_MODEL)), _const_spec((nsmp, LANES)), _const_spec((nsmp, LANES)),
                      _const_spec((nsmp, LANES)), _const_spec((W_IN_COLS, D_MODEL), l),
                      _const_spec((LANES, GK_W), l), _const_spec((1, GK_W), l)],
            out_specs=[_const_spec(s) for s in proj_out],
            out_shape=[jax.ShapeDtypeStruct(s, F32) for s in proj_out],
            compiler_params=cparams(dimension_semantics=("arbitrary",)),
            name=f"sample_proj_{l}",
        )(xs, cos_s, slo_s, shi_s, w_in_r, w_a2_p, b_a3)
        def by_group(t):
            t = t.reshape(t.shape[0] // N_SPLIT, N_SPLIT, t.shape[1], nsmp // group, group)
            return jnp.transpose(t, (3, 0, 2, 1, 4)).reshape(nsmp // group, t.shape[0], t.shape[2], N_SPLIT * group)
        col_spec = lambda n, width: pl.BlockSpec((None, n, width, N_SPLIT * group), lambda i: (i, 0, 0, 0))

        grp = lambda width: pl.BlockSpec((group, width), lambda i: (i, 0))
        cache_spec = pl.BlockSpec((None, group, N_KV_HEADS, HEAD_DIM, BLOCK), lambda i: (l, i, 0, 0, 0))
        state_spec = pl.BlockSpec((None, group, GLA_HEADS, GLA_DK, GLA_DV), lambda i: (l, i, 0, 0, 0))
        n_mix_in = 8
        att_s, o_s, *stacked = pl.pallas_call(
            functools.partial(_sample_mix_kernel, group=group), grid=(nsmp // group,),
            in_specs=[grp(C_GQ), grp(GV_W), col_spec(2, KV_W), col_spec(3, GK_W), cache_spec, cache_spec, state_spec,
                      _const_spec((N_Q_HEADS, LANES), l)] + [pl.BlockSpec(memory_space=pl.ANY)] * len(stacked),
            out_specs=[grp(ATT_W), grp(GV_W), cache_spec, cache_spec, state_spec],
            out_shape=[jax.ShapeDtypeStruct((nsmp, ATT_W), F32), jax.ShapeDtypeStruct((nsmp, GV_W), F32),
                       jax.ShapeDtypeStruct(ck.shape, F32), jax.ShapeDtypeStruct(cv.shape, F32),
                       jax.ShapeDtypeStruct(state_gla.shape, F32)],
            input_output_aliases={n_mix_in + i: 2 + i for i in range(len(stacked))},
            compiler_params=cparams(dimension_semantics=("arbitrary",)),
            name=f"sample_mix_{l}",
        )(qkv, gv_s, by_group(kvt), by_group(gcol), ck, cv, state_gla, sink_lanes, *stacked)

        hs = pl.pallas_call(
            functools.partial(_sample_finish_kernel, alpha=alpha), grid=(1,),
            in_specs=[_const_spec((nsmp, D_MODEL)), _const_spec((nsmp, ATT_W)), _const_spec((nsmp, GV_W)),
                      _const_spec((nsmp, W_IN_COLS - C_GR)), _const_spec((1, GLA_DV), l),
                      _const_spec((ATT_W, D_MODEL), l), _const_spec((GV_W, D_MODEL), l),
                      _const_spec((D_MODEL, D_MODEL), l), _const_spec((1, D_MODEL), l), _const_spec((1, D_MODEL), l)],
            out_specs=_const_spec((nsmp, D_MODEL)),
            out_shape=jax.ShapeDtypeStruct((nsmp, D_MODEL), F32),
            compiler_params=cparams(dimension_semantics=("arbitrary",)),
            name=f"sample_finish_{l}",
        )(xs, att_s, o_s, gate, gn3, w_pa, w_pb, w_o, g1, b1)
        xs = pl.pallas_call(
            ffn, grid=(1,),
            in_specs=[_const_spec((nsmp, D_MODEL))] + ffn_w,
            out_specs=_const_spec((nsmp, D_MODEL)),
            out_shape=jax.ShapeDtypeStruct((nsmp, D_MODEL), F32),
            compiler_params=cparams(dimension_semantics=("arbitrary",)),
            name=f"ffn_sample_{l}",
        )(hs, w_u, w_d, g2, b2)

    y_sample = xs.reshape(nsmp, 1, D_MODEL)
    to_rows = lambda t: jnp.transpose(t, (0, 1, 4, 2, 3))
    return (y_prompt, y_sample, to_rows(jnp.stack(pk)), to_rows(jnp.stack(pv)), jnp.stack(pst),
            to_rows(stacked[0]), to_rows(stacked[1]), stacked[2])
```

```python
import functools

import jax
import jax.numpy as jnp
from jax import lax
from jax.experimental import pallas as pl
from jax.experimental.pallas import tpu as pltpu

F32 = jnp.float32
BF16 = jnp.bfloat16

D_MODEL = 1024
PAST_LEN = 8192
N_META = 16
BLOCK = 128
META_PAD = BLOCK - N_META
HEAD_DIM = 64
N_Q_HEADS = 8
N_KV_HEADS = 2
Q_PER_KV = N_Q_HEADS // N_KV_HEADS
ROT_DIM = HEAD_DIM // 4
ROPE_THETA = 500000.0
GLA_HEADS = 4
GLA_DK = 64
GLA_DV = 128
GLA_RANK = 16
GLA_TAU = 16.0
D_FF = 4 * D_MODEL
ATT_W = N_Q_HEADS * HEAD_DIM
KV_W = N_KV_HEADS * HEAD_DIM
GK_W = GLA_HEADS * GLA_DK
GV_W = GLA_HEADS * GLA_DV
LOG2_E = 1.4426950408889634
LN_EPS = 1e-5
RMS_EPS = 1e-6
LANES = 128
SUBLANES = 8
VMEM_LIMIT = 56 * 1024 * 1024

C_Q = 0
C_K = C_Q + ATT_W
C_V = C_K + KV_W
C_GQ = C_V + KV_W
C_GK = C_GQ + GK_W
C_GV = C_GK + GK_W
C_LR = C_GV + GV_W
C_GR = C_LR + GLA_RANK
C_GA = C_GR + GV_W
C_GB = C_GA + D_MODEL
W_IN_COLS = C_GB + D_MODEL
C_LR_END = C_LR + LANES

GLA_FAST_MAX_DECAY = 40.0


def _sigmoid(x):
    return 1.0 / (1.0 + jnp.exp(-x))


def _layer_norm(y, g, b):
    mu = jnp.mean(y, axis=-1, keepdims=True)
    yc = y - mu
    var = jnp.mean(yc * yc, axis=-1, keepdims=True)
    return yc * lax.rsqrt(var + LN_EPS) * g + b


def _rope(t, cos, sin_lo, sin_hi):
    outs = []
    for j in range(t.shape[1] // LANES):
        tj = t[:, j * LANES:(j + 1) * LANES]
        outs.append(tj * cos + pltpu.roll(tj, LANES - ROT_DIM // 2, 1) * sin_lo
                    + pltpu.roll(tj, ROT_DIM // 2, 1) * sin_hi)
    return outs[0] if len(outs) == 1 else jnp.concatenate(outs, axis=1)


def _in_proj(xb, w_t_ref, lo, hi):
    return lax.dot_general(xb, w_t_ref[lo:hi, :], (((1,), (1,)), ((), ())), preferred_element_type=F32)


def _log_decay(glr, w_a2, b_a):
    z = jnp.dot(glr.astype(BF16), w_a2, preferred_element_type=F32) + b_a
    return (jnp.minimum(z, 0.0) - jnp.log1p(jnp.exp(-jnp.abs(z)))) * (1.0 / GLA_TAU)


def _row_to_col(row):
    n = row.shape[1]
    eye = lax.broadcasted_iota(jnp.int32, (n, n), 0) == lax.broadcasted_iota(jnp.int32, (n, n), 1)
    return jnp.sum(jnp.where(eye, jnp.broadcast_to(row, (n, n)), 0.0), axis=1, keepdims=True)


N_SPLIT = 3


def _split3_bf16(a):
    hi = a.astype(BF16)
    r = a - hi.astype(F32)
    mid = r.astype(BF16)
    lo = (r - mid.astype(F32)).astype(BF16)
    return hi, mid, lo


def _swish(x):
    return x * _sigmoid(x)


def _gla_gate_out(o, swish_gr, gn):
    outs = []
    for h in range(GLA_HEADS):
        oh = o[:, h * GLA_DV:(h + 1) * GLA_DV]
        ms = jnp.mean(oh * oh, axis=-1, keepdims=True)
        outs.append(oh * lax.rsqrt(ms + RMS_EPS) * gn)
    return jnp.concatenate(outs, axis=1) * swish_gr


def _finish(x, pa, gla, sig_a, sig_b, w_pb, w_out, alpha):
    pb = jnp.dot(gla.astype(BF16), w_pb, preferred_element_type=F32)
    m = sig_a * pa + sig_b * pb
    return alpha * x + jnp.dot(m.astype(BF16), w_out, preferred_element_type=F32)


_MIXER_REF_NAMES = ("cos", "slo", "shi", "w_in", "w_a2", "b_a", "sink", "gn", "w_pa", "w_pb", "w_out", "g1", "b1",
                    "h", "kwin", "vwin", "sfin",
                    "kprev_scr", "vprev_scr", "s_scr", "sprev_scr", "gq_scr", "gk_scr", "gv_scr", "la_scr",
                    "o_scr", "att_t_scr", "gate_scr", "y_scr")


def _mixer_prompt_kernel(*refs, ts, n_steps, n_total, alpha, from_tokens):
    t = pl.program_id(0)
    named = dict(zip(_MIXER_REF_NAMES, refs[len(refs) - len(_MIXER_REF_NAMES):]))
    h_ref, y_scr, g1_ref, b1_ref = named["h"], named["y_scr"], named["g1"], named["b1"]

    @pl.when(t == 0)
    def _():
        y_scr[...] = jnp.zeros(y_scr.shape, F32)

    @pl.when(t < n_total)
    def _():
        _mixer_step(lax.rem(t, n_steps), *refs, ts=ts, n_steps=n_steps, alpha=alpha, from_tokens=from_tokens)

    @pl.when(t == n_total)
    def _():
        h_ref[0] = _layer_norm(y_scr[...], g1_ref[...], b1_ref[...])


def _mixer_step(s, *refs, ts, n_steps, alpha, from_tokens):
    nblk = ts // BLOCK
    n_x = 1 + nblk if from_tokens else 1
    x_refs = refs[:n_x]
    (cos_ref, slo_ref, shi_ref, w_in_ref, w_a2_ref, b_a_ref, sink_ref, gn_ref,
     w_pa_ref, w_pb_ref, w_out_ref, g1_ref, b1_ref,
     h_ref, kwin_ref, vwin_ref, sfin_ref,
     kprev_scr, vprev_scr, s_scr, sprev_scr, gq_scr, gk_scr, gv_scr, la_scr, o_scr, att_t_scr, gate_scr,
     y_scr) = refs[n_x:]
    assert len(refs) - n_x == len(_MIXER_REF_NAMES)

    def load_x():
        if not from_tokens:
            return x_refs[0][0]
        blocks = [r[0] for r in x_refs[1:]]
        blocks[0] = jnp.where(s == 0, x_refs[0][...], blocks[0])
        return jnp.concatenate(blocks, axis=0)

    @pl.when(s == 0)
    def _():
        kprev_scr[...] = jnp.zeros((N_KV_HEADS, BLOCK, KV_W), BF16)
        vprev_scr[...] = jnp.zeros((KV_W, BLOCK), BF16)
        s_scr[...] = jnp.zeros(s_scr.shape, F32)

    x = load_x()
    xb = x.astype(BF16)
    proj = functools.partial(_in_proj, xb, w_in_ref)

    mix_in = proj(C_Q, C_LR_END)
    live = (s * ts + lax.broadcasted_iota(jnp.int32, (ts, 1), 0)) >= META_PAD
    cos, slo, shi = cos_ref[...], slo_ref[...], shi_ref[...]
    q = _rope(mix_in[:, C_Q:C_K], cos, slo, shi)
    k = _rope(mix_in[:, C_K:C_V], cos, slo, shi)
    v = mix_in[:, C_V:C_GQ]
    gq = mix_in[:, C_GQ:C_GK] * (GLA_DK ** -0.5)
    gk = jnp.where(live, mix_in[:, C_GK:C_GV], 0.0)
    gv = jnp.where(live, mix_in[:, C_GV:C_LR], 0.0)
    v_t = v.T

    @pl.when(s == n_steps - 1)
    def _():
        kwin_ref[0] = k[ts - BLOCK:, :].T
        vwin_ref[0] = v_t[:, ts - BLOCK:]

    gate_w = W_IN_COLS - C_GR
    piece_cols = 4 * LANES
    n_gate_pieces = min(N_KV_HEADS * nblk, -(-gate_w // piece_cols))
    gate_edges = [min(gate_w, piece_cols * (-(-gate_w // piece_cols) * i // n_gate_pieces))
                  for i in range(n_gate_pieces)] + [gate_w]

    def gate_piece_matmul(i):
        return proj(C_GR + gate_edges[i], C_GR + gate_edges[i + 1])

    def gate_piece_store(i, val):
        lo, hi = gate_edges[i], gate_edges[i + 1]
        mid = min(max(GV_W, lo), hi)
        if mid > lo:
            gate_scr[:, lo:mid] = _swish(val[:, :mid - lo])
        if hi > mid:
            gate_scr[:, mid:hi] = _sigmoid(val[:, mid - lo:])

    tri = (lax.broadcasted_iota(jnp.int32, (BLOCK, BLOCK), 1)
           <= lax.broadcasted_iota(jnp.int32, (BLOCK, BLOCK), 0))
    tri_bf = jnp.where(tri, 1.0, 0.0).astype(BF16)
    head_of_k = lax.broadcasted_iota(jnp.int32, (BLOCK, GK_W), 1) // GLA_DK
    head_of_v = lax.broadcasted_iota(jnp.int32, (BLOCK, GV_W), 1) // GLA_DV
    state_diag = (lax.broadcasted_iota(jnp.int32, (GK_W, GV_W), 0) // GLA_DK
                  == lax.broadcasted_iota(jnp.int32, (GK_W, GV_W), 1) // GLA_DV)
    causal = (lax.broadcasted_iota(jnp.int32, (BLOCK, GLA_HEADS * BLOCK), 1) % BLOCK
              <= lax.broadcasted_iota(jnp.int32, (BLOCK, GLA_HEADS * BLOCK), 0))
    gla = {"cums": [], "chunks": []}

    def issue_decay():
        gla["la"] = jnp.where(live, _log_decay(mix_in[:, C_LR:C_LR_END], w_a2_ref[...], b_a_ref[...]), 0.0)

    def issue_cumsum():
        worst = jnp.zeros((1, GK_W), F32)
        for c in range(nblk):
            parts = jnp.concatenate(_split3_bf16(gla["la"][c * BLOCK:(c + 1) * BLOCK]), axis=1)
            b3 = jnp.dot(tri_bf, parts, preferred_element_type=F32)
            b = b3[:, 0:GK_W] + b3[:, GK_W:2 * GK_W] + b3[:, 2 * GK_W:3 * GK_W]
            gla["cums"].append(b)
            worst = jnp.maximum(worst, -b[BLOCK - 1:BLOCK, :])
        gla["worst"] = worst

    def issue_chunk(c):
        rows = slice(c * BLOCK, (c + 1) * BLOCK)
        b = gla["cums"][c]
        b_last = b[BLOCK - 1:BLOCK, :]
        kc = gk[rows]
        q_dec = (gq[rows] * jnp.exp(b)).astype(BF16)
        k_inv = (kc * jnp.exp(-b)).astype(BF16)
        k_end_t = (kc * jnp.exp(b_last - b)).T.astype(BF16)
        vc = gv[rows].astype(BF16)
        zk = jnp.zeros_like(k_inv)
        k_bd = jnp.concatenate([jnp.where(head_of_k == h, k_inv, zk) for h in range(GLA_HEADS)], axis=0)
        a = lax.dot_general(q_dec, k_bd, (((1,), (1,)), ((), ())), preferred_element_type=F32)
        a = jnp.where(causal, a, 0.0).astype(BF16)
        zv = jnp.zeros_like(vc)
        v_bd = jnp.concatenate([jnp.where(head_of_v == h, vc, zv) for h in range(GLA_HEADS)], axis=0)
        o_intra = jnp.dot(a, v_bd, preferred_element_type=F32)
        ds = jnp.where(state_diag, jnp.dot(k_end_t, vc, preferred_element_type=F32), 0.0)
        gla["chunks"].append((rows, q_dec, o_intra, ds, _row_to_col(jnp.exp(b_last))))

    n_pairs = N_KV_HEADS * nblk
    second = min(1, n_pairs - 1)
    side_work = {}
    for pair, issue in ([(0, issue_decay), (second, issue_cumsum)]
                        + [(max(second, n_pairs - nblk + c), functools.partial(issue_chunk, c)) for c in range(nblk)]):
        side_work.setdefault(pair, []).append(issue)

    q_bf = (q * (HEAD_DIM ** -0.5 * LOG2_E)).astype(BF16)
    low_half = lax.broadcasted_iota(jnp.int32, (1, KV_W), 1) < HEAD_DIM
    k_swapped = pltpu.roll(k, HEAD_DIM, 1)
    k_dup = [jnp.where(low_half, k, k_swapped).astype(BF16), jnp.where(low_half, k_swapped, k).astype(BF16)]
    k_keys = [jnp.concatenate([kprev_scr[i], k_dup[i]], axis=0) for i in range(N_KV_HEADS)]
    vt_bf = v_t.astype(BF16)
    vt_keys = jnp.concatenate([vprev_scr[...], vt_bf], axis=1)
    kj = lax.broadcasted_iota(jnp.int32, (2 * BLOCK, Q_PER_KV * BLOCK), 0)
    qi = lax.broadcasted_iota(jnp.int32, (2 * BLOCK, Q_PER_KV * BLOCK), 1) % BLOCK
    band = (kj - qi >= 1) & (kj - qi <= BLOCK)
    q_low_half = lax.broadcasted_iota(jnp.int32, (BLOCK, LANES), 1) < HEAD_DIM
    pieces_done = 0
    for blk in range(nblk):
        first_key_slot = (s * nblk + blk - 1) * BLOCK
        valid = band & (kj + first_key_slot >= META_PAD)
        r0 = blk * BLOCK
        for kv in range(N_KV_HEADS):
            heads = [kv * Q_PER_KV + g for g in range(Q_PER_KV)]
            q_rows = []
            for hq in heads:
                grp = q_bf[r0:r0 + BLOCK, (hq // 2) * LANES:(hq // 2 + 1) * LANES]
                own = q_low_half if hq % 2 == 0 else jnp.logical_not(q_low_half)
                q_rows.append(jnp.where(own, grp, jnp.zeros_like(grp)))
            st = lax.dot_general(k_keys[kv][r0:r0 + 2 * BLOCK, :], jnp.concatenate(q_rows, axis=0),
                                 (((1,), (1,)), ((), ())), preferred_element_type=F32)
            gate_val = gate_piece_matmul(pieces_done) if pieces_done < n_gate_pieces else None
            for issue in side_work.get(blk * N_KV_HEADS + kv, ()):
                issue()
            st = jnp.where(valid, st, -jnp.inf)
            sink_row = jnp.concatenate([jnp.full((1, BLOCK), sink_ref[hq] * LOG2_E, F32) for hq in heads], axis=1)
            m = jnp.maximum(jnp.max(st, axis=0, keepdims=True), sink_row)
            p = jnp.exp2(st - m)
            den = jnp.sum(p, axis=0, keepdims=True) + jnp.exp2(sink_row - m)
            ot = jnp.dot(vt_keys[kv * HEAD_DIM:(kv + 1) * HEAD_DIM, r0:r0 + 2 * BLOCK], p.astype(BF16),
                         preferred_element_type=F32) * (1.0 / den)
            for g, hq in enumerate(heads):
                att_t_scr[hq * HEAD_DIM:(hq + 1) * HEAD_DIM, r0:r0 + BLOCK] = ot[:, g * BLOCK:(g + 1) * BLOCK]
            if gate_val is not None:
                gate_piece_store(pieces_done, gate_val)
                pieces_done += 1
    for i in range(pieces_done, n_gate_pieces):
        gate_piece_store(i, gate_piece_matmul(i))
    for i in range(N_KV_HEADS):
        kprev_scr[i] = k_dup[i][ts - BLOCK:, :]
    vprev_scr[...] = vt_bf[:, ts - BLOCK:]

    fast_ok = jnp.max(gla["worst"]) <= GLA_FAST_MAX_DECAY

    def proj_att():
        return jnp.dot(att_t_scr[...].T.astype(BF16), w_pa_ref[...], preferred_element_type=F32)

    def gla_fast():
        pa = proj_att()
        for rows, q_dec, o_intra, ds, decay_col in gla["chunks"]:
            s0 = s_scr[...]
            o_scr[rows, :] = o_intra + jnp.dot(q_dec, s0.astype(BF16), preferred_element_type=F32)
            s_scr[...] = decay_col * s0 + ds
        return pa

    def gla_slow():
        def body(i, carry):
            rows = pl.ds(pl.multiple_of(i * SUBLANES, SUBLANES), SUBLANES)
            la8, k8, q8, v8 = la_scr[rows, :], gk_scr[rows, :], gq_scr[rows, :], gv_scr[rows, :]
            outs = []
            for r in range(SUBLANES):
                a_col = _row_to_col(jnp.exp(la8[r:r + 1]))
                k_col = _row_to_col(k8[r:r + 1])
                q_col = _row_to_col(q8[r:r + 1])
                s1 = a_col * s_scr[...] + jnp.where(state_diag, k_col * v8[r:r + 1], 0.0)
                s_scr[...] = s1
                outs.append(jnp.sum(q_col * s1, axis=0, keepdims=True))
            o_scr[rows, :] = jnp.concatenate(outs, axis=0)
            return carry
        lax.fori_loop(0, ts // SUBLANES, body, 0)

    def finish_step(x_val, pa):
        gla = _gla_gate_out(o_scr[...], gate_scr[:, 0:GV_W], gn_ref[...])
        y_scr[...] = _finish(x_val, pa, gla, gate_scr[:, GV_W:GV_W + D_MODEL], gate_scr[:, GV_W + D_MODEL:],
                             w_pb_ref[...], w_out_ref[...], alpha)

    sprev_scr[...] = s_scr[...]
    pa = gla_fast()
    h_ref[0] = _layer_norm(y_scr[...], g1_ref[...], b1_ref[...])
    finish_step(x, pa)

    @pl.when(jnp.logical_not(fast_ok))
    def _():
        x_again = load_x()
        proj_again = functools.partial(_in_proj, x_again.astype(BF16), w_in_ref)
        gq_scr[...] = proj_again(C_GQ, C_GK) * (GLA_DK ** -0.5)
        gk_scr[...] = jnp.where(live, proj_again(C_GK, C_GV), 0.0)
        gv_scr[...] = jnp.where(live, proj_again(C_GV, C_LR), 0.0)
        la_scr[...] = jnp.where(live, _log_decay(proj_again(C_LR, C_LR_END), w_a2_ref[...], b_a_ref[...]), 0.0)
        s_scr[...] = sprev_scr[...]
        gla_slow()
        finish_step(x_again, proj_att())

    @pl.when(s == n_steps - 1)
    def _():
        for h in range(GLA_HEADS):
            sfin_ref[0, h] = s_scr[h * GLA_DK:(h + 1) * GLA_DK, h * GLA_DV:(h + 1) * GLA_DV]


def _ffn_kernel(*refs, alpha, col_chunk, n_total=None):
    n_tail = 5 if n_total is None else 6
    h_refs, (w_up_ref, w_dn_ref, g_ref, b_ref, o_ref) = refs[:-n_tail], refs[-n_tail:][:5]
    y_scr = None if n_total is None else refs[-1]

    def write_norm(y):
        o_ref[...] = _layer_norm(y, g_ref[...], b_ref[...]).reshape(o_ref.shape)

    def tile():
        blocks = [r[0] if len(r.shape) == 3 else r[...] for r in h_refs]
        h = blocks[0] if len(blocks) == 1 else jnp.concatenate(blocks, axis=0)
        hb = h.astype(BF16)
        acc = jnp.zeros(h.shape, F32)
        for c in range(D_FF // col_chunk):
            u = jnp.dot(hb, w_up_ref[:, c * col_chunk:(c + 1) * col_chunk], preferred_element_type=F32)
            u = jnp.maximum(u, 0.0)
            acc = acc + jnp.dot((u * u).astype(BF16), w_dn_ref[c * col_chunk:(c + 1) * col_chunk, :],
                                preferred_element_type=F32)
            if y_scr is not None and c == 0:
                write_norm(y_scr[...])
        if y_scr is None:
            write_norm(alpha * h + acc)
        else:
            y_scr[...] = alpha * h + acc

    if n_total is None:
        tile()
        return
    t = pl.program_id(0)

    @pl.when(t == 0)
    def _():
        y_scr[...] = jnp.zeros(y_scr.shape, F32)

    pl.when(t < n_total)(tile)

    @pl.when(t == n_total)
    def _():
        write_norm(y_scr[...])


def _sample_proj_kernel(x_ref, cos_ref, slo_ref, shi_ref, w_in_ref, w_a2_ref, b_a_ref,
                        qkv_ref, gv_ref, gate_ref, kvt_ref, gcol_ref):
    proj = functools.partial(_in_proj, x_ref[...].astype(BF16), w_in_ref)

    cos, slo, shi = cos_ref[...], slo_ref[...], shi_ref[...]
    k = _rope(proj(C_K, C_V), cos, slo, shi)
    v = proj(C_V, C_GQ)
    qkv_ref[:, C_Q:C_K] = _rope(proj(C_Q, C_K), cos, slo, shi)
    qkv_ref[:, C_K:C_V] = k
    qkv_ref[:, C_V:C_GQ] = v
    gv_ref[...] = proj(C_GV, C_LR)
    gate_ref[...] = proj(C_GR, W_IN_COLS)
    def store_planes(ref, i, t):
        for p, part in enumerate(_split3_bf16(t)):
            ref[N_SPLIT * i + p] = part.astype(F32).T

    store_planes(kvt_ref, 0, k)
    store_planes(kvt_ref, 1, v)
    store_planes(gcol_ref, 0, jnp.exp(_log_decay(proj(C_LR, C_LR_END), w_a2_ref[...], b_a_ref[...])))
    store_planes(gcol_ref, 1, proj(C_GQ, C_GK) * (GLA_DK ** -0.5))
    store_planes(gcol_ref, 2, proj(C_GK, C_GV))


def _sample_mix_kernel(qkv_ref, gv_ref, kvt_ref, gcol_ref, ck_ref, cv_ref, st_ref, sink_ref, *rest, group):
    att_ref, o_ref, nk_ref, nv_ref, nst_ref = rest[-5:]
    head_row = lax.broadcasted_iota(jnp.int32, (N_Q_HEADS, ATT_W), 0)
    head_lane = lax.broadcasted_iota(jnp.int32, (N_Q_HEADS, ATT_W), 1) // HEAD_DIM
    own = head_row == head_lane
    r8 = lax.broadcasted_iota(jnp.int32, (N_Q_HEADS, KV_W), 0)
    swap = (r8 % 2) != (r8 // Q_PER_KV)
    key_i = lax.broadcasted_iota(jnp.int32, (N_Q_HEADS, BLOCK), 1)
    last_row = lax.broadcasted_iota(jnp.int32, (KV_W, BLOCK), 1) == BLOCK - 1
    sink = sink_ref[...][:, 0:1]

    qkv8 = qkv_ref[...]
    gv8 = gv_ref[...]
    q8s, scores = [], []
    for j in range(group):
        q_row = qkv8[j:j + 1, C_Q:C_K]
        qm = jnp.where(own, jnp.broadcast_to(q_row, (N_Q_HEADS, ATT_W)), 0.0)
        fold = qm[:, 0:128] + qm[:, 128:256] + qm[:, 256:384] + qm[:, 384:512]
        q8 = (jnp.where(swap, pltpu.roll(fold, HEAD_DIM, 1), fold) * (HEAD_DIM ** -0.5)).astype(BF16)
        q8s.append(q8)
        scores.append(jnp.dot(q8, ck_ref[j].reshape(KV_W, BLOCK).astype(BF16), preferred_element_type=F32))

    sel = (lax.broadcasted_iota(jnp.int32, (N_SPLIT * group, group * LANES), 0) % group
           == lax.broadcasted_iota(jnp.int32, (N_SPLIT * group, group * LANES), 1) // LANES)
    sel = jnp.where(sel, 1.0, 0.0).astype(BF16)
    spread = lambda ref, i: jnp.dot(ref[i].astype(BF16), sel, preferred_element_type=F32)
    k_cols, v_cols = spread(kvt_ref, 0), spread(kvt_ref, 1)
    a_cols, q_cols, k_gla_cols = (spread(gcol_ref, i) for i in range(3))

    probs = []
    for j in range(group):
        k_new = qkv8[j:j + 1, C_K:C_V]
        sc = jnp.where(key_i >= 1, scores[j], -jnp.inf)
        s_new = jnp.sum(q8s[j].astype(F32) * k_new.astype(BF16).astype(F32), axis=-1, keepdims=True)
        m = jnp.maximum(jnp.maximum(jnp.max(sc, axis=-1, keepdims=True), s_new), sink)
        p = jnp.exp(sc - m)
        p_new = jnp.exp(s_new - m)
        den = jnp.sum(p, axis=-1, keepdims=True) + p_new + jnp.exp(sink - m)
        probs.append((p.astype(BF16), p_new, den))

    att_rows, o_rows = [], []
    for j in range(group):
        p_bf, p_new, den = probs[j]
        v_new = qkv8[j:j + 1, C_V:C_GQ]
        o8 = (lax.dot_general(p_bf, cv_ref[j].reshape(KV_W, BLOCK).astype(BF16), (((1,), (1,)), ((), ())),
                              preferred_element_type=F32)
              + p_new.astype(BF16).astype(F32) * v_new.astype(BF16).astype(F32)) / den
        o8 = jnp.where(swap, pltpu.roll(o8, HEAD_DIM, 1), o8)
        o_wide = jnp.concatenate([o8, o8, o8, o8], axis=1)
        att_rows.append(jnp.sum(jnp.where(own, o_wide, 0.0), axis=0, keepdims=True))

    for j in range(group):
        lanes_j = slice(j * LANES, (j + 1) * LANES)
        k_old = ck_ref[j].reshape(KV_W, BLOCK)
        v_old = cv_ref[j].reshape(KV_W, BLOCK)
        nk_ref[j] = jnp.where(last_row, k_cols[:, lanes_j], pltpu.roll(k_old, BLOCK - 1, 1)).reshape(
            N_KV_HEADS, HEAD_DIM, BLOCK)
        nv_ref[j] = jnp.where(last_row, v_cols[:, lanes_j], pltpu.roll(v_old, BLOCK - 1, 1)).reshape(
            N_KV_HEADS, HEAD_DIM, BLOCK)
        a_col, q_col, k_col = a_cols[:, lanes_j], q_cols[:, lanes_j], k_gla_cols[:, lanes_j]
        v_row = gv8[j:j + 1, :]
        v_exp = jnp.concatenate([jnp.broadcast_to(v_row[:, h * GLA_DV:(h + 1) * GLA_DV], (GLA_DK, GLA_DV))
                                 for h in range(GLA_HEADS)], axis=0)
        s1 = a_col * st_ref[j].reshape(GK_W, GLA_DV) + k_col * v_exp
        nst_ref[j] = s1.reshape(GLA_HEADS, GLA_DK, GLA_DV)
        qs = q_col * s1
        o_rows.append(jnp.concatenate(
            [jnp.sum(qs[h * GLA_DK:(h + 1) * GLA_DK], axis=0, keepdims=True) for h in range(GLA_HEADS)], axis=1))
    att_ref[...] = jnp.concatenate(att_rows, axis=0)
    o_ref[...] = jnp.concatenate(o_rows, axis=0)


def _sample_finish_kernel(x_ref, att_ref, o_ref, gate_ref, gn_ref, w_pa_ref, w_pb_ref, w_out_ref, g1_ref, b1_ref,
                          h_ref, *, alpha):
    gla = _gla_gate_out(o_ref[...], _swish(gate_ref[:, 0:GV_W]), gn_ref[...])
    pa = jnp.dot(att_ref[...].astype(BF16), w_pa_ref[...], preferred_element_type=F32)
    y = _finish(x_ref[...], pa, gla, _sigmoid(gate_ref[:, GV_W:GV_W + D_MODEL]),
                _sigmoid(gate_ref[:, GV_W + D_MODEL:]), w_pb_ref[...], w_out_ref[...], alpha)
    h_ref[...] = _layer_norm(y, g1_ref[...], b1_ref[...])


def _rope_tables(pos):
    half = ROT_DIM // 2
    inv = ROPE_THETA ** (-jnp.arange(half, dtype=F32) * 2.0 / ROT_DIM)
    d = jnp.arange(LANES) % HEAD_DIM
    ang = pos.astype(F32)[:, None] * inv[d % half][None, :]
    cos, sin = jnp.cos(ang), jnp.sin(ang)
    cos_t = jnp.where(d < ROT_DIM, cos, 1.0)
    sin_lo = jnp.where(d < half, -sin, 0.0)
    sin_hi = jnp.where((d >= half) & (d < ROT_DIM), sin, 0.0)
    return cos_t, sin_lo, sin_hi


def _const_spec(shape, layer=None):
    if layer is None:
        return pl.BlockSpec(shape, lambda *_: (0,) * len(shape), pipeline_mode=pl.Buffered(1))
    return pl.BlockSpec((None,) + shape, lambda *_: (layer,) + (0,) * len(shape), pipeline_mode=pl.Buffered(1))


MIXER_STEP_ROWS = (3 * BLOCK, 2 * BLOCK, BLOCK)
FFN_TILE_ROWS = (6 * BLOCK, 4 * BLOCK, 3 * BLOCK, 2 * BLOCK, BLOCK)
FFN_COL_CHUNK = D_MODEL


def _first_divisor(total, candidates):
    for t in candidates:
        if total % t == 0:
            return t
    raise ValueError(f"{total} rows are not a multiple of {candidates[-1]}")


def kernel(x_prompt, x_sample, cache_k_win, cache_v_win, state_gla, meta_tokens, w_in, w_a2, b_a, attn_sink,
           gla_norm_g, w_proj_a, w_proj_b, w_out, ln1_g, ln1_b, w_up, w_down, ln2_g, ln2_b):
    depth = w_in.shape[0]
    bsz, seq, _ = x_prompt.shape
    nsmp, dec_seq, _ = x_sample.shape
    assert dec_seq == 1 and cache_k_win.shape[2] == BLOCK and seq % BLOCK == 0
    alpha = (2 * depth) ** 0.25
    lp = seq + BLOCK
    ts = _first_divisor(lp, MIXER_STEP_ROWS)
    n_steps = lp // ts
    rows = bsz * lp
    ffn_tile = _first_divisor(rows, FFN_TILE_ROWS)
    last_tile = _first_divisor(seq, FFN_TILE_ROWS)
    group = next(g for g in (2 * SUBLANES, SUBLANES) if nsmp % g == 0)

    w_in_r = jnp.swapaxes(w_in, 1, 2).astype(BF16)
    w_a2_p = jnp.concatenate([w_a2, jnp.zeros((depth, LANES - GLA_RANK, GK_W), w_a2.dtype)], axis=1).astype(BF16)
    w_pa, w_pb, w_o = w_proj_a.astype(BF16), w_proj_b.astype(BF16), w_out.astype(BF16)
    w_u, w_d = w_up.astype(BF16), w_down.astype(BF16)
    b_a3 = b_a.reshape(depth, 1, GK_W)
    gn3 = gla_norm_g.reshape(depth, 1, GLA_DV)
    g1, b1 = ln1_g.reshape(depth, 1, D_MODEL), ln1_b.reshape(depth, 1, D_MODEL)
    g2, b2 = ln2_g.reshape(depth, 1, D_MODEL), ln2_b.reshape(depth, 1, D_MODEL)
    sink_lanes = jnp.broadcast_to(attn_sink[:, :, None], (depth, N_Q_HEADS, LANES))

    cos_p, slo_p, shi_p = _rope_tables(jnp.arange(lp) - META_PAD)
    cos_s, slo_s, shi_s = (jnp.broadcast_to(t, (nsmp, LANES)) for t in _rope_tables(PAST_LEN + jnp.arange(1)))

    meta_block = jnp.concatenate([jnp.zeros((META_PAD, D_MODEL), x_prompt.dtype),
                                  meta_tokens.astype(x_prompt.dtype)], axis=0)
    nblk = ts // BLOCK
    xp = None
    xs = x_sample.reshape(nsmp, D_MODEL)
    ck = jnp.transpose(cache_k_win, (0, 1, 3, 4, 2))
    cv = jnp.transpose(cache_v_win, (0, 1, 3, 4, 2))

    cparams = functools.partial(pltpu.CompilerParams, vmem_limit_bytes=VMEM_LIMIT)
    pk, pv, pst = [], [], []
    stacked = []
    for l in range(depth):
        n_total = bsz * n_steps
        seq_of = lambda t: jnp.minimum(t, n_total - 1) // n_steps
        step_of = lambda t: jnp.minimum(t, n_total - 1) % n_steps
        step_spec = pl.BlockSpec((1, ts, D_MODEL), lambda t: (seq_of(t), step_of(t), 0))
        h_spec = pl.BlockSpec((1, ts, D_MODEL), lambda t: (seq_of(jnp.maximum(t - 1, 0)), step_of(jnp.maximum(t - 1, 0)), 0))
        tab_spec = pl.BlockSpec((ts, LANES), lambda t: (step_of(t), 0))
        if l == 0:
            x_specs = [_const_spec((BLOCK, D_MODEL))] + [
                pl.BlockSpec((1, BLOCK, D_MODEL),
                             lambda t, j=j: (seq_of(t), jnp.maximum(nblk * step_of(t) + j - 1, 0), 0))
                for j in range(nblk)]
            x_args = [meta_block] + [x_prompt] * nblk
        else:
            x_specs, x_args = [step_spec], [xp]
        hp, kwin, vwin, sfin = pl.pallas_call(
            functools.partial(_mixer_prompt_kernel, ts=ts, n_steps=n_steps, n_total=n_total, alpha=alpha,
                              from_tokens=(l == 0)),
            grid=(n_total + 1,),
            in_specs=x_specs + [tab_spec, tab_spec, tab_spec,
                      _const_spec((W_IN_COLS, D_MODEL), l), _const_spec((LANES, GK_W), l), _const_spec((1, GK_W), l),
                      pl.BlockSpec(memory_space=pltpu.SMEM), _const_spec((1, GLA_DV), l),
                      _const_spec((ATT_W, D_MODEL), l), _const_spec((GV_W, D_MODEL), l),
                      _const_spec((D_MODEL, D_MODEL), l), _const_spec((1, D_MODEL), l), _const_spec((1, D_MODEL), l)],
            out_specs=[h_spec,
                       pl.BlockSpec((1, BLOCK, KV_W), lambda t: (seq_of(t), 0, 0)),
                       pl.BlockSpec((1, BLOCK, KV_W), lambda t: (seq_of(t), 0, 0)),
                       pl.BlockSpec((1, GLA_HEADS, GLA_DK, GLA_DV), lambda t: (seq_of(t), 0, 0, 0))],
            out_shape=[jax.ShapeDtypeStruct((bsz, lp, D_MODEL), F32),
                       jax.ShapeDtypeStruct((bsz, BLOCK, KV_W), F32),
                       jax.ShapeDtypeStruct((bsz, BLOCK, KV_W), F32),
                       jax.ShapeDtypeStruct((bsz, GLA_HEADS, GLA_DK, GLA_DV), F32)],
            scratch_shapes=[pltpu.VMEM((N_KV_HEADS, BLOCK, KV_W), BF16), pltpu.VMEM((KV_W, BLOCK), BF16),
                            pltpu.VMEM((GK_W, GV_W), F32), pltpu.VMEM((GK_W, GV_W), F32),
                            pltpu.VMEM((ts, GK_W), F32), pltpu.VMEM((ts, GK_W), F32), pltpu.VMEM((ts, GV_W), F32),
                            pltpu.VMEM((ts, GK_W), F32), pltpu.VMEM((ts, GV_W), F32), pltpu.VMEM((ATT_W, ts), F32),
                            pltpu.VMEM((ts, W_IN_COLS - C_GR), F32), pltpu.VMEM((ts, D_MODEL), F32)],
            compiler_params=cparams(dimension_semantics=("arbitrary",)),
            name=f"mixer_prompt_{l}",
        )(*x_args, cos_p, slo_p, shi_p, w_in_r, w_a2_p, b_a3, attn_sink[l], gn3, w_pa, w_pb, w_o, g1, b1)
        pk.append(kwin.reshape(bsz, N_KV_HEADS, HEAD_DIM, BLOCK))
        pv.append(vwin.reshape(bsz, N_KV_HEADS, HEAD_DIM, BLOCK))
        pst.append(sfin)

        ffn = functools.partial(_ffn_kernel, alpha=alpha, col_chunk=FFN_COL_CHUNK)
        ffn_w = [_const_spec((D_MODEL, D_FF), l), _const_spec((D_FF, D_MODEL), l),
                 _const_spec((1, D_MODEL), l), _const_spec((1, D_MODEL), l)]
        if l < depth - 1:
            n_tiles = rows // ffn_tile
            xp = pl.pallas_call(
                functools.partial(ffn, n_total=n_tiles), grid=(n_tiles + 1,),
                in_specs=[pl.BlockSpec((ffn_tile, D_MODEL), lambda t: (jnp.minimum(t, n_tiles - 1), 0))] + ffn_w,
                out_specs=pl.BlockSpec((ffn_tile, D_MODEL), lambda t: (jnp.maximum(t - 1, 0), 0)),
                out_shape=jax.ShapeDtypeStruct((rows, D_MODEL), F32),
                scratch_shapes=[pltpu.VMEM((ffn_tile, D_MODEL), F32)],
                compiler_params=cparams(dimension_semantics=("arbitrary",)),
                name=f"ffn_prompt_{l}",
            )(hp.reshape(rows, D_MODEL), w_u, w_d, g2, b2).reshape(bsz, lp, D_MODEL)
        else:
            pieces = last_tile // BLOCK
            per_seq = seq // last_tile
            n_tiles = bsz * per_seq
            rd = lambda t: jnp.minimum(t, n_tiles - 1)
            wr = lambda t: jnp.maximum(t - 1, 0)
            y_prompt = pl.pallas_call(
                functools.partial(ffn, n_total=n_tiles), grid=(n_tiles + 1,),
                in_specs=[pl.BlockSpec((1, BLOCK, D_MODEL),
                                       lambda t, j=j: (rd(t) // per_seq, 1 + pieces * (rd(t) % per_seq) + j, 0))
                          for j in range(pieces)] + ffn_w,
                out_specs=pl.BlockSpec((1, last_tile, D_MODEL), lambda t: (wr(t) // per_seq, wr(t) % per_seq, 0)),
                out_shape=jax.ShapeDtypeStruct((bsz, seq, D_MODEL), F32),
                scratch_shapes=[pltpu.VMEM((last_tile, D_MODEL), F32)],
                compiler_params=cparams(dimension_semantics=("arbitrary",)),
                name=f"ffn_prompt_{l}",
            )(*([hp] * pieces), w_u, w_d, g2, b2)

        proj_out = [(nsmp, C_GQ), (nsmp, GV_W), (nsmp, W_IN_COLS - C_GR),
                    (2 * N_SPLIT, KV_W, nsmp), (3 * N_SPLIT, GK_W, nsmp)]
        qkv, gv_s, gate, kvt, gcol = pl.pallas_call(
            _sample_proj_kernel, grid=(1,),
            in_specs=[_const_spec((nsmp, D_MODEL)), _const_spec((nsmp, LANES)), _const_spec((nsmp, LANES)),
                      _const_spec((nsmp, LANES)), _const_spec((W_IN_COLS, D_MODEL), l),
                      _const_spec((LANES, GK_W), l), _const_spec((1, GK_W), l)],
            out_specs=[_const_spec(s) for s in proj_out],
            out_shape=[jax.ShapeDtypeStruct(s, F32) for s in proj_out],
            compiler_params=cparams(dimension_semantics=("arbitrary",)),
            name=f"sample_proj_{l}",
        )(xs, cos_s, slo_s, shi_s, w_in_r, w_a2_p, b_a3)
        def by_group(t):
            t = t.reshape(t.shape[0] // N_SPLIT, N_SPLIT, t.shape[1], nsmp // group, group)
            return jnp.transpose(t, (3, 0, 2, 1, 4)).reshape(nsmp // group, t.shape[0], t.shape[2], N_SPLIT * group)
        col_spec = lambda n, width: pl.BlockSpec((None, n, width, N_SPLIT * group), lambda i: (i, 0, 0, 0))

        grp = lambda width: pl.BlockSpec((group, width), lambda i: (i, 0))
        cache_spec = pl.BlockSpec((None, group, N_KV_HEADS, HEAD_DIM, BLOCK), lambda i: (l, i, 0, 0, 0))
        state_spec = pl.BlockSpec((None, group, GLA_HEADS, GLA_DK, GLA_DV), lambda i: (l, i, 0, 0, 0))
        n_mix_in = 8
        att_s, o_s, *stacked = pl.pallas_call(
            functools.partial(_sample_mix_kernel, group=group), grid=(nsmp // group,),
            in_specs=[grp(C_GQ), grp(GV_W), col_spec(2, KV_W), col_spec(3, GK_W), cache_spec, cache_spec, state_spec,
                      _const_spec((N_Q_HEADS, LANES), l)] + [pl.BlockSpec(memory_space=pl.ANY)] * len(stacked),
            out_specs=[grp(ATT_W), grp(GV_W), cache_spec, cache_spec, state_spec],
            out_shape=[jax.ShapeDtypeStruct((nsmp, ATT_W), F32), jax.ShapeDtypeStruct((nsmp, GV_W), F32),
                       jax.ShapeDtypeStruct(ck.shape, F32), jax.ShapeDtypeStruct(cv.shape, F32),
                       jax.ShapeDtypeStruct(state_gla.shape, F32)],
            input_output_aliases={n_mix_in + i: 2 + i for i in range(len(stacked))},
            compiler_params=cparams(dimension_semantics=("arbitrary",)),
            name=f"sample_mix_{l}",
        )(qkv, gv_s, by_group(kvt), by_group(gcol), ck, cv, state_gla, sink_lanes, *stacked)

        hs = pl.pallas_call(
            functools.partial(_sample_finish_kernel, alpha=alpha), grid=(1,),
            in_specs=[_const_spec((nsmp, D_MODEL)), _const_spec((nsmp, ATT_W)), _const_spec((nsmp, GV_W)),
                      _const_spec((nsmp, W_IN_COLS - C_GR)), _const_spec((1, GLA_DV), l),
                      _const_spec((ATT_W, D_MODEL), l), _const_spec((GV_W, D_MODEL), l),
                      _const_spec((D_MODEL, D_MODEL), l), _const_spec((1, D_MODEL), l), _const_spec((1, D_MODEL), l)],
            out_specs=_const_spec((nsmp, D_MODEL)),
            out_shape=jax.ShapeDtypeStruct((nsmp, D_MODEL), F32),
            compiler_params=cparams(dimension_semantics=("arbitrary",)),
            name=f"sample_finish_{l}",
        )(xs, att_s, o_s, gate, gn3, w_pa, w_pb, w_o, g1, b1)
        xs = pl.pallas_call(
            ffn, grid=(1,),
            in_specs=[_const_spec((nsmp, D_MODEL))] + ffn_w,
            out_specs=_const_spec((nsmp, D_MODEL)),
            out_shape=jax.ShapeDtypeStruct((nsmp, D_MODEL), F32),
            compiler_params=cparams(dimension_semantics=("arbitrary",)),
            name=f"ffn_sample_{l}",
        )(hs, w_u, w_d, g2, b2)

    y_sample = xs.reshape(nsmp, 1, D_MODEL)
    to_rows = lambda t: jnp.transpose(t, (0, 1, 4, 2, 3))
    return (y_prompt, y_sample, to_rows(jnp.stack(pk)), to_rows(jnp.stack(pv)), jnp.stack(pst),
            to_rows(stacked[0]), to_rows(stacked[1]), stacked[2])
```

```python
import functools

import jax
import jax.numpy as jnp
from jax import lax
from jax.experimental import pallas as pl
from jax.experimental.pallas import tpu as pltpu

F32 = jnp.float32
BF16 = jnp.bfloat16

D_MODEL = 1024
PAST_LEN = 8192
N_META = 16
BLOCK = 128
META_PAD = BLOCK - N_META
HEAD_DIM = 64
N_Q_HEADS = 8
N_KV_HEADS = 2
Q_PER_KV = N_Q_HEADS // N_KV_HEADS
ROT_DIM = HEAD_DIM // 4
ROPE_THETA = 500000.0
GLA_HEADS = 4
GLA_DK = 64
GLA_DV = 128
GLA_RANK = 16
GLA_TAU = 16.0
D_FF = 4 * D_MODEL
ATT_W = N_Q_HEADS * HEAD_DIM
KV_W = N_KV_HEADS * HEAD_DIM
GK_W = GLA_HEADS * GLA_DK
GV_W = GLA_HEADS * GLA_DV
LOG2_E = 1.4426950408889634
LN_EPS = 1e-5
RMS_EPS = 1e-6
LANES = 128
SUBLANES = 8
VMEM_LIMIT = 56 * 1024 * 1024

C_Q = 0
C_K = C_Q + ATT_W
C_V = C_K + KV_W
C_GQ = C_V + KV_W
C_GK = C_GQ + GK_W
C_GV = C_GK + GK_W
C_LR = C_GV + GV_W
C_GR = C_LR + GLA_RANK
C_GA = C_GR + GV_W
C_GB = C_GA + D_MODEL
W_IN_COLS = C_GB + D_MODEL
C_LR_END = C_LR + LANES

GLA_FAST_MAX_DECAY = 40.0


def _sigmoid(x):
    return 1.0 / (1.0 + jnp.exp(-x))


def _layer_norm(y, g, b):
    mu = jnp.mean(y, axis=-1, keepdims=True)
    yc = y - mu
    var = jnp.mean(yc * yc, axis=-1, keepdims=True)
    return yc * lax.rsqrt(var + LN_EPS) * g + b


def _rope(t, cos, sin_lo, sin_hi):
    outs = []
    for j in range(t.shape[1] // LANES):
        tj = t[:, j * LANES:(j + 1) * LANES]
        outs.append(tj * cos + pltpu.roll(tj, LANES - ROT_DIM // 2, 1) * sin_lo
                    + pltpu.roll(tj, ROT_DIM // 2, 1) * sin_hi)
    return outs[0] if len(outs) == 1 else jnp.concatenate(outs, axis=1)


def _in_proj(xb, w_t_ref, lo, hi):
    return lax.dot_general(xb, w_t_ref[lo:hi, :], (((1,), (1,)), ((), ())), preferred_element_type=F32)


def _log_decay(glr, w_a2, b_a):
    z = jnp.dot(glr.astype(BF16), w_a2, preferred_element_type=F32) + b_a
    return (jnp.minimum(z, 0.0) - jnp.log1p(jnp.exp(-jnp.abs(z)))) * (1.0 / GLA_TAU)


def _row_to_col(row):
    n = row.shape[1]
    eye = lax.broadcasted_iota(jnp.int32, (n, n), 0) == lax.broadcasted_iota(jnp.int32, (n, n), 1)
    return jnp.sum(jnp.where(eye, jnp.broadcast_to(row, (n, n)), 0.0), axis=1, keepdims=True)


N_SPLIT = 3


def _split3_bf16(a):
    hi = a.astype(BF16)
    r = a - hi.astype(F32)
    mid = r.astype(BF16)
    lo = (r - mid.astype(F32)).astype(BF16)
    return hi, mid, lo


def _swish(x):
    return x * _sigmoid(x)


def _gla_gate_out(o, swish_gr, gn):
    outs = []
    for h in range(GLA_HEADS):
        oh = o[:, h * GLA_DV:(h + 1) * GLA_DV]
        ms = jnp.mean(oh * oh, axis=-1, keepdims=True)
        outs.append(oh * lax.rsqrt(ms + RMS_EPS) * gn)
    return jnp.concatenate(outs, axis=1) * swish_gr


def _finish(x, pa, gla, sig_a, sig_b, w_pb, w_out, alpha):
    pb = jnp.dot(gla.astype(BF16), w_pb, preferred_element_type=F32)
    m = sig_a * pa + sig_b * pb
    return alpha * x + jnp.dot(m.astype(BF16), w_out, preferred_element_type=F32)


_MIXER_REF_NAMES = ("cos", "slo", "shi", "w_in", "w_a2", "b_a", "sink", "gn", "w_pa", "w_pb", "w_out", "g1", "b1",
                    "h", "kwin", "vwin", "sfin",
                    "kprev_scr", "vprev_scr", "s_scr", "sprev_scr", "gq_scr", "gk_scr", "gv_scr", "la_scr",
                    "o_scr", "att_t_scr", "gate_scr", "y_scr")


def _mixer_prompt_kernel(*refs, ts, n_steps, n_total, alpha, from_tokens):
    t = pl.program_id(0)
    named = dict(zip(_MIXER_REF_NAMES, refs[len(refs) - len(_MIXER_REF_NAMES):]))
    h_ref, y_scr, g1_ref, b1_ref = named["h"], named["y_scr"], named["g1"], named["b1"]

    @pl.when(t == 0)
    def _():
        y_scr[...] = jnp.zeros(y_scr.shape, F32)

    @pl.when(t < n_total)
    def _():
        _mixer_step(lax.rem(t, n_steps), *refs, ts=ts, n_steps=n_steps, alpha=alpha, from_tokens=from_tokens)

    @pl.when(t == n_total)
    def _():
        h_ref[0] = _layer_norm(y_scr[...], g1_ref[...], b1_ref[...])


def _mixer_step(s, *refs, ts, n_steps, alpha, from_tokens):
    nblk = ts // BLOCK
    n_x = 1 + nblk if from_tokens else 1
    x_refs = refs[:n_x]
    (cos_ref, slo_ref, shi_ref, w_in_ref, w_a2_ref, b_a_ref, sink_ref, gn_ref,
     w_pa_ref, w_pb_ref, w_out_ref, g1_ref, b1_ref,
     h_ref, kwin_ref, vwin_ref, sfin_ref,
     kprev_scr, vprev_scr, s_scr, sprev_scr, gq_scr, gk_scr, gv_scr, la_scr, o_scr, att_t_scr, gate_scr,
     y_scr) = refs[n_x:]
    assert len(refs) - n_x == len(_MIXER_REF_NAMES)

    def load_x():
        if not from_tokens:
            return x_refs[0][0]
        blocks = [r[0] for r in x_refs[1:]]
        blocks[0] = jnp.where(s == 0, x_refs[0][...], blocks[0])
        return jnp.concatenate(blocks, axis=0)

    @pl.when(s == 0)
    def _():
        kprev_scr[...] = jnp.zeros((N_KV_HEADS, BLOCK, KV_W), BF16)
        vprev_scr[...] = jnp.zeros((KV_W, BLOCK), BF16)
        s_scr[...] = jnp.zeros(s_scr.shape, F32)

    x = load_x()
    xb = x.astype(BF16)
    proj = functools.partial(_in_proj, xb, w_in_ref)

    mix_in = proj(C_Q, C_LR_END)
    live = (s * ts + lax.broadcasted_iota(jnp.int32, (ts, 1), 0)) >= META_PAD
    cos, slo, shi = cos_ref[...], slo_ref[...], shi_ref[...]
    q = _rope(mix_in[:, C_Q:C_K], cos, slo, shi)
    k = _rope(mix_in[:, C_K:C_V], cos, slo, shi)
    v = mix_in[:, C_V:C_GQ]
    gq = mix_in[:, C_GQ:C_GK] * (GLA_DK ** -0.5)
    gk = jnp.where(live, mix_in[:, C_GK:C_GV], 0.0)
    gv = jnp.where(live, mix_in[:, C_GV:C_LR], 0.0)
    v_t = v.T

    @pl.when(s == n_steps - 1)
    def _():
        kwin_ref[0] = k[ts - BLOCK:, :].T
        vwin_ref[0] = v_t[:, ts - BLOCK:]

    gate_w = W_IN_COLS - C_GR
    piece_cols = 4 * LANES
    n_gate_pieces = min(N_KV_HEADS * nblk, -(-gate_w // piece_cols))
    gate_edges = [min(gate_w, piece_cols * (-(-gate_w // piece_cols) * i // n_gate_pieces))
                  for i in range(n_gate_pieces)] + [gate_w]

    def gate_piece_matmul(i):
        return proj(C_GR + gate_edges[i], C_GR + gate_edges[i + 1])

    def gate_piece_store(i, val):
        lo, hi = gate_edges[i], gate_edges[i + 1]
        mid = min(max(GV_W, lo), hi)
        if mid > lo:
            gate_scr[:, lo:mid] = _swish(val[:, :mid - lo])
        if hi > mid:
            gate_scr[:, mid:hi] = _sigmoid(val[:, mid - lo:])

    tri = (lax.broadcasted_iota(jnp.int32, (BLOCK, BLOCK), 1)
           <= lax.broadcasted_iota(jnp.int32, (BLOCK, BLOCK), 0))
    tri_bf = jnp.where(tri, 1.0, 0.0).astype(BF16)
    head_of_k = lax.broadcasted_iota(jnp.int32, (BLOCK, GK_W), 1) // GLA_DK
    head_of_v = lax.broadcasted_iota(jnp.int32, (BLOCK, GV_W), 1) // GLA_DV
    state_diag = (lax.broadcasted_iota(jnp.int32, (GK_W, GV_W), 0) // GLA_DK
                  == lax.broadcasted_iota(jnp.int32, (GK_W, GV_W), 1) // GLA_DV)
    causal = (lax.broadcasted_iota(jnp.int32, (BLOCK, GLA_HEADS * BLOCK), 1) % BLOCK
              <= lax.broadcasted_iota(jnp.int32, (BLOCK, GLA_HEADS * BLOCK), 0))
    gla = {"cums": [], "chunks": []}

    def issue_decay():
        gla["la"] = jnp.where(live, _log_decay(mix_in[:, C_LR:C_LR_END], w_a2_ref[...], b_a_ref[...]), 0.0)

    def issue_cumsum():
        worst = jnp.zeros((1, GK_W), F32)
        for c in range(nblk):
            parts = jnp.concatenate(_split3_bf16(gla["la"][c * BLOCK:(c + 1) * BLOCK]), axis=1)
            b3 = jnp.dot(tri_bf, parts, preferred_element_type=F32)
            b = b3[:, 0:GK_W] + b3[:, GK_W:2 * GK_W] + b3[:, 2 * GK_W:3 * GK_W]
            gla["cums"].append(b)
            worst = jnp.maximum(worst, -b[BLOCK - 1:BLOCK, :])
        gla["worst"] = worst

    def issue_chunk(c):
        rows = slice(c * BLOCK, (c + 1) * BLOCK)
        b = gla["cums"][c]
        b_last = b[BLOCK - 1:BLOCK, :]
        kc = gk[rows]
        q_dec = (gq[rows] * jnp.exp(b)).astype(BF16)
        k_inv = (kc * jnp.exp(-b)).astype(BF16)
        k_end_t = (kc * jnp.exp(b_last - b)).T.astype(BF16)
        vc = gv[rows].astype(BF16)
        zk = jnp.zeros_like(k_inv)
        k_bd = jnp.concatenate([jnp.where(head_of_k == h, k_inv, zk) for h in range(GLA_HEADS)], axis=0)
        a = lax.dot_general(q_dec, k_bd, (((1,), (1,)), ((), ())), preferred_element_type=F32)
        a = jnp.where(causal, a, 0.0).astype(BF16)
        zv = jnp.zeros_like(vc)
        v_bd = jnp.concatenate([jnp.where(head_of_v == h, vc, zv) for h in range(GLA_HEADS)], axis=0)
        o_intra = jnp.dot(a, v_bd, preferred_element_type=F32)
        ds = jnp.where(state_diag, jnp.dot(k_end_t, vc, preferred_element_type=F32), 0.0)
        gla["chunks"].append((rows, q_dec, o_intra, ds, _row_to_col(jnp.exp(b_last))))

    n_pairs = N_KV_HEADS * nblk
    second = min(1, n_pairs - 1)
    side_work = {}
    for pair, issue in ([(0, issue_decay), (second, issue_cumsum)]
                        + [(max(second, n_pairs - nblk + c), functools.partial(issue_chunk, c)) for c in range(nblk)]):
        side_work.setdefault(pair, []).append(issue)

    q_bf = (q * (HEAD_DIM ** -0.5 * LOG2_E)).astype(BF16)
    low_half = lax.broadcasted_iota(jnp.int32, (1, KV_W), 1) < HEAD_DIM
    k_swapped = pltpu.roll(k, HEAD_DIM, 1)
    k_dup = [jnp.where(low_half, k, k_swapped).astype(BF16), jnp.where(low_half, k_swapped, k).astype(BF16)]
    k_keys = [jnp.concatenate([kprev_scr[i], k_dup[i]], axis=0) for i in range(N_KV_HEADS)]
    vt_bf = v_t.astype(BF16)
    vt_keys = jnp.concatenate([vprev_scr[...], vt_bf], axis=1)
    kj = lax.broadcasted_iota(jnp.int32, (2 * BLOCK, Q_PER_KV * BLOCK), 0)
    qi = lax.broadcasted_iota(jnp.int32, (2 * BLOCK, Q_PER_KV * BLOCK), 1) % BLOCK
    band = (kj - qi >= 1) & (kj - qi <= BLOCK)
    q_low_half = lax.broadcasted_iota(jnp.int32, (BLOCK, LANES), 1) < HEAD_DIM
    pieces_done = 0
    for blk in range(nblk):
        first_key_slot = (s * nblk + blk - 1) * BLOCK
        valid = band & (kj + first_key_slot >= META_PAD)
        r0 = blk * BLOCK
        for kv in range(N_KV_HEADS):
            heads = [kv * Q_PER_KV + g for g in range(Q_PER_KV)]
            q_rows = []
            for hq in heads:
                grp = q_bf[r0:r0 + BLOCK, (hq // 2) * LANES:(hq // 2 + 1) * LANES]
                own = q_low_half if hq % 2 == 0 else jnp.logical_not(q_low_half)
                q_rows.append(jnp.where(own, grp, jnp.zeros_like(grp)))
            st = lax.dot_general(k_keys[kv][r0:r0 + 2 * BLOCK, :], jnp.concatenate(q_rows, axis=0),
                                 (((1,), (1,)), ((), ())), preferred_element_type=F32)
            gate_val = gate_piece_matmul(pieces_done) if pieces_done < n_gate_pieces else None
            for issue in side_work.get(blk * N_KV_HEADS + kv, ()):
                issue()
            st = jnp.where(valid, st, -jnp.inf)
            sink_row = jnp.concatenate([jnp.full((1, BLOCK), sink_ref[hq] * LOG2_E, F32) for hq in heads], axis=1)
            m = jnp.maximum(jnp.max(st, axis=0, keepdims=True), sink_row)
            p = jnp.exp2(st - m)
            den = jnp.sum(p, axis=0, keepdims=True) + jnp.exp2(sink_row - m)
            ot = jnp.dot(vt_keys[kv * HEAD_DIM:(kv + 1) * HEAD_DIM, r0:r0 + 2 * BLOCK], p.astype(BF16),
                         preferred_element_type=F32) * (1.0 / den)
            for g, hq in enumerate(heads):
                att_t_scr[hq * HEAD_DIM:(hq + 1) * HEAD_DIM, r0:r0 + BLOCK] = ot[:, g * BLOCK:(g + 1) * BLOCK]
            if gate_val is not None:
                gate_piece_store(pieces_done, gate_val)
                pieces_done += 1
    for i in range(pieces_done, n_gate_pieces):
        gate_piece_store(i, gate_piece_matmul(i))
    for i in range(N_KV_HEADS):
        kprev_scr[i] = k_dup[i][ts - BLOCK:, :]
    vprev_scr[...] = vt_bf[:, ts - BLOCK:]

    fast_ok = jnp.max(gla["worst"]) <= GLA_FAST_MAX_DECAY

    def proj_att():
        return jnp.dot(att_t_scr[...].T.astype(BF16), w_pa_ref[...], preferred_element_type=F32)

    def gla_fast():
        pa = proj_att()
        for rows, q_dec, o_intra, ds, decay_col in gla["chunks"]:
            s0 = s_scr[...]
            o_scr[rows, :] = o_intra + jnp.dot(q_dec, s0.astype(BF16), preferred_element_type=F32)
            s_scr[...] = decay_col * s0 + ds
        return pa

    def gla_slow():
        def body(i, carry):
            rows = pl.ds(pl.multiple_of(i * SUBLANES, SUBLANES), SUBLANES)
            la8, k8, q8, v8 = la_scr[rows, :], gk_scr[rows, :], gq_scr[rows, :], gv_scr[rows, :]
            outs = []
            for r in range(SUBLANES):
                a_col = _row_to_col(jnp.exp(la8[r:r + 1]))
                k_col = _row_to_col(k8[r:r + 1])
                q_col = _row_to_col(q8[r:r + 1])
                s1 = a_col * s_scr[...] + jnp.where(state_diag, k_col * v8[r:r + 1], 0.0)
                s_scr[...] = s1
                outs.append(jnp.sum(q_col * s1, axis=0, keepdims=True))
            o_scr[rows, :] = jnp.concatenate(outs, axis=0)
            return carry
        lax.fori_loop(0, ts // SUBLANES, body, 0)

    def finish_step(x_val, pa):
        gla = _gla_gate_out(o_scr[...], gate_scr[:, 0:GV_W], gn_ref[...])
        y_scr[...] = _finish(x_val, pa, gla, gate_scr[:, GV_W:GV_W + D_MODEL], gate_scr[:, GV_W + D_MODEL:],
                             w_pb_ref[...], w_out_ref[...], alpha)

    sprev_scr[...] = s_scr[...]
    pa = gla_fast()
    h_ref[0] = _layer_norm(y_scr[...], g1_ref[...], b1_ref[...])
    finish_step(x, pa)

    @pl.when(jnp.logical_not(fast_ok))
    def _():
        x_again = load_x()
        proj_again = functools.partial(_in_proj, x_again.astype(BF16), w_in_ref)
        gq_scr[...] = proj_again(C_GQ, C_GK) * (GLA_DK ** -0.5)
        gk_scr[...] = jnp.where(live, proj_again(C_GK, C_GV), 0.0)
        gv_scr[...] = jnp.where(live, proj_again(C_GV, C_LR), 0.0)
        la_scr[...] = jnp.where(live, _log_decay(proj_again(C_LR, C_LR_END), w_a2_ref[...], b_a_ref[...]), 0.0)
        s_scr[...] = sprev_scr[...]
        gla_slow()
        finish_step(x_again, proj_att())

    @pl.when(s == n_steps - 1)
    def _():
        for h in range(GLA_HEADS):
            sfin_ref[0, h] = s_scr[h * GLA_DK:(h + 1) * GLA_DK, h * GLA_DV:(h + 1) * GLA_DV]


def _ffn_update(h, w_up_ref, w_dn_ref, col_chunk, after_first_chunk=None):
    hb = h.astype(BF16)
    acc = jnp.zeros(h.shape, F32)
    for c in range(D_FF // col_chunk):
        u = jnp.dot(hb, w_up_ref[:, c * col_chunk:(c + 1) * col_chunk], preferred_element_type=F32)
        u = jnp.maximum(u, 0.0)
        acc = acc + jnp.dot((u * u).astype(BF16), w_dn_ref[c * col_chunk:(c + 1) * col_chunk, :],
                            preferred_element_type=F32)
        if c == 0 and after_first_chunk is not None:
            after_first_chunk()
    return acc


def _ffn_kernel(*refs, alpha, col_chunk, n_total):
    h_refs, (w_up_ref, w_dn_ref, g_ref, b_ref, o_ref, y_scr) = refs[:-6], refs[-6:]

    def write_norm():
        o_ref[...] = _layer_norm(y_scr[...], g_ref[...], b_ref[...]).reshape(o_ref.shape)

    def tile():
        blocks = [r[0] if len(r.shape) == 3 else r[...] for r in h_refs]
        h = blocks[0] if len(blocks) == 1 else jnp.concatenate(blocks, axis=0)
        acc = _ffn_update(h, w_up_ref, w_dn_ref, col_chunk, after_first_chunk=write_norm)
        y_scr[...] = alpha * h + acc

    t = pl.program_id(0)

    @pl.when(t == 0)
    def _():
        y_scr[...] = jnp.zeros(y_scr.shape, F32)

    pl.when(t < n_total)(tile)
    pl.when(t == n_total)(write_norm)


def _sample_proj_kernel(x_ref, cos_ref, slo_ref, shi_ref, w_in_ref, w_a2_ref, b_a_ref,
                        qkv_ref, gv_ref, gate_ref, kvt_ref, gcol_ref):
    proj = functools.partial(_in_proj, x_ref[...].astype(BF16), w_in_ref)

    cos, slo, shi = cos_ref[...], slo_ref[...], shi_ref[...]
    k = _rope(proj(C_K, C_V), cos, slo, shi)
    v = proj(C_V, C_GQ)
    qkv_ref[:, C_Q:C_K] = _rope(proj(C_Q, C_K), cos, slo, shi)
    qkv_ref[:, C_K:C_V] = k
    qkv_ref[:, C_V:C_GQ] = v
    gv_ref[...] = proj(C_GV, C_LR)
    gate_ref[...] = proj(C_GR, W_IN_COLS)
    def store_planes(ref, i, t):
        for p, part in enumerate(_split3_bf16(t)):
            ref[N_SPLIT * i + p] = part.astype(F32).T

    store_planes(kvt_ref, 0, k)
    store_planes(kvt_ref, 1, v)
    store_planes(gcol_ref, 0, jnp.exp(_log_decay(proj(C_LR, C_LR_END), w_a2_ref[...], b_a_ref[...])))
    store_planes(gcol_ref, 1, proj(C_GQ, C_GK) * (GLA_DK ** -0.5))
    store_planes(gcol_ref, 2, proj(C_GK, C_GV))


def _sample_mix_kernel(qkv_ref, gv_ref, kvt_ref, gcol_ref, ck_ref, cv_ref, st_ref, sink_ref, *rest, group):
    att_ref, o_ref, nk_ref, nv_ref, nst_ref = rest[-5:]
    head_row = lax.broadcasted_iota(jnp.int32, (N_Q_HEADS, ATT_W), 0)
    head_lane = lax.broadcasted_iota(jnp.int32, (N_Q_HEADS, ATT_W), 1) // HEAD_DIM
    own = head_row == head_lane
    r8 = lax.broadcasted_iota(jnp.int32, (N_Q_HEADS, KV_W), 0)
    swap = (r8 % 2) != (r8 // Q_PER_KV)
    key_i = lax.broadcasted_iota(jnp.int32, (N_Q_HEADS, BLOCK), 1)
    last_row = lax.broadcasted_iota(jnp.int32, (KV_W, BLOCK), 1) == BLOCK - 1
    sink = sink_ref[...][:, 0:1]

    qkv8 = qkv_ref[...]
    gv8 = gv_ref[...]
    q8s, scores = [], []
    for j in range(group):
        q_row = qkv8[j:j + 1, C_Q:C_K]
        qm = jnp.where(own, jnp.broadcast_to(q_row, (N_Q_HEADS, ATT_W)), 0.0)
        fold = qm[:, 0:128] + qm[:, 128:256] + qm[:, 256:384] + qm[:, 384:512]
        q8 = (jnp.where(swap, pltpu.roll(fold, HEAD_DIM, 1), fold) * (HEAD_DIM ** -0.5)).astype(BF16)
        q8s.append(q8)
        scores.append(jnp.dot(q8, ck_ref[j].reshape(KV_W, BLOCK).astype(BF16), preferred_element_type=F32))

    sel = (lax.broadcasted_iota(jnp.int32, (N_SPLIT * group, group * LANES), 0) % group
           == lax.broadcasted_iota(jnp.int32, (N_SPLIT * group, group * LANES), 1) // LANES)
    sel = jnp.where(sel, 1.0, 0.0).astype(BF16)
    spread = lambda ref, i: jnp.dot(ref[i].astype(BF16), sel, preferred_element_type=F32)
    k_cols, v_cols = spread(kvt_ref, 0), spread(kvt_ref, 1)
    a_cols, q_cols, k_gla_cols = (spread(gcol_ref, i) for i in range(3))

    probs = []
    for j in range(group):
        k_new = qkv8[j:j + 1, C_K:C_V]
        sc = jnp.where(key_i >= 1, scores[j], -jnp.inf)
        s_new = jnp.sum(q8s[j].astype(F32) * k_new.astype(BF16).astype(F32), axis=-1, keepdims=True)
        m = jnp.maximum(jnp.maximum(jnp.max(sc, axis=-1, keepdims=True), s_new), sink)
        p = jnp.exp(sc - m)
        p_new = jnp.exp(s_new - m)
        den = jnp.sum(p, axis=-1, keepdims=True) + p_new + jnp.exp(sink - m)
        probs.append((p.astype(BF16), p_new, den))

    att_rows, o_rows = [], []
    for j in range(group):
        p_bf, p_new, den = probs[j]
        v_new = qkv8[j:j + 1, C_V:C_GQ]
        o8 = (lax.dot_general(p_bf, cv_ref[j].reshape(KV_W, BLOCK).astype(BF16), (((1,), (1,)), ((), ())),
                              preferred_element_type=F32)
              + p_new.astype(BF16).astype(F32) * v_new.astype(BF16).astype(F32)) / den
        o8 = jnp.where(swap, pltpu.roll(o8, HEAD_DIM, 1), o8)
        o_wide = jnp.concatenate([o8, o8, o8, o8], axis=1)
        att_rows.append(jnp.sum(jnp.where(own, o_wide, 0.0), axis=0, keepdims=True))

    for j in range(group):
        lanes_j = slice(j * LANES, (j + 1) * LANES)
        k_old = ck_ref[j].reshape(KV_W, BLOCK)
        v_old = cv_ref[j].reshape(KV_W, BLOCK)
        nk_ref[j] = jnp.where(last_row, k_cols[:, lanes_j], pltpu.roll(k_old, BLOCK - 1, 1)).reshape(
            N_KV_HEADS, HEAD_DIM, BLOCK)
        nv_ref[j] = jnp.where(last_row, v_cols[:, lanes_j], pltpu.roll(v_old, BLOCK - 1, 1)).reshape(
            N_KV_HEADS, HEAD_DIM, BLOCK)
        a_col, q_col, k_col = a_cols[:, lanes_j], q_cols[:, lanes_j], k_gla_cols[:, lanes_j]
        v_row = gv8[j:j + 1, :]
        v_exp = jnp.concatenate([jnp.broadcast_to(v_row[:, h * GLA_DV:(h + 1) * GLA_DV], (GLA_DK, GLA_DV))
                                 for h in range(GLA_HEADS)], axis=0)
        s1 = a_col * st_ref[j].reshape(GK_W, GLA_DV) + k_col * v_exp
        nst_ref[j] = s1.reshape(GLA_HEADS, GLA_DK, GLA_DV)
        qs = q_col * s1
        o_rows.append(jnp.concatenate(
            [jnp.sum(qs[h * GLA_DK:(h + 1) * GLA_DK], axis=0, keepdims=True) for h in range(GLA_HEADS)], axis=1))
    att_ref[...] = jnp.concatenate(att_rows, axis=0)
    o_ref[...] = jnp.concatenate(o_rows, axis=0)


def _sample_tail_kernel(x_ref, att_ref, o_ref, gate_ref, gn_ref, w_pa_ref, w_pb_ref, w_out_ref, g1_ref, b1_ref,
                        w_up_ref, w_dn_ref, g2_ref, b2_ref, out_ref, *, alpha, col_chunk):
    gla = _gla_gate_out(o_ref[...], _swish(gate_ref[:, 0:GV_W]), gn_ref[...])
    pa = jnp.dot(att_ref[...].astype(BF16), w_pa_ref[...], preferred_element_type=F32)
    y = _finish(x_ref[...], pa, gla, _sigmoid(gate_ref[:, GV_W:GV_W + D_MODEL]),
                _sigmoid(gate_ref[:, GV_W + D_MODEL:]), w_pb_ref[...], w_out_ref[...], alpha)
    h = _layer_norm(y, g1_ref[...], b1_ref[...])
    acc = _ffn_update(h, w_up_ref, w_dn_ref, col_chunk)
    out_ref[...] = _layer_norm(alpha * h + acc, g2_ref[...], b2_ref[...])


def _rope_tables(pos):
    half = ROT_DIM // 2
    inv = ROPE_THETA ** (-jnp.arange(half, dtype=F32) * 2.0 / ROT_DIM)
    d = jnp.arange(LANES) % HEAD_DIM
    ang = pos.astype(F32)[:, None] * inv[d % half][None, :]
    cos, sin = jnp.cos(ang), jnp.sin(ang)
    cos_t = jnp.where(d < ROT_DIM, cos, 1.0)
    sin_lo = jnp.where(d < half, -sin, 0.0)
    sin_hi = jnp.where((d >= half) & (d < ROT_DIM), sin, 0.0)
    return cos_t, sin_lo, sin_hi


def _const_spec(shape, layer=None):
    if layer is None:
        return pl.BlockSpec(shape, lambda *_: (0,) * len(shape), pipeline_mode=pl.Buffered(1))
    return pl.BlockSpec((None,) + shape, lambda *_: (layer,) + (0,) * len(shape), pipeline_mode=pl.Buffered(1))


MIXER_STEP_ROWS = (3 * BLOCK, 2 * BLOCK, BLOCK)
FFN_TILE_ROWS = (6 * BLOCK, 4 * BLOCK, 3 * BLOCK, 2 * BLOCK, BLOCK)
FFN_COL_CHUNK = D_MODEL


def _first_divisor(total, candidates):
    for t in candidates:
        if total % t == 0:
            return t
    raise ValueError(f"{total} rows are not a multiple of {candidates[-1]}")


def kernel(x_prompt, x_sample, cache_k_win, cache_v_win, state_gla, meta_tokens, w_in, w_a2, b_a, attn_sink,
           gla_norm_g, w_proj_a, w_proj_b, w_out, ln1_g, ln1_b, w_up, w_down, ln2_g, ln2_b):
    depth = w_in.shape[0]
    bsz, seq, _ = x_prompt.shape
    nsmp, dec_seq, _ = x_sample.shape
    assert dec_seq == 1 and cache_k_win.shape[2] == BLOCK and seq % BLOCK == 0
    alpha = (2 * depth) ** 0.25
    lp = seq + BLOCK
    ts = _first_divisor(lp, MIXER_STEP_ROWS)
    n_steps = lp // ts
    rows = bsz * lp
    ffn_tile = _first_divisor(rows, FFN_TILE_ROWS)
    last_tile = _first_divisor(seq, FFN_TILE_ROWS)
    group = next(g for g in (2 * SUBLANES, SUBLANES) if nsmp % g == 0)

    w_in_r = jnp.swapaxes(w_in, 1, 2).astype(BF16)
    w_a2_p = jnp.concatenate([w_a2, jnp.zeros((depth, LANES - GLA_RANK, GK_W), w_a2.dtype)], axis=1).astype(BF16)
    w_pa, w_pb, w_o = w_proj_a.astype(BF16), w_proj_b.astype(BF16), w_out.astype(BF16)
    w_u, w_d = w_up.astype(BF16), w_down.astype(BF16)
    b_a3 = b_a.reshape(depth, 1, GK_W)
    gn3 = gla_norm_g.reshape(depth, 1, GLA_DV)
    g1, b1 = ln1_g.reshape(depth, 1, D_MODEL), ln1_b.reshape(depth, 1, D_MODEL)
    g2, b2 = ln2_g.reshape(depth, 1, D_MODEL), ln2_b.reshape(depth, 1, D_MODEL)
    sink_lanes = jnp.broadcast_to(attn_sink[:, :, None], (depth, N_Q_HEADS, LANES))

    cos_p, slo_p, shi_p = _rope_tables(jnp.arange(lp) - META_PAD)
    cos_s, slo_s, shi_s = (jnp.broadcast_to(t, (nsmp, LANES)) for t in _rope_tables(PAST_LEN + jnp.arange(1)))

    meta_block = jnp.concatenate([jnp.zeros((META_PAD, D_MODEL), x_prompt.dtype),
                                  meta_tokens.astype(x_prompt.dtype)], axis=0)
    nblk = ts // BLOCK
    xp = None
    xs = x_sample.reshape(nsmp, D_MODEL)
    ck = jnp.transpose(cache_k_win, (0, 1, 3, 4, 2))
    cv = jnp.transpose(cache_v_win, (0, 1, 3, 4, 2))

    cparams = functools.partial(pltpu.CompilerParams, vmem_limit_bytes=VMEM_LIMIT)
    pk, pv, pst = [], [], []
    stacked = []
    for l in range(depth):
        n_total = bsz * n_steps
        seq_of = lambda t: jnp.minimum(t, n_total - 1) // n_steps
        step_of = lambda t: jnp.minimum(t, n_total - 1) % n_steps
        step_spec = pl.BlockSpec((1, ts, D_MODEL), lambda t: (seq_of(t), step_of(t), 0))
        h_spec = pl.BlockSpec((1, ts, D_MODEL), lambda t: (seq_of(jnp.maximum(t - 1, 0)), step_of(jnp.maximum(t - 1, 0)), 0))
        tab_spec = pl.BlockSpec((ts, LANES), lambda t: (step_of(t), 0))
        if l == 0:
            x_specs = [_const_spec((BLOCK, D_MODEL))] + [
                pl.BlockSpec((1, BLOCK, D_MODEL),
                             lambda t, j=j: (seq_of(t), jnp.maximum(nblk * step_of(t) + j - 1, 0), 0))
                for j in range(nblk)]
            x_args = [meta_block] + [x_prompt] * nblk
        else:
            x_specs, x_args = [step_spec], [xp]
        hp, kwin, vwin, sfin = pl.pallas_call(
            functools.partial(_mixer_prompt_kernel, ts=ts, n_steps=n_steps, n_total=n_total, alpha=alpha,
                              from_tokens=(l == 0)),
            grid=(n_total + 1,),
            in_specs=x_specs + [tab_spec, tab_spec, tab_spec,
                      _const_spec((W_IN_COLS, D_MODEL), l), _const_spec((LANES, GK_W), l), _const_spec((1, GK_W), l),
                      pl.BlockSpec(memory_space=pltpu.SMEM), _const_spec((1, GLA_DV), l),
                      _const_spec((ATT_W, D_MODEL), l), _const_spec((GV_W, D_MODEL), l),
                      _const_spec((D_MODEL, D_MODEL), l), _const_spec((1, D_MODEL), l), _const_spec((1, D_MODEL), l)],
            out_specs=[h_spec,
                       pl.BlockSpec((1, BLOCK, KV_W), lambda t: (seq_of(t), 0, 0)),
                       pl.BlockSpec((1, BLOCK, KV_W), lambda t: (seq_of(t), 0, 0)),
                       pl.BlockSpec((1, GLA_HEADS, GLA_DK, GLA_DV), lambda t: (seq_of(t), 0, 0, 0))],
            out_shape=[jax.ShapeDtypeStruct((bsz, lp, D_MODEL), F32),
                       jax.ShapeDtypeStruct((bsz, BLOCK, KV_W), F32),
                       jax.ShapeDtypeStruct((bsz, BLOCK, KV_W), F32),
                       jax.ShapeDtypeStruct((bsz, GLA_HEADS, GLA_DK, GLA_DV), F32)],
            scratch_shapes=[pltpu.VMEM((N_KV_HEADS, BLOCK, KV_W), BF16), pltpu.VMEM((KV_W, BLOCK), BF16),
                            pltpu.VMEM((GK_W, GV_W), F32), pltpu.VMEM((GK_W, GV_W), F32),
                            pltpu.VMEM((ts, GK_W), F32), pltpu.VMEM((ts, GK_W), F32), pltpu.VMEM((ts, GV_W), F32),
                            pltpu.VMEM((ts, GK_W), F32), pltpu.VMEM((ts, GV_W), F32), pltpu.VMEM((ATT_W, ts), F32),
                            pltpu.VMEM((ts, W_IN_COLS - C_GR), F32), pltpu.VMEM((ts, D_MODEL), F32)],
            compiler_params=cparams(dimension_semantics=("arbitrary",)),
            name=f"mixer_prompt_{l}",
        )(*x_args, cos_p, slo_p, shi_p, w_in_r, w_a2_p, b_a3, attn_sink[l], gn3, w_pa, w_pb, w_o, g1, b1)
        pk.append(kwin.reshape(bsz, N_KV_HEADS, HEAD_DIM, BLOCK))
        pv.append(vwin.reshape(bsz, N_KV_HEADS, HEAD_DIM, BLOCK))
        pst.append(sfin)

        ffn = functools.partial(_ffn_kernel, alpha=alpha, col_chunk=FFN_COL_CHUNK)
        ffn_w = [_const_spec((D_MODEL, D_FF), l), _const_spec((D_FF, D_MODEL), l),
                 _const_spec((1, D_MODEL), l), _const_spec((1, D_MODEL), l)]
        if l < depth - 1:
            n_tiles = rows // ffn_tile
            xp = pl.pallas_call(
                functools.partial(ffn, n_total=n_tiles), grid=(n_tiles + 1,),
                in_specs=[pl.BlockSpec((ffn_tile, D_MODEL), lambda t: (jnp.minimum(t, n_tiles - 1), 0))] + ffn_w,
                out_specs=pl.BlockSpec((ffn_tile, D_MODEL), lambda t: (jnp.maximum(t - 1, 0), 0)),
                out_shape=jax.ShapeDtypeStruct((rows, D_MODEL), F32),
                scratch_shapes=[pltpu.VMEM((ffn_tile, D_MODEL), F32)],
                compiler_params=cparams(dimension_semantics=("arbitrary",)),
                name=f"ffn_prompt_{l}",
            )(hp.reshape(rows, D_MODEL), w_u, w_d, g2, b2).reshape(bsz, lp, D_MODEL)
        else:
            pieces = last_tile // BLOCK
            per_seq = seq // last_tile
            n_tiles = bsz * per_seq
            rd = lambda t: jnp.minimum(t, n_tiles - 1)
            wr = lambda t: jnp.maximum(t - 1, 0)
            y_prompt = pl.pallas_call(
                functools.partial(ffn, n_total=n_tiles), grid=(n_tiles + 1,),
                in_specs=[pl.BlockSpec((1, BLOCK, D_MODEL),
                                       lambda t, j=j: (rd(t) // per_seq, 1 + pieces * (rd(t) % per_seq) + j, 0))
                          for j in range(pieces)] + ffn_w,
                out_specs=pl.BlockSpec((1, last_tile, D_MODEL), lambda t: (wr(t) // per_seq, wr(t) % per_seq, 0)),
                out_shape=jax.ShapeDtypeStruct((bsz, seq, D_MODEL), F32),
                scratch_shapes=[pltpu.VMEM((last_tile, D_MODEL), F32)],
                compiler_params=cparams(dimension_semantics=("arbitrary",)),
                name=f"ffn_prompt_{l}",
            )(*([hp] * pieces), w_u, w_d, g2, b2)

        proj_out = [(nsmp, C_GQ), (nsmp, GV_W), (nsmp, W_IN_COLS - C_GR),
                    (2 * N_SPLIT, KV_W, nsmp), (3 * N_SPLIT, GK_W, nsmp)]
        qkv, gv_s, gate, kvt, gcol = pl.pallas_call(
            _sample_proj_kernel, grid=(1,),
            in_specs=[_const_spec((nsmp, D_MODEL)), _const_spec((nsmp, LANES)), _const_spec((nsmp, LANES)),
                      _const_spec((nsmp, LANES)), _const_spec((W_IN_COLS, D_MODEL), l),
                      _const_spec((LANES, GK_W), l), _const_spec((1, GK_W), l)],
            out_specs=[_const_spec(s) for s in proj_out],
            out_shape=[jax.ShapeDtypeStruct(s, F32) for s in proj_out],
            compiler_params=cparams(dimension_semantics=("arbitrary",)),
            name=f"sample_proj_{l}",
        )(xs, cos_s, slo_s, shi_s, w_in_r, w_a2_p, b_a3)
        def by_group(t):
            t = t.reshape(t.shape[0] // N_SPLIT, N_SPLIT, t.shape[1], nsmp // group, group)
            return jnp.transpose(t, (3, 0, 2, 1, 4)).reshape(nsmp // group, t.shape[0], t.shape[2], N_SPLIT * group)
        col_spec = lambda n, width: pl.BlockSpec((None, n, width, N_SPLIT * group), lambda i: (i, 0, 0, 0))

        grp = lambda width: pl.BlockSpec((group, width), lambda i: (i, 0))
        cache_spec = pl.BlockSpec((None, group, N_KV_HEADS, HEAD_DIM, BLOCK), lambda i: (l, i, 0, 0, 0))
        state_spec = pl.BlockSpec((None, group, GLA_HEADS, GLA_DK, GLA_DV), lambda i: (l, i, 0, 0, 0))
        n_mix_in = 8
        att_s, o_s, *stacked = pl.pallas_call(
            functools.partial(_sample_mix_kernel, group=group), grid=(nsmp // group,),
            in_specs=[grp(C_GQ), grp(GV_W), col_spec(2, KV_W), col_spec(3, GK_W), cache_spec, cache_spec, state_spec,
                      _const_spec((N_Q_HEADS, LANES), l)] + [pl.BlockSpec(memory_space=pl.ANY)] * len(stacked),
            out_specs=[grp(ATT_W), grp(GV_W), cache_spec, cache_spec, state_spec],
            out_shape=[jax.ShapeDtypeStruct((nsmp, ATT_W), F32), jax.ShapeDtypeStruct((nsmp, GV_W), F32),
                       jax.ShapeDtypeStruct(ck.shape, F32), jax.ShapeDtypeStruct(cv.shape, F32),
                       jax.ShapeDtypeStruct(state_gla.shape, F32)],
            input_output_aliases={n_mix_in + i: 2 + i for i in range(len(stacked))},
            compiler_params=cparams(dimension_semantics=("arbitrary",)),
            name=f"sample_mix_{l}",
        )(qkv, gv_s, by_group(kvt), by_group(gcol), ck, cv, state_gla, sink_lanes, *stacked)

        xs = pl.pallas_call(
            functools.partial(_sample_tail_kernel, alpha=alpha, col_chunk=FFN_COL_CHUNK), grid=(1,),
            in_specs=[_const_spec((nsmp, D_MODEL)), _const_spec((nsmp, ATT_W)), _const_spec((nsmp, GV_W)),
                      _const_spec((nsmp, W_IN_COLS - C_GR)), _const_spec((1, GLA_DV), l),
                      _const_spec((ATT_W, D_MODEL), l), _const_spec((GV_W, D_MODEL), l),
                      _const_spec((D_MODEL, D_MODEL), l), _const_spec((1, D_MODEL), l), _const_spec((1, D_MODEL), l)]
            + ffn_w,
            out_specs=_const_spec((nsmp, D_MODEL)),
            out_shape=jax.ShapeDtypeStruct((nsmp, D_MODEL), F32),
            compiler_params=cparams(dimension_semantics=("arbitrary",)),
            name=f"sample_tail_{l}",
        )(xs, att_s, o_s, gate, gn3, w_pa, w_pb, w_o, g1, b1, w_u, w_d, g2, b2)

    y_sample = xs.reshape(nsmp, 1, D_MODEL)
    to_rows = lambda t: jnp.transpose(t, (0, 1, 4, 2, 3))
    return (y_prompt, y_sample, to_rows(jnp.stack(pk)), to_rows(jnp.stack(pv)), jnp.stack(pst),
            to_rows(stacked[0]), to_rows(stacked[1]), stacked[2])
```

```python
import functools

import jax
import jax.numpy as jnp
from jax import lax
from jax.experimental import pallas as pl
from jax.experimental.pallas import tpu as pltpu

F32 = jnp.float32
BF16 = jnp.bfloat16

D_MODEL = 1024
PAST_LEN = 8192
N_META = 16
BLOCK = 128
META_PAD = BLOCK - N_META
HEAD_DIM = 64
N_Q_HEADS = 8
N_KV_HEADS = 2
Q_PER_KV = N_Q_HEADS // N_KV_HEADS
ROT_DIM = HEAD_DIM // 4
ROPE_THETA = 500000.0
GLA_HEADS = 4
GLA_DK = 64
GLA_DV = 128
GLA_RANK = 16
GLA_TAU = 16.0
D_FF = 4 * D_MODEL
ATT_W = N_Q_HEADS * HEAD_DIM
KV_W = N_KV_HEADS * HEAD_DIM
GK_W = GLA_HEADS * GLA_DK
GV_W = GLA_HEADS * GLA_DV
LOG2_E = 1.4426950408889634
LN_EPS = 1e-5
RMS_EPS = 1e-6
LANES = 128
SUBLANES = 8
VMEM_LIMIT = 56 * 1024 * 1024

C_Q = 0
C_K = C_Q + ATT_W
C_V = C_K + KV_W
C_GQ = C_V + KV_W
C_GK = C_GQ + GK_W
C_GV = C_GK + GK_W
C_LR = C_GV + GV_W
C_GR = C_LR + GLA_RANK
C_GA = C_GR + GV_W
C_GB = C_GA + D_MODEL
W_IN_COLS = C_GB + D_MODEL
C_LR_END = C_LR + LANES

GLA_FAST_MAX_DECAY = 40.0


def _sigmoid(x):
    return 1.0 / (1.0 + jnp.exp(-x))


def _layer_norm(y, g, b):
    mu = jnp.mean(y, axis=-1, keepdims=True)
    yc = y - mu
    var = jnp.mean(yc * yc, axis=-1, keepdims=True)
    return yc * lax.rsqrt(var + LN_EPS) * g + b


def _rope(t, cos, sin_lo, sin_hi):
    outs = []
    for j in range(t.shape[1] // LANES):
        tj = t[:, j * LANES:(j + 1) * LANES]
        outs.append(tj * cos + pltpu.roll(tj, LANES - ROT_DIM // 2, 1) * sin_lo
                    + pltpu.roll(tj, ROT_DIM // 2, 1) * sin_hi)
    return outs[0] if len(outs) == 1 else jnp.concatenate(outs, axis=1)


def _in_proj(xb, w_t_ref, lo, hi):
    return lax.dot_general(xb, w_t_ref[lo:hi, :], (((1,), (1,)), ((), ())), preferred_element_type=F32)


def _log_decay(glr, w_a2, b_a):
    z = jnp.dot(glr.astype(BF16), w_a2, preferred_element_type=F32) + b_a
    return (jnp.minimum(z, 0.0) - jnp.log1p(jnp.exp(-jnp.abs(z)))) * (1.0 / GLA_TAU)


def _row_to_col(row):
    n = row.shape[1]
    eye = lax.broadcasted_iota(jnp.int32, (n, n), 0) == lax.broadcasted_iota(jnp.int32, (n, n), 1)
    return jnp.sum(jnp.where(eye, jnp.broadcast_to(row, (n, n)), 0.0), axis=1, keepdims=True)


N_SPLIT = 3


def _split3_bf16(a):
    hi = a.astype(BF16)
    r = a - hi.astype(F32)
    mid = r.astype(BF16)
    lo = (r - mid.astype(F32)).astype(BF16)
    return hi, mid, lo


def _swish(x):
    return x * _sigmoid(x)


def _gla_gate_out(o, swish_gr, gn):
    outs = []
    for h in range(GLA_HEADS):
        oh = o[:, h * GLA_DV:(h + 1) * GLA_DV]
        ms = jnp.mean(oh * oh, axis=-1, keepdims=True)
        outs.append(oh * lax.rsqrt(ms + RMS_EPS) * gn)
    return jnp.concatenate(outs, axis=1) * swish_gr


def _finish(x, pa, gla, sig_a, sig_b, w_pb, w_out, alpha):
    pb = jnp.dot(gla.astype(BF16), w_pb, preferred_element_type=F32)
    m = sig_a * pa + sig_b * pb
    return alpha * x + jnp.dot(m.astype(BF16), w_out, preferred_element_type=F32)


_MIXER_REF_NAMES = ("cos", "slo", "shi", "w_in", "w_a2", "b_a", "sink", "gn", "w_pa", "w_pb", "w_out", "g1", "b1",
                    "h", "kwin", "vwin", "sfin",
                    "kprev_scr", "vprev_scr", "s_scr", "sprev_scr", "gq_scr", "gk_scr", "gv_scr", "la_scr",
                    "o_scr", "att_t_scr", "gate_scr", "y_scr")


def _mixer_prompt_kernel(*refs, ts, n_steps, n_total, alpha, from_tokens):
    t = pl.program_id(0)
    named = dict(zip(_MIXER_REF_NAMES, refs[len(refs) - len(_MIXER_REF_NAMES):]))
    h_ref, y_scr, g1_ref, b1_ref = named["h"], named["y_scr"], named["g1"], named["b1"]

    @pl.when(t == 0)
    def _():
        y_scr[...] = jnp.zeros(y_scr.shape, F32)

    @pl.when(t < n_total)
    def _():
        _mixer_step(lax.rem(t, n_steps), *refs, ts=ts, n_steps=n_steps, alpha=alpha, from_tokens=from_tokens)

    @pl.when(t == n_total)
    def _():
        h_ref[0] = _layer_norm(y_scr[...], g1_ref[...], b1_ref[...])


def _mixer_step(s, *refs, ts, n_steps, alpha, from_tokens):
    nblk = ts // BLOCK
    n_x = 1 + nblk if from_tokens else 1
    x_refs = refs[:n_x]
    (cos_ref, slo_ref, shi_ref, w_in_ref, w_a2_ref, b_a_ref, sink_ref, gn_ref,
     w_pa_ref, w_pb_ref, w_out_ref, g1_ref, b1_ref,
     h_ref, kwin_ref, vwin_ref, sfin_ref,
     kprev_scr, vprev_scr, s_scr, sprev_scr, gq_scr, gk_scr, gv_scr, la_scr, o_scr, att_t_scr, gate_scr,
     y_scr) = refs[n_x:]
    assert len(refs) - n_x == len(_MIXER_REF_NAMES)

    def load_x():
        if not from_tokens:
            return x_refs[0][0]
        blocks = [r[0] for r in x_refs[1:]]
        blocks[0] = jnp.where(s == 0, x_refs[0][...], blocks[0])
        return jnp.concatenate(blocks, axis=0)

    @pl.when(s == 0)
    def _():
        kprev_scr[...] = jnp.zeros((N_KV_HEADS, BLOCK, KV_W), BF16)
        vprev_scr[...] = jnp.zeros((KV_W, BLOCK), BF16)
        s_scr[...] = jnp.zeros(s_scr.shape, F32)

    x = load_x()
    xb = x.astype(BF16)
    proj = functools.partial(_in_proj, xb, w_in_ref)

    mix_in = proj(C_Q, C_LR_END)
    live = (s * ts + lax.broadcasted_iota(jnp.int32, (ts, 1), 0)) >= META_PAD
    cos, slo, shi = cos_ref[...], slo_ref[...], shi_ref[...]
    q = _rope(mix_in[:, C_Q:C_K], cos, slo, shi)
    k = _rope(mix_in[:, C_K:C_V], cos, slo, shi)
    v = mix_in[:, C_V:C_GQ]
    gq = mix_in[:, C_GQ:C_GK] * (GLA_DK ** -0.5)
    gk = jnp.where(live, mix_in[:, C_GK:C_GV], 0.0)
    gv = jnp.where(live, mix_in[:, C_GV:C_LR], 0.0)
    v_t = v.T

    @pl.when(s == n_steps - 1)
    def _():
        kwin_ref[0] = k[ts - BLOCK:, :].T
        vwin_ref[0] = v_t[:, ts - BLOCK:]

    gate_w = W_IN_COLS - C_GR
    piece_cols = 4 * LANES
    n_gate_pieces = min(N_KV_HEADS * nblk, -(-gate_w // piece_cols))
    gate_edges = [min(gate_w, piece_cols * (-(-gate_w // piece_cols) * i // n_gate_pieces))
                  for i in range(n_gate_pieces)] + [gate_w]

    def gate_piece_matmul(i):
        return proj(C_GR + gate_edges[i], C_GR + gate_edges[i + 1])

    def gate_piece_store(i, val):
        lo, hi = gate_edges[i], gate_edges[i + 1]
        mid = min(max(GV_W, lo), hi)
        if mid > lo:
            gate_scr[:, lo:mid] = _swish(val[:, :mid - lo])
        if hi > mid:
            gate_scr[:, mid:hi] = _sigmoid(val[:, mid - lo:])

    tri = (lax.broadcasted_iota(jnp.int32, (BLOCK, BLOCK), 1)
           <= lax.broadcasted_iota(jnp.int32, (BLOCK, BLOCK), 0))
    tri_bf = jnp.where(tri, 1.0, 0.0).astype(BF16)
    head_of_k = lax.broadcasted_iota(jnp.int32, (BLOCK, GK_W), 1) // GLA_DK
    head_of_v = lax.broadcasted_iota(jnp.int32, (BLOCK, GV_W), 1) // GLA_DV
    state_diag = (lax.broadcasted_iota(jnp.int32, (GK_W, GV_W), 0) // GLA_DK
                  == lax.broadcasted_iota(jnp.int32, (GK_W, GV_W), 1) // GLA_DV)
    causal = (lax.broadcasted_iota(jnp.int32, (BLOCK, GLA_HEADS * BLOCK), 1) % BLOCK
              <= lax.broadcasted_iota(jnp.int32, (BLOCK, GLA_HEADS * BLOCK), 0))
    gla = {"cums": [], "chunks": []}

    def issue_decay():
        gla["la"] = jnp.where(live, _log_decay(mix_in[:, C_LR:C_LR_END], w_a2_ref[...], b_a_ref[...]), 0.0)

    def issue_cumsum():
        worst = jnp.zeros((1, GK_W), F32)
        for c in range(nblk):
            parts = jnp.concatenate(_split3_bf16(gla["la"][c * BLOCK:(c + 1) * BLOCK]), axis=1)
            b3 = jnp.dot(tri_bf, parts, preferred_element_type=F32)
            b = b3[:, 0:GK_W] + b3[:, GK_W:2 * GK_W] + b3[:, 2 * GK_W:3 * GK_W]
            gla["cums"].append(b)
            worst = jnp.maximum(worst, -b[BLOCK - 1:BLOCK, :])
        gla["worst"] = worst

    def issue_chunk(c):
        rows = slice(c * BLOCK, (c + 1) * BLOCK)
        b = gla["cums"][c]
        b_last = b[BLOCK - 1:BLOCK, :]
        kc = gk[rows]
        q_dec = (gq[rows] * jnp.exp(b)).astype(BF16)
        k_inv = (kc * jnp.exp(-b)).astype(BF16)
        k_end_t = (kc * jnp.exp(b_last - b)).T.astype(BF16)
        vc = gv[rows].astype(BF16)
        zk = jnp.zeros_like(k_inv)
        k_bd = jnp.concatenate([jnp.where(head_of_k == h, k_inv, zk) for h in range(GLA_HEADS)], axis=0)
        a = lax.dot_general(q_dec, k_bd, (((1,), (1,)), ((), ())), preferred_element_type=F32)
        a = jnp.where(causal, a, 0.0).astype(BF16)
        zv = jnp.zeros_like(vc)
        v_bd = jnp.concatenate([jnp.where(head_of_v == h, vc, zv) for h in range(GLA_HEADS)], axis=0)
        o_intra = jnp.dot(a, v_bd, preferred_element_type=F32)
        ds = jnp.where(state_diag, jnp.dot(k_end_t, vc, preferred_element_type=F32), 0.0)
        gla["chunks"].append((rows, q_dec, o_intra, ds, _row_to_col(jnp.exp(b_last))))

    n_pairs = N_KV_HEADS * nblk
    second = min(1, n_pairs - 1)
    side_work = {}
    for pair, issue in ([(0, issue_decay), (second, issue_cumsum)]
                        + [(max(second, n_pairs - nblk + c), functools.partial(issue_chunk, c)) for c in range(nblk)]):
        side_work.setdefault(pair, []).append(issue)

    q_bf = (q * (HEAD_DIM ** -0.5 * LOG2_E)).astype(BF16)
    low_half = lax.broadcasted_iota(jnp.int32, (1, KV_W), 1) < HEAD_DIM
    k_swapped = pltpu.roll(k, HEAD_DIM, 1)
    k_dup = [jnp.where(low_half, k, k_swapped).astype(BF16), jnp.where(low_half, k_swapped, k).astype(BF16)]
    k_keys = [jnp.concatenate([kprev_scr[i], k_dup[i]], axis=0) for i in range(N_KV_HEADS)]
    vt_bf = v_t.astype(BF16)
    vt_keys = jnp.concatenate([vprev_scr[...], vt_bf], axis=1)
    kj = lax.broadcasted_iota(jnp.int32, (2 * BLOCK, Q_PER_KV * BLOCK), 0)
    qi = lax.broadcasted_iota(jnp.int32, (2 * BLOCK, Q_PER_KV * BLOCK), 1) % BLOCK
    band = (kj - qi >= 1) & (kj - qi <= BLOCK)
    q_low_half = lax.broadcasted_iota(jnp.int32, (BLOCK, LANES), 1) < HEAD_DIM
    pieces_done = 0
    for blk in range(nblk):
        first_key_slot = (s * nblk + blk - 1) * BLOCK
        valid = band & (kj + first_key_slot >= META_PAD)
        r0 = blk * BLOCK
        for kv in range(N_KV_HEADS):
            heads = [kv * Q_PER_KV + g for g in range(Q_PER_KV)]
            q_rows = []
            for hq in heads:
                grp = q_bf[r0:r0 + BLOCK, (hq // 2) * LANES:(hq // 2 + 1) * LANES]
                own = q_low_half if hq % 2 == 0 else jnp.logical_not(q_low_half)
                q_rows.append(jnp.where(own, grp, jnp.zeros_like(grp)))
            st = lax.dot_general(k_keys[kv][r0:r0 + 2 * BLOCK, :], jnp.concatenate(q_rows, axis=0),
                                 (((1,), (1,)), ((), ())), preferred_element_type=F32)
            gate_val = gate_piece_matmul(pieces_done) if pieces_done < n_gate_pieces else None
            for issue in side_work.get(blk * N_KV_HEADS + kv, ()):
                issue()
            st = jnp.where(valid, st, -jnp.inf)
            sink_row = jnp.concatenate([jnp.full((1, BLOCK), sink_ref[hq] * LOG2_E, F32) for hq in heads], axis=1)
            m = jnp.maximum(jnp.max(st, axis=0, keepdims=True), sink_row)
            p = jnp.exp2(st - m)
            den = jnp.sum(p, axis=0, keepdims=True) + jnp.exp2(sink_row - m)
            ot = jnp.dot(vt_keys[kv * HEAD_DIM:(kv + 1) * HEAD_DIM, r0:r0 + 2 * BLOCK], p.astype(BF16),
                         preferred_element_type=F32) * (1.0 / den)
            for g, hq in enumerate(heads):
                att_t_scr[hq * HEAD_DIM:(hq + 1) * HEAD_DIM, r0:r0 + BLOCK] = ot[:, g * BLOCK:(g + 1) * BLOCK]
            if gate_val is not None:
                gate_piece_store(pieces_done, gate_val)
                pieces_done += 1
    for i in range(pieces_done, n_gate_pieces):
        gate_piece_store(i, gate_piece_matmul(i))
    for i in range(N_KV_HEADS):
        kprev_scr[i] = k_dup[i][ts - BLOCK:, :]
    vprev_scr[...] = vt_bf[:, ts - BLOCK:]

    fast_ok = jnp.max(gla["worst"]) <= GLA_FAST_MAX_DECAY

    def proj_att():
        return jnp.dot(att_t_scr[...].T.astype(BF16), w_pa_ref[...], preferred_element_type=F32)

    def gla_fast():
        pa = proj_att()
        for rows, q_dec, o_intra, ds, decay_col in gla["chunks"]:
            s0 = s_scr[...]
            o_scr[rows, :] = o_intra + jnp.dot(q_dec, s0.astype(BF16), preferred_element_type=F32)
            s_scr[...] = decay_col * s0 + ds
        return pa

    def gla_slow():
        def body(i, carry):
            rows = pl.ds(pl.multiple_of(i * SUBLANES, SUBLANES), SUBLANES)
            la8, k8, q8, v8 = la_scr[rows, :], gk_scr[rows, :], gq_scr[rows, :], gv_scr[rows, :]
            outs = []
            for r in range(SUBLANES):
                a_col = _row_to_col(jnp.exp(la8[r:r + 1]))
                k_col = _row_to_col(k8[r:r + 1])
                q_col = _row_to_col(q8[r:r + 1])
                s1 = a_col * s_scr[...] + jnp.where(state_diag, k_col * v8[r:r + 1], 0.0)
                s_scr[...] = s1
                outs.append(jnp.sum(q_col * s1, axis=0, keepdims=True))
            o_scr[rows, :] = jnp.concatenate(outs, axis=0)
            return carry
        lax.fori_loop(0, ts // SUBLANES, body, 0)

    def finish_step(x_val, pa):
        gla = _gla_gate_out(o_scr[...], gate_scr[:, 0:GV_W], gn_ref[...])
        y_scr[...] = _finish(x_val, pa, gla, gate_scr[:, GV_W:GV_W + D_MODEL], gate_scr[:, GV_W + D_MODEL:],
                             w_pb_ref[...], w_out_ref[...], alpha)

    sprev_scr[...] = s_scr[...]
    pa = gla_fast()
    h_ref[0] = _layer_norm(y_scr[...], g1_ref[...], b1_ref[...])
    finish_step(x, pa)

    @pl.when(jnp.logical_not(fast_ok))
    def _():
        x_again = load_x()
        proj_again = functools.partial(_in_proj, x_again.astype(BF16), w_in_ref)
        gq_scr[...] = proj_again(C_GQ, C_GK) * (GLA_DK ** -0.5)
        gk_scr[...] = jnp.where(live, proj_again(C_GK, C_GV), 0.0)
        gv_scr[...] = jnp.where(live, proj_again(C_GV, C_LR), 0.0)
        la_scr[...] = jnp.where(live, _log_decay(proj_again(C_LR, C_LR_END), w_a2_ref[...], b_a_ref[...]), 0.0)
        s_scr[...] = sprev_scr[...]
        gla_slow()
        finish_step(x_again, proj_att())

    @pl.when(s == n_steps - 1)
    def _():
        for h in range(GLA_HEADS):
            sfin_ref[0, h] = s_scr[h * GLA_DK:(h + 1) * GLA_DK, h * GLA_DV:(h + 1) * GLA_DV]


def _ffn_update(h, w_up_ref, w_dn_ref, col_chunk, after_first_chunk=None):
    hb = h.astype(BF16)
    acc = jnp.zeros(h.shape, F32)
    for c in range(D_FF // col_chunk):
        u = jnp.dot(hb, w_up_ref[:, c * col_chunk:(c + 1) * col_chunk], preferred_element_type=F32)
        u = jnp.maximum(u, 0.0)
        acc = acc + jnp.dot((u * u).astype(BF16), w_dn_ref[c * col_chunk:(c + 1) * col_chunk, :],
                            preferred_element_type=F32)
        if c == 0 and after_first_chunk is not None:
            after_first_chunk()
    return acc


def _ffn_kernel(*refs, alpha, col_chunk, n_total):
    h_refs, (w_up_ref, w_dn_ref, g_ref, b_ref, o_ref, y_scr) = refs[:-6], refs[-6:]

    def write_norm():
        o_ref[...] = _layer_norm(y_scr[...], g_ref[...], b_ref[...]).reshape(o_ref.shape)

    def tile():
        blocks = [r[0] if len(r.shape) == 3 else r[...] for r in h_refs]
        h = blocks[0] if len(blocks) == 1 else jnp.concatenate(blocks, axis=0)
        acc = _ffn_update(h, w_up_ref, w_dn_ref, col_chunk, after_first_chunk=write_norm)
        y_scr[...] = alpha * h + acc

    t = pl.program_id(0)

    @pl.when(t == 0)
    def _():
        y_scr[...] = jnp.zeros(y_scr.shape, F32)

    pl.when(t < n_total)(tile)
    pl.when(t == n_total)(write_norm)


def _sample_proj_kernel(x_ref, cos_ref, slo_ref, shi_ref, w_in_ref, w_a2_ref, b_a_ref,
                        qkv_ref, gv_ref, gate_ref, kvt_ref, gcol_ref):
    proj = functools.partial(_in_proj, x_ref[...].astype(BF16), w_in_ref)

    cos, slo, shi = cos_ref[...], slo_ref[...], shi_ref[...]
    k = _rope(proj(C_K, C_V), cos, slo, shi)
    v = proj(C_V, C_GQ)
    qkv_ref[:, C_Q:C_K] = _rope(proj(C_Q, C_K), cos, slo, shi)
    qkv_ref[:, C_K:C_V] = k
    qkv_ref[:, C_V:C_GQ] = v
    gv_ref[...] = proj(C_GV, C_LR)
    gate_ref[...] = proj(C_GR, W_IN_COLS)
    def store_planes(ref, i, t):
        for p, part in enumerate(_split3_bf16(t)):
            ref[N_SPLIT * i + p] = part.astype(F32).T

    store_planes(kvt_ref, 0, k)
    store_planes(kvt_ref, 1, v)
    store_planes(gcol_ref, 0, jnp.exp(_log_decay(proj(C_LR, C_LR_END), w_a2_ref[...], b_a_ref[...])))
    store_planes(gcol_ref, 1, proj(C_GQ, C_GK) * (GLA_DK ** -0.5))
    store_planes(gcol_ref, 2, proj(C_GK, C_GV))


def _sample_mix_kernel(qkv_ref, gv_ref, kvt_ref, gcol_ref, ck_ref, cv_ref, st_ref, sink_ref, *rest, group):
    att_ref, o_ref, nk_ref, nv_ref, nst_ref = rest[-5:]
    head_row = lax.broadcasted_iota(jnp.int32, (N_Q_HEADS, ATT_W), 0)
    head_lane = lax.broadcasted_iota(jnp.int32, (N_Q_HEADS, ATT_W), 1) // HEAD_DIM
    own = head_row == head_lane
    r8 = lax.broadcasted_iota(jnp.int32, (N_Q_HEADS, KV_W), 0)
    swap = (r8 % 2) != (r8 // Q_PER_KV)
    key_i = lax.broadcasted_iota(jnp.int32, (N_Q_HEADS, BLOCK), 1)
    last_row = lax.broadcasted_iota(jnp.int32, (KV_W, BLOCK), 1) == BLOCK - 1
    sink = sink_ref[...][:, 0:1]

    qkv8 = qkv_ref[...]
    gv8 = gv_ref[...]
    q8s, scores = [], []
    for j in range(group):
        q_row = qkv8[j:j + 1, C_Q:C_K]
        qm = jnp.where(own, jnp.broadcast_to(q_row, (N_Q_HEADS, ATT_W)), 0.0)
        fold = qm[:, 0:128] + qm[:, 128:256] + qm[:, 256:384] + qm[:, 384:512]
        q8 = (jnp.where(swap, pltpu.roll(fold, HEAD_DIM, 1), fold) * (HEAD_DIM ** -0.5)).astype(BF16)
        q8s.append(q8)
        scores.append(jnp.dot(q8, ck_ref[j].reshape(KV_W, BLOCK).astype(BF16), preferred_element_type=F32))

    sel = (lax.broadcasted_iota(jnp.int32, (N_SPLIT * group, group * LANES), 0) % group
           == lax.broadcasted_iota(jnp.int32, (N_SPLIT * group, group * LANES), 1) // LANES)
    sel = jnp.where(sel, 1.0, 0.0).astype(BF16)
    spread = lambda ref, i: jnp.dot(ref[i].astype(BF16), sel, preferred_element_type=F32)
    k_cols, v_cols = spread(kvt_ref, 0), spread(kvt_ref, 1)
    a_cols, q_cols, k_gla_cols = (spread(gcol_ref, i) for i in range(3))

    probs = []
    for j in range(group):
        k_new = qkv8[j:j + 1, C_K:C_V]
        sc = jnp.where(key_i >= 1, scores[j], -jnp.inf)
        s_new = jnp.sum(q8s[j].astype(F32) * k_new.astype(BF16).astype(F32), axis=-1, keepdims=True)
        m = jnp.maximum(jnp.maximum(jnp.max(sc, axis=-1, keepdims=True), s_new), sink)
        p = jnp.exp(sc - m)
        p_new = jnp.exp(s_new - m)
        den = jnp.sum(p, axis=-1, keepdims=True) + p_new + jnp.exp(sink - m)
        probs.append((p.astype(BF16), p_new, den))

    att_rows, o_rows = [], []
    for j in range(group):
        p_bf, p_new, den = probs[j]
        v_new = qkv8[j:j + 1, C_V:C_GQ]
        o8 = (lax.dot_general(p_bf, cv_ref[j].reshape(KV_W, BLOCK).astype(BF16), (((1,), (1,)), ((), ())),
                              preferred_element_type=F32)
              + p_new.astype(BF16).astype(F32) * v_new.astype(BF16).astype(F32)) / den
        o8 = jnp.where(swap, pltpu.roll(o8, HEAD_DIM, 1), o8)
        o_wide = jnp.concatenate([o8, o8, o8, o8], axis=1)
        att_rows.append(jnp.sum(jnp.where(own, o_wide, 0.0), axis=0, keepdims=True))

    for j in range(group):
        lanes_j = slice(j * LANES, (j + 1) * LANES)
        k_old = ck_ref[j].reshape(KV_W, BLOCK)
        v_old = cv_ref[j].reshape(KV_W, BLOCK)
        nk_ref[j] = jnp.where(last_row, k_cols[:, lanes_j], pltpu.roll(k_old, BLOCK - 1, 1)).reshape(
            N_KV_HEADS, HEAD_DIM, BLOCK)
        nv_ref[j] = jnp.where(last_row, v_cols[:, lanes_j], pltpu.roll(v_old, BLOCK - 1, 1)).reshape(
            N_KV_HEADS, HEAD_DIM, BLOCK)
        a_col, q_col, k_col = a_cols[:, lanes_j], q_cols[:, lanes_j], k_gla_cols[:, lanes_j]
        v_row = gv8[j:j + 1, :]
        v_exp = jnp.concatenate([jnp.broadcast_to(v_row[:, h * GLA_DV:(h + 1) * GLA_DV], (GLA_DK, GLA_DV))
                                 for h in range(GLA_HEADS)], axis=0)
        s1 = a_col * st_ref[j].reshape(GK_W, GLA_DV) + k_col * v_exp
        nst_ref[j] = s1.reshape(GLA_HEADS, GLA_DK, GLA_DV)
        qs = q_col * s1
        o_rows.append(jnp.concatenate(
            [jnp.sum(qs[h * GLA_DK:(h + 1) * GLA_DK], axis=0, keepdims=True) for h in range(GLA_HEADS)], axis=1))
    att_ref[...] = jnp.concatenate(att_rows, axis=0)
    o_ref[...] = jnp.concatenate(o_rows, axis=0)


def _sample_tail_kernel(x_ref, att_ref, o_ref, gate_ref, gn_ref, w_pa_ref, w_pb_ref, w_out_ref, g1_ref, b1_ref,
                        w_up_ref, w_dn_ref, g2_ref, b2_ref, out_ref, h_scr, acc_scr, *, alpha):
    c = pl.program_id(0)

    @pl.when(c == 0)
    def _():
        gla = _gla_gate_out(o_ref[...], _swish(gate_ref[:, 0:GV_W]), gn_ref[...])
        pa = jnp.dot(att_ref[...].astype(BF16), w_pa_ref[...], preferred_element_type=F32)
        y = _finish(x_ref[...], pa, gla, _sigmoid(gate_ref[:, GV_W:GV_W + D_MODEL]),
                    _sigmoid(gate_ref[:, GV_W + D_MODEL:]), w_pb_ref[...], w_out_ref[...], alpha)
        h_scr[...] = _layer_norm(y, g1_ref[...], b1_ref[...])
        acc_scr[...] = jnp.zeros(acc_scr.shape, F32)

    h = h_scr[...]
    u = jnp.maximum(jnp.dot(h.astype(BF16), w_up_ref[...], preferred_element_type=F32), 0.0)
    acc_scr[...] += jnp.dot((u * u).astype(BF16), w_dn_ref[...], preferred_element_type=F32)

    @pl.when(c == pl.num_programs(0) - 1)
    def _():
        out_ref[...] = _layer_norm(alpha * h + acc_scr[...], g2_ref[...], b2_ref[...])


def _rope_tables(pos):
    half = ROT_DIM // 2
    inv = ROPE_THETA ** (-jnp.arange(half, dtype=F32) * 2.0 / ROT_DIM)
    d = jnp.arange(LANES) % HEAD_DIM
    ang = pos.astype(F32)[:, None] * inv[d % half][None, :]
    cos, sin = jnp.cos(ang), jnp.sin(ang)
    cos_t = jnp.where(d < ROT_DIM, cos, 1.0)
    sin_lo = jnp.where(d < half, -sin, 0.0)
    sin_hi = jnp.where((d >= half) & (d < ROT_DIM), sin, 0.0)
    return cos_t, sin_lo, sin_hi


def _const_spec(shape, layer=None):
    if layer is None:
        return pl.BlockSpec(shape, lambda *_: (0,) * len(shape), pipeline_mode=pl.Buffered(1))
    return pl.BlockSpec((None,) + shape, lambda *_: (layer,) + (0,) * len(shape), pipeline_mode=pl.Buffered(1))


MIXER_STEP_ROWS = (3 * BLOCK, 2 * BLOCK, BLOCK)
FFN_TILE_ROWS = (6 * BLOCK, 4 * BLOCK, 3 * BLOCK, 2 * BLOCK, BLOCK)
FFN_COL_CHUNK = D_MODEL


def _first_divisor(total, candidates):
    for t in candidates:
        if total % t == 0:
            return t
    raise ValueError(f"{total} rows are not a multiple of {candidates[-1]}")


def kernel(x_prompt, x_sample, cache_k_win, cache_v_win, state_gla, meta_tokens, w_in, w_a2, b_a, attn_sink,
           gla_norm_g, w_proj_a, w_proj_b, w_out, ln1_g, ln1_b, w_up, w_down, ln2_g, ln2_b):
    depth = w_in.shape[0]
    bsz, seq, _ = x_prompt.shape
    nsmp, dec_seq, _ = x_sample.shape
    assert dec_seq == 1 and cache_k_win.shape[2] == BLOCK and seq % BLOCK == 0
    alpha = (2 * depth) ** 0.25
    lp = seq + BLOCK
    ts = _first_divisor(lp, MIXER_STEP_ROWS)
    n_steps = lp // ts
    rows = bsz * lp
    ffn_tile = _first_divisor(rows, FFN_TILE_ROWS)
    last_tile = _first_divisor(seq, FFN_TILE_ROWS)
    group = next(g for g in (2 * SUBLANES, SUBLANES) if nsmp % g == 0)

    w_in_r = jnp.swapaxes(w_in, 1, 2).astype(BF16)
    w_a2_p = jnp.concatenate([w_a2, jnp.zeros((depth, LANES - GLA_RANK, GK_W), w_a2.dtype)], axis=1).astype(BF16)
    w_pa, w_pb, w_o = w_proj_a.astype(BF16), w_proj_b.astype(BF16), w_out.astype(BF16)
    w_u, w_d = w_up.astype(BF16), w_down.astype(BF16)
    b_a3 = b_a.reshape(depth, 1, GK_W)
    gn3 = gla_norm_g.reshape(depth, 1, GLA_DV)
    g1, b1 = ln1_g.reshape(depth, 1, D_MODEL), ln1_b.reshape(depth, 1, D_MODEL)
    g2, b2 = ln2_g.reshape(depth, 1, D_MODEL), ln2_b.reshape(depth, 1, D_MODEL)
    sink_lanes = jnp.broadcast_to(attn_sink[:, :, None], (depth, N_Q_HEADS, LANES))

    cos_p, slo_p, shi_p = _rope_tables(jnp.arange(lp) - META_PAD)
    cos_s, slo_s, shi_s = (jnp.broadcast_to(t, (nsmp, LANES)) for t in _rope_tables(PAST_LEN + jnp.arange(1)))

    meta_block = jnp.concatenate([jnp.zeros((META_PAD, D_MODEL), x_prompt.dtype),
                                  meta_tokens.astype(x_prompt.dtype)], axis=0)
    nblk = ts // BLOCK
    xp = None
    xs = x_sample.reshape(nsmp, D_MODEL)
    ck = jnp.transpose(cache_k_win, (0, 1, 3, 4, 2))
    cv = jnp.transpose(cache_v_win, (0, 1, 3, 4, 2))

    cparams = functools.partial(pltpu.CompilerParams, vmem_limit_bytes=VMEM_LIMIT)
    pk, pv, pst = [], [], []
    stacked = []
    for l in range(depth):
        n_total = bsz * n_steps
        seq_of = lambda t: jnp.minimum(t, n_total - 1) // n_steps
        step_of = lambda t: jnp.minimum(t, n_total - 1) % n_steps
        step_spec = pl.BlockSpec((1, ts, D_MODEL), lambda t: (seq_of(t), step_of(t), 0))
        h_spec = pl.BlockSpec((1, ts, D_MODEL), lambda t: (seq_of(jnp.maximum(t - 1, 0)), step_of(jnp.maximum(t - 1, 0)), 0))
        tab_spec = pl.BlockSpec((ts, LANES), lambda t: (step_of(t), 0))
        if l == 0:
            x_specs = [_const_spec((BLOCK, D_MODEL))] + [
                pl.BlockSpec((1, BLOCK, D_MODEL),
                             lambda t, j=j: (seq_of(t), jnp.maximum(nblk * step_of(t) + j - 1, 0), 0))
                for j in range(nblk)]
            x_args = [meta_block] + [x_prompt] * nblk
        else:
            x_specs, x_args = [step_spec], [xp]
        hp, kwin, vwin, sfin = pl.pallas_call(
            functools.partial(_mixer_prompt_kernel, ts=ts, n_steps=n_steps, n_total=n_total, alpha=alpha,
                              from_tokens=(l == 0)),
            grid=(n_total + 1,),
            in_specs=x_specs + [tab_spec, tab_spec, tab_spec,
                      _const_spec((W_IN_COLS, D_MODEL), l), _const_spec((LANES, GK_W), l), _const_spec((1, GK_W), l),
                      pl.BlockSpec(memory_space=pltpu.SMEM), _const_spec((1, GLA_DV), l),
                      _const_spec((ATT_W, D_MODEL), l), _const_spec((GV_W, D_MODEL), l),
                      _const_spec((D_MODEL, D_MODEL), l), _const_spec((1, D_MODEL), l), _const_spec((1, D_MODEL), l)],
            out_specs=[h_spec,
                       pl.BlockSpec((1, BLOCK, KV_W), lambda t: (seq_of(t), 0, 0)),
                       pl.BlockSpec((1, BLOCK, KV_W), lambda t: (seq_of(t), 0, 0)),
                       pl.BlockSpec((1, GLA_HEADS, GLA_DK, GLA_DV), lambda t: (seq_of(t), 0, 0, 0))],
            out_shape=[jax.ShapeDtypeStruct((bsz, lp, D_MODEL), F32),
                       jax.ShapeDtypeStruct((bsz, BLOCK, KV_W), F32),
                       jax.ShapeDtypeStruct((bsz, BLOCK, KV_W), F32),
                       jax.ShapeDtypeStruct((bsz, GLA_HEADS, GLA_DK, GLA_DV), F32)],
            scratch_shapes=[pltpu.VMEM((N_KV_HEADS, BLOCK, KV_W), BF16), pltpu.VMEM((KV_W, BLOCK), BF16),
                            pltpu.VMEM((GK_W, GV_W), F32), pltpu.VMEM((GK_W, GV_W), F32),
                            pltpu.VMEM((ts, GK_W), F32), pltpu.VMEM((ts, GK_W), F32), pltpu.VMEM((ts, GV_W), F32),
                            pltpu.VMEM((ts, GK_W), F32), pltpu.VMEM((ts, GV_W), F32), pltpu.VMEM((ATT_W, ts), F32),
                            pltpu.VMEM((ts, W_IN_COLS - C_GR), F32), pltpu.VMEM((ts, D_MODEL), F32)],
            compiler_params=cparams(dimension_semantics=("arbitrary",)),
            name=f"mixer_prompt_{l}",
        )(*x_args, cos_p, slo_p, shi_p, w_in_r, w_a2_p, b_a3, attn_sink[l], gn3, w_pa, w_pb, w_o, g1, b1)
        pk.append(kwin.reshape(bsz, N_KV_HEADS, HEAD_DIM, BLOCK))
        pv.append(vwin.reshape(bsz, N_KV_HEADS, HEAD_DIM, BLOCK))
        pst.append(sfin)

        ffn = functools.partial(_ffn_kernel, alpha=alpha, col_chunk=FFN_COL_CHUNK)
        ffn_w = [_const_spec((D_MODEL, D_FF), l), _const_spec((D_FF, D_MODEL), l),
                 _const_spec((1, D_MODEL), l), _const_spec((1, D_MODEL), l)]
        if l < depth - 1:
            n_tiles = rows // ffn_tile
            xp = pl.pallas_call(
                functools.partial(ffn, n_total=n_tiles), grid=(n_tiles + 1,),
                in_specs=[pl.BlockSpec((ffn_tile, D_MODEL), lambda t: (jnp.minimum(t, n_tiles - 1), 0))] + ffn_w,
                out_specs=pl.BlockSpec((ffn_tile, D_MODEL), lambda t: (jnp.maximum(t - 1, 0), 0)),
                out_shape=jax.ShapeDtypeStruct((rows, D_MODEL), F32),
                scratch_shapes=[pltpu.VMEM((ffn_tile, D_MODEL), F32)],
                compiler_params=cparams(dimension_semantics=("arbitrary",)),
                name=f"ffn_prompt_{l}",
            )(hp.reshape(rows, D_MODEL), w_u, w_d, g2, b2).reshape(bsz, lp, D_MODEL)
        else:
            pieces = last_tile // BLOCK
            per_seq = seq // last_tile
            n_tiles = bsz * per_seq
            rd = lambda t: jnp.minimum(t, n_tiles - 1)
            wr = lambda t: jnp.maximum(t - 1, 0)
            y_prompt = pl.pallas_call(
                functools.partial(ffn, n_total=n_tiles), grid=(n_tiles + 1,),
                in_specs=[pl.BlockSpec((1, BLOCK, D_MODEL),
                                       lambda t, j=j: (rd(t) // per_seq, 1 + pieces * (rd(t) % per_seq) + j, 0))
                          for j in range(pieces)] + ffn_w,
                out_specs=pl.BlockSpec((1, last_tile, D_MODEL), lambda t: (wr(t) // per_seq, wr(t) % per_seq, 0)),
                out_shape=jax.ShapeDtypeStruct((bsz, seq, D_MODEL), F32),
                scratch_shapes=[pltpu.VMEM((last_tile, D_MODEL), F32)],
                compiler_params=cparams(dimension_semantics=("arbitrary",)),
                name=f"ffn_prompt_{l}",
            )(*([hp] * pieces), w_u, w_d, g2, b2)

        proj_out = [(nsmp, C_GQ), (nsmp, GV_W), (nsmp, W_IN_COLS - C_GR),
                    (2 * N_SPLIT, KV_W, nsmp), (3 * N_SPLIT, GK_W, nsmp)]
        qkv, gv_s, gate, kvt, gcol = pl.pallas_call(
            _sample_proj_kernel, grid=(1,),
            in_specs=[_const_spec((nsmp, D_MODEL)), _const_spec((nsmp, LANES)), _const_spec((nsmp, LANES)),
                      _const_spec((nsmp, LANES)), _const_spec((W_IN_COLS, D_MODEL), l),
                      _const_spec((LANES, GK_W), l), _const_spec((1, GK_W), l)],
            out_specs=[_const_spec(s) for s in proj_out],
            out_shape=[jax.ShapeDtypeStruct(s, F32) for s in proj_out],
            compiler_params=cparams(dimension_semantics=("arbitrary",)),
            name=f"sample_proj_{l}",
        )(xs, cos_s, slo_s, shi_s, w_in_r, w_a2_p, b_a3)
        def by_group(t):
            t = t.reshape(t.shape[0] // N_SPLIT, N_SPLIT, t.shape[1], nsmp // group, group)
            return jnp.transpose(t, (3, 0, 2, 1, 4)).reshape(nsmp // group, t.shape[0], t.shape[2], N_SPLIT * group)
        col_spec = lambda n, width: pl.BlockSpec((None, n, width, N_SPLIT * group), lambda i: (i, 0, 0, 0))

        grp = lambda width: pl.BlockSpec((group, width), lambda i: (i, 0))
        cache_spec = pl.BlockSpec((None, group, N_KV_HEADS, HEAD_DIM, BLOCK), lambda i: (l, i, 0, 0, 0))
        state_spec = pl.BlockSpec((None, group, GLA_HEADS, GLA_DK, GLA_DV), lambda i: (l, i, 0, 0, 0))
        n_mix_in = 8
        att_s, o_s, *stacked = pl.pallas_call(
            functools.partial(_sample_mix_kernel, group=group), grid=(nsmp // group,),
            in_specs=[grp(C_GQ), grp(GV_W), col_spec(2, KV_W), col_spec(3, GK_W), cache_spec, cache_spec, state_spec,
                      _const_spec((N_Q_HEADS, LANES), l)] + [pl.BlockSpec(memory_space=pl.ANY)] * len(stacked),
            out_specs=[grp(ATT_W), grp(GV_W), cache_spec, cache_spec, state_spec],
            out_shape=[jax.ShapeDtypeStruct((nsmp, ATT_W), F32), jax.ShapeDtypeStruct((nsmp, GV_W), F32),
                       jax.ShapeDtypeStruct(ck.shape, F32), jax.ShapeDtypeStruct(cv.shape, F32),
                       jax.ShapeDtypeStruct(state_gla.shape, F32)],
            input_output_aliases={n_mix_in + i: 2 + i for i in range(len(stacked))},
            compiler_params=cparams(dimension_semantics=("arbitrary",)),
            name=f"sample_mix_{l}",
        )(qkv, gv_s, by_group(kvt), by_group(gcol), ck, cv, state_gla, sink_lanes, *stacked)

        xs = pl.pallas_call(
            functools.partial(_sample_tail_kernel, alpha=alpha), grid=(D_FF // FFN_COL_CHUNK,),
            in_specs=[_const_spec((nsmp, D_MODEL)), _const_spec((nsmp, ATT_W)), _const_spec((nsmp, GV_W)),
                      _const_spec((nsmp, W_IN_COLS - C_GR)), _const_spec((1, GLA_DV), l),
                      _const_spec((ATT_W, D_MODEL), l), _const_spec((GV_W, D_MODEL), l),
                      _const_spec((D_MODEL, D_MODEL), l), _const_spec((1, D_MODEL), l), _const_spec((1, D_MODEL), l),
                      pl.BlockSpec((None, D_MODEL, FFN_COL_CHUNK), lambda c: (l, 0, c)),
                      pl.BlockSpec((None, FFN_COL_CHUNK, D_MODEL), lambda c: (l, c, 0)),
                      _const_spec((1, D_MODEL), l), _const_spec((1, D_MODEL), l)],
            out_specs=_const_spec((nsmp, D_MODEL)),
            out_shape=jax.ShapeDtypeStruct((nsmp, D_MODEL), F32),
            scratch_shapes=[pltpu.VMEM((nsmp, D_MODEL), F32), pltpu.VMEM((nsmp, D_MODEL), F32)],
            compiler_params=cparams(dimension_semantics=("arbitrary",)),
            name=f"sample_tail_{l}",
        )(xs, att_s, o_s, gate, gn3, w_pa, w_pb, w_o, g1, b1, w_u, w_d, g2, b2)

    y_sample = xs.reshape(nsmp, 1, D_MODEL)
    to_rows = lambda t: jnp.transpose(t, (0, 1, 4, 2, 3))
    return (y_prompt, y_sample, to_rows(jnp.stack(pk)), to_rows(jnp.stack(pv)), jnp.stack(pst),
            to_rows(stacked[0]), to_rows(stacked[1]), stacked[2])
```

```python
import functools

import jax
import jax.numpy as jnp
from jax import lax
from jax.experimental import pallas as pl
from jax.experimental.pallas import tpu as pltpu

F32 = jnp.float32
BF16 = jnp.bfloat16

D_MODEL = 1024
PAST_LEN = 8192
N_META = 16
BLOCK = 128
META_PAD = BLOCK - N_META
HEAD_DIM = 64
N_Q_HEADS = 8
N_KV_HEADS = 2
Q_PER_KV = N_Q_HEADS // N_KV_HEADS
ROT_DIM = HEAD_DIM // 4
ROPE_THETA = 500000.0
GLA_HEADS = 4
GLA_DK = 64
GLA_DV = 128
GLA_RANK = 16
GLA_TAU = 16.0
D_FF = 4 * D_MODEL
ATT_W = N_Q_HEADS * HEAD_DIM
KV_W = N_KV_HEADS * HEAD_DIM
GK_W = GLA_HEADS * GLA_DK
GV_W = GLA_HEADS * GLA_DV
LOG2_E = 1.4426950408889634
LN_EPS = 1e-5
RMS_EPS = 1e-6
LANES = 128
SUBLANES = 8
VMEM_LIMIT = 56 * 1024 * 1024

C_Q = 0
C_K = C_Q + ATT_W
C_V = C_K + KV_W
C_GQ = C_V + KV_W
C_GK = C_GQ + GK_W
C_GV = C_GK + GK_W
C_LR = C_GV + GV_W
C_GR = C_LR + GLA_RANK
C_GA = C_GR + GV_W
C_GB = C_GA + D_MODEL
W_IN_COLS = C_GB + D_MODEL
C_LR_END = C_LR + LANES

GLA_FAST_MAX_DECAY = 40.0


def _sigmoid(x):
    return 1.0 / (1.0 + jnp.exp(-x))


def _layer_norm(y, g, b):
    mu = jnp.mean(y, axis=-1, keepdims=True)
    yc = y - mu
    var = jnp.mean(yc * yc, axis=-1, keepdims=True)
    return yc * lax.rsqrt(var + LN_EPS) * g + b


def _rope(t, cos, sin_lo, sin_hi):
    outs = []
    for j in range(t.shape[1] // LANES):
        tj = t[:, j * LANES:(j + 1) * LANES]
        outs.append(tj * cos + pltpu.roll(tj, LANES - ROT_DIM // 2, 1) * sin_lo
                    + pltpu.roll(tj, ROT_DIM // 2, 1) * sin_hi)
    return outs[0] if len(outs) == 1 else jnp.concatenate(outs, axis=1)


def _in_proj(xb, w_t_ref, lo, hi):
    return lax.dot_general(xb, w_t_ref[lo:hi, :], (((1,), (1,)), ((), ())), preferred_element_type=F32)


def _log_decay(glr, w_a2, b_a):
    z = jnp.dot(glr.astype(BF16), w_a2, preferred_element_type=F32) + b_a
    return (jnp.minimum(z, 0.0) - jnp.log1p(jnp.exp(-jnp.abs(z)))) * (1.0 / GLA_TAU)


def _row_to_col(row):
    n = row.shape[1]
    eye = lax.broadcasted_iota(jnp.int32, (n, n), 0) == lax.broadcasted_iota(jnp.int32, (n, n), 1)
    return jnp.sum(jnp.where(eye, jnp.broadcast_to(row, (n, n)), 0.0), axis=1, keepdims=True)


N_SPLIT = 3


def _split3_bf16(a):
    hi = a.astype(BF16)
    r = a - hi.astype(F32)
    mid = r.astype(BF16)
    lo = (r - mid.astype(F32)).astype(BF16)
    return hi, mid, lo


def _swish(x):
    return x * _sigmoid(x)


def _gla_gate_out(o, swish_gr, gn):
    outs = []
    for h in range(GLA_HEADS):
        oh = o[:, h * GLA_DV:(h + 1) * GLA_DV]
        ms = jnp.mean(oh * oh, axis=-1, keepdims=True)
        outs.append(oh * lax.rsqrt(ms + RMS_EPS) * gn)
    return jnp.concatenate(outs, axis=1) * swish_gr


def _finish(x, pa, gla, sig_a, sig_b, w_pb, w_out, alpha):
    pb = jnp.dot(gla.astype(BF16), w_pb, preferred_element_type=F32)
    m = sig_a * pa + sig_b * pb
    return alpha * x + jnp.dot(m.astype(BF16), w_out, preferred_element_type=F32)


_MIXER_REF_NAMES = ("cos", "slo", "shi", "w_in", "w_a2", "b_a", "sink", "gn", "w_pa", "w_pb", "w_out", "g1", "b1",
                    "h", "kwin", "vwin", "sfin",
                    "kprev_scr", "vprev_scr", "s_scr", "sprev_scr", "gq_scr", "gk_scr", "gv_scr", "la_scr",
                    "o_scr", "att_t_scr", "gate_scr", "y_scr")


def _mixer_prompt_kernel(*refs, ts, n_steps, n_total, alpha, from_tokens):
    t = pl.program_id(0)
    named = dict(zip(_MIXER_REF_NAMES, refs[len(refs) - len(_MIXER_REF_NAMES):]))
    h_ref, y_scr, g1_ref, b1_ref = named["h"], named["y_scr"], named["g1"], named["b1"]

    @pl.when(t == 0)
    def _():
        y_scr[...] = jnp.zeros(y_scr.shape, F32)

    @pl.when(t < n_total)
    def _():
        _mixer_step(lax.rem(t, n_steps), *refs, ts=ts, n_steps=n_steps, alpha=alpha, from_tokens=from_tokens)

    @pl.when(t == n_total)
    def _():
        h_ref[0] = _layer_norm(y_scr[...], g1_ref[...], b1_ref[...])


def _mixer_step(s, *refs, ts, n_steps, alpha, from_tokens):
    nblk = ts // BLOCK
    n_x = 1 + nblk if from_tokens else 1
    x_refs = refs[:n_x]
    (cos_ref, slo_ref, shi_ref, w_in_ref, w_a2_ref, b_a_ref, sink_ref, gn_ref,
     w_pa_ref, w_pb_ref, w_out_ref, g1_ref, b1_ref,
     h_ref, kwin_ref, vwin_ref, sfin_ref,
     kprev_scr, vprev_scr, s_scr, sprev_scr, gq_scr, gk_scr, gv_scr, la_scr, o_scr, att_t_scr, gate_scr,
     y_scr) = refs[n_x:]
    assert len(refs) - n_x == len(_MIXER_REF_NAMES)

    def load_x():
        if not from_tokens:
            return x_refs[0][0]
        blocks = [r[0] for r in x_refs[1:]]
        blocks[0] = jnp.where(s == 0, x_refs[0][...], blocks[0])
        return jnp.concatenate(blocks, axis=0)

    @pl.when(s == 0)
    def _():
        kprev_scr[...] = jnp.zeros((N_KV_HEADS, BLOCK, KV_W), BF16)
        vprev_scr[...] = jnp.zeros((KV_W, BLOCK), BF16)
        s_scr[...] = jnp.zeros(s_scr.shape, F32)

    x = load_x()
    xb = x.astype(BF16)
    proj = functools.partial(_in_proj, xb, w_in_ref)

    mix_in = proj(C_Q, C_LR_END)
    live = (s * ts + lax.broadcasted_iota(jnp.int32, (ts, 1), 0)) >= META_PAD
    cos, slo, shi = cos_ref[...], slo_ref[...], shi_ref[...]
    q = _rope(mix_in[:, C_Q:C_K], cos, slo, shi)
    k = _rope(mix_in[:, C_K:C_V], cos, slo, shi)
    v = mix_in[:, C_V:C_GQ]
    gq = mix_in[:, C_GQ:C_GK] * (GLA_DK ** -0.5)
    gk = jnp.where(live, mix_in[:, C_GK:C_GV], 0.0)
    gv = jnp.where(live, mix_in[:, C_GV:C_LR], 0.0)
    v_t = v.T

    @pl.when(s == n_steps - 1)
    def _():
        kwin_ref[0] = k[ts - BLOCK:, :].T
        vwin_ref[0] = v_t[:, ts - BLOCK:]

    gate_w = W_IN_COLS - C_GR
    piece_cols = 4 * LANES
    n_gate_pieces = min(N_KV_HEADS * nblk, -(-gate_w // piece_cols))
    gate_edges = [min(gate_w, piece_cols * (-(-gate_w // piece_cols) * i // n_gate_pieces))
                  for i in range(n_gate_pieces)] + [gate_w]

    def gate_piece_matmul(i):
        return proj(C_GR + gate_edges[i], C_GR + gate_edges[i + 1])

    def gate_piece_store(i, val):
        lo, hi = gate_edges[i], gate_edges[i + 1]
        mid = min(max(GV_W, lo), hi)
        if mid > lo:
            gate_scr[:, lo:mid] = _swish(val[:, :mid - lo])
        if hi > mid:
            gate_scr[:, mid:hi] = _sigmoid(val[:, mid - lo:])

    tri = (lax.broadcasted_iota(jnp.int32, (BLOCK, BLOCK), 1)
           <= lax.broadcasted_iota(jnp.int32, (BLOCK, BLOCK), 0))
    tri_bf = jnp.where(tri, 1.0, 0.0).astype(BF16)
    head_of_k = lax.broadcasted_iota(jnp.int32, (BLOCK, GK_W), 1) // GLA_DK
    head_of_v = lax.broadcasted_iota(jnp.int32, (BLOCK, GV_W), 1) // GLA_DV
    state_diag = (lax.broadcasted_iota(jnp.int32, (GK_W, GV_W), 0) // GLA_DK
                  == lax.broadcasted_iota(jnp.int32, (GK_W, GV_W), 1) // GLA_DV)
    causal = (lax.broadcasted_iota(jnp.int32, (BLOCK, GLA_HEADS * BLOCK), 1) % BLOCK
              <= lax.broadcasted_iota(jnp.int32, (BLOCK, GLA_HEADS * BLOCK), 0))
    gla = {"cums": [], "chunks": []}

    def issue_decay():
        gla["la"] = jnp.where(live, _log_decay(mix_in[:, C_LR:C_LR_END], w_a2_ref[...], b_a_ref[...]), 0.0)

    def issue_cumsum():
        worst = jnp.zeros((1, GK_W), F32)
        for c in range(nblk):
            parts = jnp.concatenate(_split3_bf16(gla["la"][c * BLOCK:(c + 1) * BLOCK]), axis=1)
            b3 = jnp.dot(tri_bf, parts, preferred_element_type=F32)
            b = b3[:, 0:GK_W] + b3[:, GK_W:2 * GK_W] + b3[:, 2 * GK_W:3 * GK_W]
            gla["cums"].append(b)
            worst = jnp.maximum(worst, -b[BLOCK - 1:BLOCK, :])
        gla["worst"] = worst

    def issue_chunk(c):
        rows = slice(c * BLOCK, (c + 1) * BLOCK)
        b = gla["cums"][c]
        b_last = b[BLOCK - 1:BLOCK, :]
        kc = gk[rows]
        q_dec = (gq[rows] * jnp.exp(b)).astype(BF16)
        k_inv = (kc * jnp.exp(-b)).astype(BF16)
        k_end_t = (kc * jnp.exp(b_last - b)).T.astype(BF16)
        vc = gv[rows].astype(BF16)
        zk = jnp.zeros_like(k_inv)
        k_bd = jnp.concatenate([jnp.where(head_of_k == h, k_inv, zk) for h in range(GLA_HEADS)], axis=0)
        a = lax.dot_general(q_dec, k_bd, (((1,), (1,)), ((), ())), preferred_element_type=F32)
        a = jnp.where(causal, a, 0.0).astype(BF16)
        zv = jnp.zeros_like(vc)
        v_bd = jnp.concatenate([jnp.where(head_of_v == h, vc, zv) for h in range(GLA_HEADS)], axis=0)
        o_intra = jnp.dot(a, v_bd, preferred_element_type=F32)
        ds = jnp.where(state_diag, jnp.dot(k_end_t, vc, preferred_element_type=F32), 0.0)
        gla["chunks"].append((rows, q_dec, o_intra, ds, _row_to_col(jnp.exp(b_last))))

    n_pairs = N_KV_HEADS * nblk
    second = min(1, n_pairs - 1)
    side_work = {}
    for pair, issue in ([(0, issue_decay), (second, issue_cumsum)]
                        + [(max(second, n_pairs - nblk + c), functools.partial(issue_chunk, c)) for c in range(nblk)]):
        side_work.setdefault(pair, []).append(issue)

    q_bf = (q * (HEAD_DIM ** -0.5 * LOG2_E)).astype(BF16)
    low_half = lax.broadcasted_iota(jnp.int32, (1, KV_W), 1) < HEAD_DIM
    k_swapped = pltpu.roll(k, HEAD_DIM, 1)
    k_dup = [jnp.where(low_half, k, k_swapped).astype(BF16), jnp.where(low_half, k_swapped, k).astype(BF16)]
    k_keys = [jnp.concatenate([kprev_scr[i], k_dup[i]], axis=0) for i in range(N_KV_HEADS)]
    vt_bf = v_t.astype(BF16)
    vt_keys = jnp.concatenate([vprev_scr[...], vt_bf], axis=1)
    kj = lax.broadcasted_iota(jnp.int32, (2 * BLOCK, Q_PER_KV * BLOCK), 0)
    qi = lax.broadcasted_iota(jnp.int32, (2 * BLOCK, Q_PER_KV * BLOCK), 1) % BLOCK
    band = (kj - qi >= 1) & (kj - qi <= BLOCK)
    q_low_half = lax.broadcasted_iota(jnp.int32, (BLOCK, LANES), 1) < HEAD_DIM
    pieces_done = 0
    for blk in range(nblk):
        first_key_slot = (s * nblk + blk - 1) * BLOCK
        valid = band & (kj + first_key_slot >= META_PAD)
        r0 = blk * BLOCK
        for kv in range(N_KV_HEADS):
            heads = [kv * Q_PER_KV + g for g in range(Q_PER_KV)]
            q_rows = []
            for hq in heads:
                grp = q_bf[r0:r0 + BLOCK, (hq // 2) * LANES:(hq // 2 + 1) * LANES]
                own = q_low_half if hq % 2 == 0 else jnp.logical_not(q_low_half)
                q_rows.append(jnp.where(own, grp, jnp.zeros_like(grp)))
            st = lax.dot_general(k_keys[kv][r0:r0 + 2 * BLOCK, :], jnp.concatenate(q_rows, axis=0),
                                 (((1,), (1,)), ((), ())), preferred_element_type=F32)
            gate_val = gate_piece_matmul(pieces_done) if pieces_done < n_gate_pieces else None
            for issue in side_work.get(blk * N_KV_HEADS + kv, ()):
                issue()
            st = jnp.where(valid, st, -jnp.inf)
            sink_row = jnp.concatenate([jnp.full((1, BLOCK), sink_ref[hq] * LOG2_E, F32) for hq in heads], axis=1)
            m = jnp.maximum(jnp.max(st, axis=0, keepdims=True), sink_row)
            p = jnp.exp2(st - m)
            den = jnp.sum(p, axis=0, keepdims=True) + jnp.exp2(sink_row - m)
            ot = jnp.dot(vt_keys[kv * HEAD_DIM:(kv + 1) * HEAD_DIM, r0:r0 + 2 * BLOCK], p.astype(BF16),
                         preferred_element_type=F32) * (1.0 / den)
            for g, hq in enumerate(heads):
                att_t_scr[hq * HEAD_DIM:(hq + 1) * HEAD_DIM, r0:r0 + BLOCK] = ot[:, g * BLOCK:(g + 1) * BLOCK]
            if gate_val is not None:
                gate_piece_store(pieces_done, gate_val)
                pieces_done += 1
    for i in range(pieces_done, n_gate_pieces):
        gate_piece_store(i, gate_piece_matmul(i))
    for i in range(N_KV_HEADS):
        kprev_scr[i] = k_dup[i][ts - BLOCK:, :]
    vprev_scr[...] = vt_bf[:, ts - BLOCK:]

    fast_ok = jnp.max(gla["worst"]) <= GLA_FAST_MAX_DECAY

    def proj_att():
        return jnp.dot(att_t_scr[...].T.astype(BF16), w_pa_ref[...], preferred_element_type=F32)

    def gla_fast():
        pa = proj_att()
        for rows, q_dec, o_intra, ds, decay_col in gla["chunks"]:
            s0 = s_scr[...]
            o_scr[rows, :] = o_intra + jnp.dot(q_dec, s0.astype(BF16), preferred_element_type=F32)
            s_scr[...] = decay_col * s0 + ds
        return pa

    def gla_slow():
        def body(i, carry):
            rows = pl.ds(pl.multiple_of(i * SUBLANES, SUBLANES), SUBLANES)
            la8, k8, q8, v8 = la_scr[rows, :], gk_scr[rows, :], gq_scr[rows, :], gv_scr[rows, :]
            outs = []
            for r in range(SUBLANES):
                a_col = _row_to_col(jnp.exp(la8[r:r + 1]))
                k_col = _row_to_col(k8[r:r + 1])
                q_col = _row_to_col(q8[r:r + 1])
                s1 = a_col * s_scr[...] + jnp.where(state_diag, k_col * v8[r:r + 1], 0.0)
                s_scr[...] = s1
                outs.append(jnp.sum(q_col * s1, axis=0, keepdims=True))
            o_scr[rows, :] = jnp.concatenate(outs, axis=0)
            return carry
        lax.fori_loop(0, ts // SUBLANES, body, 0)

    def finish_step(x_val, pa):
        gla = _gla_gate_out(o_scr[...], gate_scr[:, 0:GV_W], gn_ref[...])
        y_scr[...] = _finish(x_val, pa, gla, gate_scr[:, GV_W:GV_W + D_MODEL], gate_scr[:, GV_W + D_MODEL:],
                             w_pb_ref[...], w_out_ref[...], alpha)

    sprev_scr[...] = s_scr[...]
    pa = gla_fast()
    h_ref[0] = _layer_norm(y_scr[...], g1_ref[...], b1_ref[...])
    finish_step(x, pa)

    @pl.when(jnp.logical_not(fast_ok))
    def _():
        x_again = load_x()
        proj_again = functools.partial(_in_proj, x_again.astype(BF16), w_in_ref)
        gq_scr[...] = proj_again(C_GQ, C_GK) * (GLA_DK ** -0.5)
        gk_scr[...] = jnp.where(live, proj_again(C_GK, C_GV), 0.0)
        gv_scr[...] = jnp.where(live, proj_again(C_GV, C_LR), 0.0)
        la_scr[...] = jnp.where(live, _log_decay(proj_again(C_LR, C_LR_END), w_a2_ref[...], b_a_ref[...]), 0.0)
        s_scr[...] = sprev_scr[...]
        gla_slow()
        finish_step(x_again, proj_att())

    @pl.when(s == n_steps - 1)
    def _():
        for h in range(GLA_HEADS):
            sfin_ref[0, h] = s_scr[h * GLA_DK:(h + 1) * GLA_DK, h * GLA_DV:(h + 1) * GLA_DV]


def _ffn_update(h, w_up_ref, w_dn_ref, col_chunk, after_first_chunk=None):
    hb = h.astype(BF16)
    acc = jnp.zeros(h.shape, F32)
    for c in range(D_FF // col_chunk):
        u = jnp.dot(hb, w_up_ref[:, c * col_chunk:(c + 1) * col_chunk], preferred_element_type=F32)
        u = jnp.maximum(u, 0.0)
        acc = acc + jnp.dot((u * u).astype(BF16), w_dn_ref[c * col_chunk:(c + 1) * col_chunk, :],
                            preferred_element_type=F32)
        if c == 0 and after_first_chunk is not None:
            after_first_chunk()
    return acc


def _ffn_kernel(*refs, alpha, col_chunk, n_total):
    h_refs, (w_up_ref, w_dn_ref, g_ref, b_ref, o_ref, y_scr) = refs[:-6], refs[-6:]

    def write_norm():
        o_ref[...] = _layer_norm(y_scr[...], g_ref[...], b_ref[...]).reshape(o_ref.shape)

    def tile():
        blocks = [r[0] if len(r.shape) == 3 else r[...] for r in h_refs]
        h = blocks[0] if len(blocks) == 1 else jnp.concatenate(blocks, axis=0)
        acc = _ffn_update(h, w_up_ref, w_dn_ref, col_chunk, after_first_chunk=write_norm)
        y_scr[...] = alpha * h + acc

    t = pl.program_id(0)

    @pl.when(t == 0)
    def _():
        y_scr[...] = jnp.zeros(y_scr.shape, F32)

    pl.when(t < n_total)(tile)
    pl.when(t == n_total)(write_norm)


def _sample_proj_kernel(x_ref, cos_ref, slo_ref, shi_ref, w_in_ref, w_a2_ref, b_a_ref,
                        qkv_ref, gv_ref, gate_ref, kvt_ref, gcol_ref):
    proj = functools.partial(_in_proj, x_ref[...].astype(BF16), w_in_ref)

    cos, slo, shi = cos_ref[...], slo_ref[...], shi_ref[...]
    k = _rope(proj(C_K, C_V), cos, slo, shi)
    v = proj(C_V, C_GQ)
    qkv_ref[:, C_Q:C_K] = _rope(proj(C_Q, C_K), cos, slo, shi)
    qkv_ref[:, C_K:C_V] = k
    qkv_ref[:, C_V:C_GQ] = v
    gv_ref[...] = proj(C_GV, C_LR)
    gate_ref[...] = proj(C_GR, W_IN_COLS)
    def store_planes(ref, i, t):
        for p, part in enumerate(_split3_bf16(t)):
            ref[N_SPLIT * i + p] = part.astype(F32).T

    store_planes(kvt_ref, 0, k)
    store_planes(kvt_ref, 1, v)
    store_planes(gcol_ref, 0, jnp.exp(_log_decay(proj(C_LR, C_LR_END), w_a2_ref[...], b_a_ref[...])))
    store_planes(gcol_ref, 1, proj(C_GQ, C_GK) * (GLA_DK ** -0.5))
    store_planes(gcol_ref, 2, proj(C_GK, C_GV))


def _sample_mix_kernel(qkv_ref, gv_ref, kvt_ref, gcol_ref, ck_ref, cv_ref, st_ref, sink_ref, *rest, group):
    att_ref, o_ref, nk_ref, nv_ref, nst_ref = rest[-5:]
    head_row = lax.broadcasted_iota(jnp.int32, (N_Q_HEADS, ATT_W), 0)
    head_lane = lax.broadcasted_iota(jnp.int32, (N_Q_HEADS, ATT_W), 1) // HEAD_DIM
    own = head_row == head_lane
    r8 = lax.broadcasted_iota(jnp.int32, (N_Q_HEADS, KV_W), 0)
    swap = (r8 % 2) != (r8 // Q_PER_KV)
    key_i = lax.broadcasted_iota(jnp.int32, (N_Q_HEADS, BLOCK), 1)
    last_row = lax.broadcasted_iota(jnp.int32, (KV_W, BLOCK), 1) == BLOCK - 1
    sink = sink_ref[...][:, 0:1]

    qkv8 = qkv_ref[...]
    gv8 = gv_ref[...]
    q8s, scores = [], []
    for j in range(group):
        q_row = qkv8[j:j + 1, C_Q:C_K]
        qm = jnp.where(own, jnp.broadcast_to(q_row, (N_Q_HEADS, ATT_W)), 0.0)
        fold = qm[:, 0:128] + qm[:, 128:256] + qm[:, 256:384] + qm[:, 384:512]
        q8 = (jnp.where(swap, pltpu.roll(fold, HEAD_DIM, 1), fold) * (HEAD_DIM ** -0.5)).astype(BF16)
        q8s.append(q8)
        scores.append(jnp.dot(q8, ck_ref[j].reshape(KV_W, BLOCK).astype(BF16), preferred_element_type=F32))

    sel = (lax.broadcasted_iota(jnp.int32, (N_SPLIT * group, group * LANES), 0) % group
           == lax.broadcasted_iota(jnp.int32, (N_SPLIT * group, group * LANES), 1) // LANES)
    sel = jnp.where(sel, 1.0, 0.0).astype(BF16)
    spread = lambda ref, i: jnp.dot(ref[i].astype(BF16), sel, preferred_element_type=F32)
    k_cols, v_cols = spread(kvt_ref, 0), spread(kvt_ref, 1)
    a_cols, q_cols, k_gla_cols = (spread(gcol_ref, i) for i in range(3))

    probs = []
    for j in range(group):
        k_new = qkv8[j:j + 1, C_K:C_V]
        sc = jnp.where(key_i >= 1, scores[j], -jnp.inf)
        s_new = jnp.sum(q8s[j].astype(F32) * k_new.astype(BF16).astype(F32), axis=-1, keepdims=True)
        m = jnp.maximum(jnp.maximum(jnp.max(sc, axis=-1, keepdims=True), s_new), sink)
        p = jnp.exp(sc - m)
        p_new = jnp.exp(s_new - m)
        den = jnp.sum(p, axis=-1, keepdims=True) + p_new + jnp.exp(sink - m)
        probs.append((p.astype(BF16), p_new, den))

    att_rows, o_rows = [], []
    for j in range(group):
        p_bf, p_new, den = probs[j]
        v_new = qkv8[j:j + 1, C_V:C_GQ]
        o8 = (lax.dot_general(p_bf, cv_ref[j].reshape(KV_W, BLOCK).astype(BF16), (((1,), (1,)), ((), ())),
                              preferred_element_type=F32)
              + p_new.astype(BF16).astype(F32) * v_new.astype(BF16).astype(F32)) / den
        o8 = jnp.where(swap, pltpu.roll(o8, HEAD_DIM, 1), o8)
        o_wide = jnp.concatenate([o8, o8, o8, o8], axis=1)
        att_rows.append(jnp.sum(jnp.where(own, o_wide, 0.0), axis=0, keepdims=True))

    for j in range(group):
        lanes_j = slice(j * LANES, (j + 1) * LANES)
        k_old = ck_ref[j].reshape(KV_W, BLOCK)
        v_old = cv_ref[j].reshape(KV_W, BLOCK)
        nk_ref[j] = jnp.where(last_row, k_cols[:, lanes_j], pltpu.roll(k_old, BLOCK - 1, 1)).reshape(
            N_KV_HEADS, HEAD_DIM, BLOCK)
        nv_ref[j] = jnp.where(last_row, v_cols[:, lanes_j], pltpu.roll(v_old, BLOCK - 1, 1)).reshape(
            N_KV_HEADS, HEAD_DIM, BLOCK)
        a_col, q_col, k_col = a_cols[:, lanes_j], q_cols[:, lanes_j], k_gla_cols[:, lanes_j]
        v_row = gv8[j:j + 1, :]
        v_exp = jnp.concatenate([jnp.broadcast_to(v_row[:, h * GLA_DV:(h + 1) * GLA_DV], (GLA_DK, GLA_DV))
                                 for h in range(GLA_HEADS)], axis=0)
        s1 = a_col * st_ref[j].reshape(GK_W, GLA_DV) + k_col * v_exp
        nst_ref[j] = s1.reshape(GLA_HEADS, GLA_DK, GLA_DV)
        qs = q_col * s1
        o_rows.append(jnp.concatenate(
            [jnp.sum(qs[h * GLA_DK:(h + 1) * GLA_DK], axis=0, keepdims=True) for h in range(GLA_HEADS)], axis=1))
    att_ref[...] = jnp.concatenate(att_rows, axis=0)
    o_ref[...] = jnp.concatenate(o_rows, axis=0)


def _sample_tail_kernel(x_ref, att_ref, o_ref, gate_ref, gn_ref, w_pa_ref, w_pb_ref, w_out_ref, g1_ref, b1_ref,
                        w_up_ref, w_dn_ref, g2_ref, b2_ref, out_ref, *, alpha, col_chunk):
    gla = _gla_gate_out(o_ref[...], _swish(gate_ref[:, 0:GV_W]), gn_ref[...])
    pa = jnp.dot(att_ref[...].astype(BF16), w_pa_ref[...], preferred_element_type=F32)
    y = _finish(x_ref[...], pa, gla, _sigmoid(gate_ref[:, GV_W:GV_W + D_MODEL]),
                _sigmoid(gate_ref[:, GV_W + D_MODEL:]), w_pb_ref[...], w_out_ref[...], alpha)
    h = _layer_norm(y, g1_ref[...], b1_ref[...])
    acc = _ffn_update(h, w_up_ref, w_dn_ref, col_chunk)
    out_ref[...] = _layer_norm(alpha * h + acc, g2_ref[...], b2_ref[...])


def _rope_tables(pos):
    half = ROT_DIM // 2
    inv = ROPE_THETA ** (-jnp.arange(half, dtype=F32) * 2.0 / ROT_DIM)
    d = jnp.arange(LANES) % HEAD_DIM
    ang = pos.astype(F32)[:, None] * inv[d % half][None, :]
    cos, sin = jnp.cos(ang), jnp.sin(ang)
    cos_t = jnp.where(d < ROT_DIM, cos, 1.0)
    sin_lo = jnp.where(d < half, -sin, 0.0)
    sin_hi = jnp.where((d >= half) & (d < ROT_DIM), sin, 0.0)
    return cos_t, sin_lo, sin_hi


def _const_spec(shape, layer=None):
    if layer is None:
        return pl.BlockSpec(shape, lambda *_: (0,) * len(shape), pipeline_mode=pl.Buffered(1))
    return pl.BlockSpec((None,) + shape, lambda *_: (layer,) + (0,) * len(shape), pipeline_mode=pl.Buffered(1))


MIXER_STEP_ROWS = (3 * BLOCK, 2 * BLOCK, BLOCK)
FFN_TILE_ROWS = (6 * BLOCK, 4 * BLOCK, 3 * BLOCK, 2 * BLOCK, BLOCK)
FFN_COL_CHUNK = D_MODEL


def _first_divisor(total, candidates):
    for t in candidates:
        if total % t == 0:
            return t
    raise ValueError(f"{total} rows are not a multiple of {candidates[-1]}")


def kernel(x_prompt, x_sample, cache_k_win, cache_v_win, state_gla, meta_tokens, w_in, w_a2, b_a, attn_sink,
           gla_norm_g, w_proj_a, w_proj_b, w_out, ln1_g, ln1_b, w_up, w_down, ln2_g, ln2_b):
    depth = w_in.shape[0]
    bsz, seq, _ = x_prompt.shape
    nsmp, dec_seq, _ = x_sample.shape
    assert dec_seq == 1 and cache_k_win.shape[2] == BLOCK and seq % BLOCK == 0
    alpha = (2 * depth) ** 0.25
    lp = seq + BLOCK
    ts = _first_divisor(lp, MIXER_STEP_ROWS)
    n_steps = lp // ts
    rows = bsz * lp
    ffn_tile = _first_divisor(rows, FFN_TILE_ROWS)
    last_tile = _first_divisor(seq, FFN_TILE_ROWS)
    group = next(g for g in (2 * SUBLANES, SUBLANES) if nsmp % g == 0)

    w_in_r = jnp.swapaxes(w_in, 1, 2).astype(BF16)
    w_a2_p = jnp.concatenate([w_a2, jnp.zeros((depth, LANES - GLA_RANK, GK_W), w_a2.dtype)], axis=1).astype(BF16)
    w_pa, w_pb, w_o = w_proj_a.astype(BF16), w_proj_b.astype(BF16), w_out.astype(BF16)
    w_u, w_d = w_up.astype(BF16), w_down.astype(BF16)
    b_a3 = b_a.reshape(depth, 1, GK_W)
    gn3 = gla_norm_g.reshape(depth, 1, GLA_DV)
    g1, b1 = ln1_g.reshape(depth, 1, D_MODEL), ln1_b.reshape(depth, 1, D_MODEL)
    g2, b2 = ln2_g.reshape(depth, 1, D_MODEL), ln2_b.reshape(depth, 1, D_MODEL)
    sink_lanes = jnp.broadcast_to(attn_sink[:, :, None], (depth, N_Q_HEADS, LANES))

    cos_p, slo_p, shi_p = _rope_tables(jnp.arange(lp) - META_PAD)
    cos_s, slo_s, shi_s = (jnp.broadcast_to(t, (nsmp, LANES)) for t in _rope_tables(PAST_LEN + jnp.arange(1)))

    meta_block = jnp.concatenate([jnp.zeros((META_PAD, D_MODEL), x_prompt.dtype),
                                  meta_tokens.astype(x_prompt.dtype)], axis=0)
    nblk = ts // BLOCK
    xp = None
    xs = x_sample.reshape(nsmp, D_MODEL)
    ck = jnp.transpose(cache_k_win, (0, 1, 3, 4, 2))
    cv = jnp.transpose(cache_v_win, (0, 1, 3, 4, 2))

    cparams = functools.partial(pltpu.CompilerParams, vmem_limit_bytes=VMEM_LIMIT)
    pk, pv, pst = [], [], []
    stacked = [jnp.zeros(ck.shape, F32), jnp.zeros(cv.shape, F32), jnp.zeros(state_gla.shape, F32)]
    for l in range(depth):
        n_total = bsz * n_steps
        seq_of = lambda t: jnp.minimum(t, n_total - 1) // n_steps
        step_of = lambda t: jnp.minimum(t, n_total - 1) % n_steps
        step_spec = pl.BlockSpec((1, ts, D_MODEL), lambda t: (seq_of(t), step_of(t), 0))
        h_spec = pl.BlockSpec((1, ts, D_MODEL), lambda t: (seq_of(jnp.maximum(t - 1, 0)), step_of(jnp.maximum(t - 1, 0)), 0))
        tab_spec = pl.BlockSpec((ts, LANES), lambda t: (step_of(t), 0))
        if l == 0:
            x_specs = [_const_spec((BLOCK, D_MODEL))] + [
                pl.BlockSpec((1, BLOCK, D_MODEL),
                             lambda t, j=j: (seq_of(t), jnp.maximum(nblk * step_of(t) + j - 1, 0), 0))
                for j in range(nblk)]
            x_args = [meta_block] + [x_prompt] * nblk
        else:
            x_specs, x_args = [step_spec], [xp]
        hp, kwin, vwin, sfin = pl.pallas_call(
            functools.partial(_mixer_prompt_kernel, ts=ts, n_steps=n_steps, n_total=n_total, alpha=alpha,
                              from_tokens=(l == 0)),
            grid=(n_total + 1,),
            in_specs=x_specs + [tab_spec, tab_spec, tab_spec,
                      _const_spec((W_IN_COLS, D_MODEL), l), _const_spec((LANES, GK_W), l), _const_spec((1, GK_W), l),
                      pl.BlockSpec(memory_space=pltpu.SMEM), _const_spec((1, GLA_DV), l),
                      _const_spec((ATT_W, D_MODEL), l), _const_spec((GV_W, D_MODEL), l),
                      _const_spec((D_MODEL, D_MODEL), l), _const_spec((1, D_MODEL), l), _const_spec((1, D_MODEL), l)],
            out_specs=[h_spec,
                       pl.BlockSpec((1, BLOCK, KV_W), lambda t: (seq_of(t), 0, 0)),
                       pl.BlockSpec((1, BLOCK, KV_W), lambda t: (seq_of(t), 0, 0)),
                       pl.BlockSpec((1, GLA_HEADS, GLA_DK, GLA_DV), lambda t: (seq_of(t), 0, 0, 0))],
            out_shape=[jax.ShapeDtypeStruct((bsz, lp, D_MODEL), F32),
                       jax.ShapeDtypeStruct((bsz, BLOCK, KV_W), F32),
                       jax.ShapeDtypeStruct((bsz, BLOCK, KV_W), F32),
                       jax.ShapeDtypeStruct((bsz, GLA_HEADS, GLA_DK, GLA_DV), F32)],
            scratch_shapes=[pltpu.VMEM((N_KV_HEADS, BLOCK, KV_W), BF16), pltpu.VMEM((KV_W, BLOCK), BF16),
                            pltpu.VMEM((GK_W, GV_W), F32), pltpu.VMEM((GK_W, GV_W), F32),
                            pltpu.VMEM((ts, GK_W), F32), pltpu.VMEM((ts, GK_W), F32), pltpu.VMEM((ts, GV_W), F32),
                            pltpu.VMEM((ts, GK_W), F32), pltpu.VMEM((ts, GV_W), F32), pltpu.VMEM((ATT_W, ts), F32),
                            pltpu.VMEM((ts, W_IN_COLS - C_GR), F32), pltpu.VMEM((ts, D_MODEL), F32)],
            compiler_params=cparams(dimension_semantics=("arbitrary",)),
            name=f"mixer_prompt_{l}",
        )(*x_args, cos_p, slo_p, shi_p, w_in_r, w_a2_p, b_a3, attn_sink[l], gn3, w_pa, w_pb, w_o, g1, b1)
        pk.append(kwin.reshape(bsz, N_KV_HEADS, HEAD_DIM, BLOCK))
        pv.append(vwin.reshape(bsz, N_KV_HEADS, HEAD_DIM, BLOCK))
        pst.append(sfin)

        ffn = functools.partial(_ffn_kernel, alpha=alpha, col_chunk=FFN_COL_CHUNK)
        ffn_w = [_const_spec((D_MODEL, D_FF), l), _const_spec((D_FF, D_MODEL), l),
                 _const_spec((1, D_MODEL), l), _const_spec((1, D_MODEL), l)]
        if l < depth - 1:
            n_tiles = rows // ffn_tile
            xp = pl.pallas_call(
                functools.partial(ffn, n_total=n_tiles), grid=(n_tiles + 1,),
                in_specs=[pl.BlockSpec((ffn_tile, D_MODEL), lambda t: (jnp.minimum(t, n_tiles - 1), 0))] + ffn_w,
                out_specs=pl.BlockSpec((ffn_tile, D_MODEL), lambda t: (jnp.maximum(t - 1, 0), 0)),
                out_shape=jax.ShapeDtypeStruct((rows, D_MODEL), F32),
                scratch_shapes=[pltpu.VMEM((ffn_tile, D_MODEL), F32)],
                compiler_params=cparams(dimension_semantics=("arbitrary",)),
                name=f"ffn_prompt_{l}",
            )(hp.reshape(rows, D_MODEL), w_u, w_d, g2, b2).reshape(bsz, lp, D_MODEL)
        else:
            pieces = last_tile // BLOCK
            per_seq = seq // last_tile
            n_tiles = bsz * per_seq
            rd = lambda t: jnp.minimum(t, n_tiles - 1)
            wr = lambda t: jnp.maximum(t - 1, 0)
            y_prompt = pl.pallas_call(
                functools.partial(ffn, n_total=n_tiles), grid=(n_tiles + 1,),
                in_specs=[pl.BlockSpec((1, BLOCK, D_MODEL),
                                       lambda t, j=j: (rd(t) // per_seq, 1 + pieces * (rd(t) % per_seq) + j, 0))
                          for j in range(pieces)] + ffn_w,
                out_specs=pl.BlockSpec((1, last_tile, D_MODEL), lambda t: (wr(t) // per_seq, wr(t) % per_seq, 0)),
                out_shape=jax.ShapeDtypeStruct((bsz, seq, D_MODEL), F32),
                scratch_shapes=[pltpu.VMEM((last_tile, D_MODEL), F32)],
                compiler_params=cparams(dimension_semantics=("arbitrary",)),
                name=f"ffn_prompt_{l}",
            )(*([hp] * pieces), w_u, w_d, g2, b2)

        proj_out = [(nsmp, C_GQ), (nsmp, GV_W), (nsmp, W_IN_COLS - C_GR),
                    (2 * N_SPLIT, KV_W, nsmp), (3 * N_SPLIT, GK_W, nsmp)]
        qkv, gv_s, gate, kvt, gcol = pl.pallas_call(
            _sample_proj_kernel, grid=(1,),
            in_specs=[_const_spec((nsmp, D_MODEL)), _const_spec((nsmp, LANES)), _const_spec((nsmp, LANES)),
                      _const_spec((nsmp, LANES)), _const_spec((W_IN_COLS, D_MODEL), l),
                      _const_spec((LANES, GK_W), l), _const_spec((1, GK_W), l)],
            out_specs=[_const_spec(s) for s in proj_out],
            out_shape=[jax.ShapeDtypeStruct(s, F32) for s in proj_out],
            compiler_params=cparams(dimension_semantics=("arbitrary",)),
            name=f"sample_proj_{l}",
        )(xs, cos_s, slo_s, shi_s, w_in_r, w_a2_p, b_a3)
        def by_group(t):
            t = t.reshape(t.shape[0] // N_SPLIT, N_SPLIT, t.shape[1], nsmp // group, group)
            return jnp.transpose(t, (3, 0, 2, 1, 4)).reshape(nsmp // group, t.shape[0], t.shape[2], N_SPLIT * group)
        col_spec = lambda n, width: pl.BlockSpec((None, n, width, N_SPLIT * group), lambda i: (i, 0, 0, 0))

        grp = lambda width: pl.BlockSpec((group, width), lambda i: (i, 0))
        cache_spec = pl.BlockSpec((None, group, N_KV_HEADS, HEAD_DIM, BLOCK), lambda i: (l, i, 0, 0, 0))
        state_spec = pl.BlockSpec((None, group, GLA_HEADS, GLA_DK, GLA_DV), lambda i: (l, i, 0, 0, 0))
        n_mix_in = 8
        att_s, o_s, *stacked = pl.pallas_call(
            functools.partial(_sample_mix_kernel, group=group), grid=(nsmp // group,),
            in_specs=[grp(C_GQ), grp(GV_W), col_spec(2, KV_W), col_spec(3, GK_W), cache_spec, cache_spec, state_spec,
                      _const_spec((N_Q_HEADS, LANES), l)] + [pl.BlockSpec(memory_space=pl.ANY)] * len(stacked),
            out_specs=[grp(ATT_W), grp(GV_W), cache_spec, cache_spec, state_spec],
            out_shape=[jax.ShapeDtypeStruct((nsmp, ATT_W), F32), jax.ShapeDtypeStruct((nsmp, GV_W), F32),
                       jax.ShapeDtypeStruct(ck.shape, F32), jax.ShapeDtypeStruct(cv.shape, F32),
                       jax.ShapeDtypeStruct(state_gla.shape, F32)],
            input_output_aliases={n_mix_in + i: 2 + i for i in range(len(stacked))},
            compiler_params=cparams(dimension_semantics=("arbitrary",)),
            name=f"sample_mix_{l}",
        )(qkv, gv_s, by_group(kvt), by_group(gcol), ck, cv, state_gla, sink_lanes, *stacked)

        xs = pl.pallas_call(
            functools.partial(_sample_tail_kernel, alpha=alpha, col_chunk=FFN_COL_CHUNK), grid=(1,),
            in_specs=[_const_spec((nsmp, D_MODEL)), _const_spec((nsmp, ATT_W)), _const_spec((nsmp, GV_W)),
                      _const_spec((nsmp, W_IN_COLS - C_GR)), _const_spec((1, GLA_DV), l),
                      _const_spec((ATT_W, D_MODEL), l), _const_spec((GV_W, D_MODEL), l),
                      _const_spec((D_MODEL, D_MODEL), l), _const_spec((1, D_MODEL), l), _const_spec((1, D_MODEL), l)]
            + ffn_w,
            out_specs=_const_spec((nsmp, D_MODEL)),
            out_shape=jax.ShapeDtypeStruct((nsmp, D_MODEL), F32),
            compiler_params=cparams(dimension_semantics=("arbitrary",)),
            name=f"sample_tail_{l}",
        )(xs, att_s, o_s, gate, gn3, w_pa, w_pb, w_o, g1, b1, w_u, w_d, g2, b2)

    y_sample = xs.reshape(nsmp, 1, D_MODEL)
    to_rows = lambda t: jnp.transpose(t, (0, 1, 4, 2, 3))
    return (y_prompt, y_sample, to_rows(jnp.stack(pk)), to_rows(jnp.stack(pv)), jnp.stack(pst),
            to_rows(stacked[0]), to_rows(stacked[1]), stacked[2])
```

```python
import functools

import jax
import jax.numpy as jnp
from jax import lax
from jax.experimental import pallas as pl
from jax.experimental.pallas import tpu as pltpu

F32 = jnp.float32
BF16 = jnp.bfloat16

D_MODEL = 1024
PAST_LEN = 8192
N_META = 16
BLOCK = 128
META_PAD = BLOCK - N_META
HEAD_DIM = 64
N_Q_HEADS = 8
N_KV_HEADS = 2
Q_PER_KV = N_Q_HEADS // N_KV_HEADS
ROT_DIM = HEAD_DIM // 4
ROPE_THETA = 500000.0
GLA_HEADS = 4
GLA_DK = 64
GLA_DV = 128
GLA_RANK = 16
GLA_TAU = 16.0
D_FF = 4 * D_MODEL
ATT_W = N_Q_HEADS * HEAD_DIM
KV_W = N_KV_HEADS * HEAD_DIM
GK_W = GLA_HEADS * GLA_DK
GV_W = GLA_HEADS * GLA_DV
LOG2_E = 1.4426950408889634
LN_EPS = 1e-5
RMS_EPS = 1e-6
LANES = 128
SUBLANES = 8
VMEM_LIMIT = 56 * 1024 * 1024

C_Q = 0
C_K = C_Q + ATT_W
C_V = C_K + KV_W
C_GQ = C_V + KV_W
C_GK = C_GQ + GK_W
C_GV = C_GK + GK_W
C_LR = C_GV + GV_W
C_GR = C_LR + GLA_RANK
C_GA = C_GR + GV_W
C_GB = C_GA + D_MODEL
W_IN_COLS = C_GB + D_MODEL
C_LR_END = C_LR + LANES

GLA_FAST_MAX_DECAY = 40.0


def _sigmoid(x):
    return 1.0 / (1.0 + jnp.exp(-x))


def _layer_norm(y, g, b):
    mu = jnp.mean(y, axis=-1, keepdims=True)
    yc = y - mu
    var = jnp.mean(yc * yc, axis=-1, keepdims=True)
    return yc * lax.rsqrt(var + LN_EPS) * g + b


def _rope(t, cos, sin_lo, sin_hi):
    outs = []
    for j in range(t.shape[1] // LANES):
        tj = t[:, j * LANES:(j + 1) * LANES]
        outs.append(tj * cos + pltpu.roll(tj, LANES - ROT_DIM // 2, 1) * sin_lo
                    + pltpu.roll(tj, ROT_DIM // 2, 1) * sin_hi)
    return outs[0] if len(outs) == 1 else jnp.concatenate(outs, axis=1)


def _in_proj(xb, w_t_ref, lo, hi):
    return lax.dot_general(xb, w_t_ref[lo:hi, :], (((1,), (1,)), ((), ())), preferred_element_type=F32)


def _log_decay(glr, w_a2, b_a):
    z = jnp.dot(glr.astype(BF16), w_a2, preferred_element_type=F32) + b_a
    return (jnp.minimum(z, 0.0) - jnp.log1p(jnp.exp(-jnp.abs(z)))) * (1.0 / GLA_TAU)


def _row_to_col(row):
    n = row.shape[1]
    eye = lax.broadcasted_iota(jnp.int32, (n, n), 0) == lax.broadcasted_iota(jnp.int32, (n, n), 1)
    return jnp.sum(jnp.where(eye, jnp.broadcast_to(row, (n, n)), 0.0), axis=1, keepdims=True)


N_SPLIT = 3


def _split3_bf16(a):
    hi = a.astype(BF16)
    r = a - hi.astype(F32)
    mid = r.astype(BF16)
    lo = (r - mid.astype(F32)).astype(BF16)
    return hi, mid, lo


def _swish(x):
    return x * _sigmoid(x)


def _gla_gate_out(o, swish_gr, gn):
    outs = []
    for h in range(GLA_HEADS):
        oh = o[:, h * GLA_DV:(h + 1) * GLA_DV]
        ms = jnp.mean(oh * oh, axis=-1, keepdims=True)
        outs.append(oh * lax.rsqrt(ms + RMS_EPS) * gn)
    return jnp.concatenate(outs, axis=1) * swish_gr


def _finish(x, pa, gla, sig_a, sig_b, w_pb, w_out, alpha):
    pb = jnp.dot(gla.astype(BF16), w_pb, preferred_element_type=F32)
    m = sig_a * pa + sig_b * pb
    return alpha * x + jnp.dot(m.astype(BF16), w_out, preferred_element_type=F32)


_MIXER_REF_NAMES = ("cos", "slo", "shi", "w_in", "w_a2", "b_a", "sink", "gn", "w_pa", "w_pb", "w_out", "g1", "b1",
                    "h", "kwin", "vwin", "sfin",
                    "kprev_scr", "vprev_scr", "s_scr", "sprev_scr", "gq_scr", "gk_scr", "gv_scr", "la_scr",
                    "o_scr", "att_t_scr", "gate_scr", "y_scr")


def _mixer_prompt_kernel(*refs, ts, n_steps, n_total, alpha, from_tokens):
    t = pl.program_id(0)
    named = dict(zip(_MIXER_REF_NAMES, refs[len(refs) - len(_MIXER_REF_NAMES):]))
    h_ref, y_scr, g1_ref, b1_ref = named["h"], named["y_scr"], named["g1"], named["b1"]

    @pl.when(t == 0)
    def _():
        y_scr[...] = jnp.zeros(y_scr.shape, F32)

    @pl.when(t < n_total)
    def _():
        _mixer_step(lax.rem(t, n_steps), *refs, ts=ts, n_steps=n_steps, alpha=alpha, from_tokens=from_tokens)

    @pl.when(t == n_total)
    def _():
        h_ref[0] = _layer_norm(y_scr[...], g1_ref[...], b1_ref[...])


def _mixer_step(s, *refs, ts, n_steps, alpha, from_tokens):
    nblk = ts // BLOCK
    n_x = 1 + nblk if from_tokens else 1
    x_refs = refs[:n_x]
    (cos_ref, slo_ref, shi_ref, w_in_ref, w_a2_ref, b_a_ref, sink_ref, gn_ref,
     w_pa_ref, w_pb_ref, w_out_ref, g1_ref, b1_ref,
     h_ref, kwin_ref, vwin_ref, sfin_ref,
     kprev_scr, vprev_scr, s_scr, sprev_scr, gq_scr, gk_scr, gv_scr, la_scr, o_scr, att_t_scr, gate_scr,
     y_scr) = refs[n_x:]
    assert len(refs) - n_x == len(_MIXER_REF_NAMES)

    def load_x():
        if not from_tokens:
            return x_refs[0][0]
        blocks = [r[0] for r in x_refs[1:]]
        blocks[0] = jnp.where(s == 0, x_refs[0][...], blocks[0])
        return jnp.concatenate(blocks, axis=0)

    @pl.when(s == 0)
    def _():
        kprev_scr[...] = jnp.zeros((N_KV_HEADS, BLOCK, KV_W), BF16)
        vprev_scr[...] = jnp.zeros((KV_W, BLOCK), BF16)
        s_scr[...] = jnp.zeros(s_scr.shape, F32)

    x = load_x()
    xb = x.astype(BF16)
    proj = functools.partial(_in_proj, xb, w_in_ref)

    mix_in = proj(C_Q, C_LR_END)
    live = (s * ts + lax.broadcasted_iota(jnp.int32, (ts, 1), 0)) >= META_PAD
    cos, slo, shi = cos_ref[...], slo_ref[...], shi_ref[...]
    q = _rope(mix_in[:, C_Q:C_K], cos, slo, shi)
    k = _rope(mix_in[:, C_K:C_V], cos, slo, shi)
    v = mix_in[:, C_V:C_GQ]
    gq = mix_in[:, C_GQ:C_GK] * (GLA_DK ** -0.5)
    gk = jnp.where(live, mix_in[:, C_GK:C_GV], 0.0)
    gv = jnp.where(live, mix_in[:, C_GV:C_LR], 0.0)
    v_t = v.T

    @pl.when(s == n_steps - 1)
    def _():
        kwin_ref[0] = k[ts - BLOCK:, :].T
        vwin_ref[0] = v_t[:, ts - BLOCK:]

    gate_w = W_IN_COLS - C_GR
    piece_cols = 4 * LANES
    n_gate_pieces = min(N_KV_HEADS * nblk, -(-gate_w // piece_cols))
    gate_edges = [min(gate_w, piece_cols * (-(-gate_w // piece_cols) * i // n_gate_pieces))
                  for i in range(n_gate_pieces)] + [gate_w]

    def gate_piece_matmul(i):
        return proj(C_GR + gate_edges[i], C_GR + gate_edges[i + 1])

    def gate_piece_store(i, val):
        lo, hi = gate_edges[i], gate_edges[i + 1]
        mid = min(max(GV_W, lo), hi)
        if mid > lo:
            gate_scr[:, lo:mid] = _swish(val[:, :mid - lo])
        if hi > mid:
            gate_scr[:, mid:hi] = _sigmoid(val[:, mid - lo:])

    tri = (lax.broadcasted_iota(jnp.int32, (BLOCK, BLOCK), 1)
           <= lax.broadcasted_iota(jnp.int32, (BLOCK, BLOCK), 0))
    tri_bf = jnp.where(tri, 1.0, 0.0).astype(BF16)
    head_of_k = lax.broadcasted_iota(jnp.int32, (BLOCK, GK_W), 1) // GLA_DK
    head_of_v = lax.broadcasted_iota(jnp.int32, (BLOCK, GV_W), 1) // GLA_DV
    state_diag = (lax.broadcasted_iota(jnp.int32, (GK_W, GV_W), 0) // GLA_DK
                  == lax.broadcasted_iota(jnp.int32, (GK_W, GV_W), 1) // GLA_DV)
    causal = (lax.broadcasted_iota(jnp.int32, (BLOCK, GLA_HEADS * BLOCK), 1) % BLOCK
              <= lax.broadcasted_iota(jnp.int32, (BLOCK, GLA_HEADS * BLOCK), 0))
    gla = {"cums": [], "chunks": []}

    def issue_decay():
        gla["la"] = jnp.where(live, _log_decay(mix_in[:, C_LR:C_LR_END], w_a2_ref[...], b_a_ref[...]), 0.0)

    def issue_cumsum():
        worst = jnp.zeros((1, GK_W), F32)
        for c in range(nblk):
            parts = jnp.concatenate(_split3_bf16(gla["la"][c * BLOCK:(c + 1) * BLOCK]), axis=1)
            b3 = jnp.dot(tri_bf, parts, preferred_element_type=F32)
            b = b3[:, 0:GK_W] + b3[:, GK_W:2 * GK_W] + b3[:, 2 * GK_W:3 * GK_W]
            gla["cums"].append(b)
            worst = jnp.maximum(worst, -b[BLOCK - 1:BLOCK, :])
        gla["worst"] = worst

    def issue_chunk(c):
        rows = slice(c * BLOCK, (c + 1) * BLOCK)
        b = gla["cums"][c]
        b_last = b[BLOCK - 1:BLOCK, :]
        kc = gk[rows]
        q_dec = (gq[rows] * jnp.exp(b)).astype(BF16)
        k_inv = (kc * jnp.exp(-b)).astype(BF16)
        k_end_t = (kc * jnp.exp(b_last - b)).T.astype(BF16)
        vc = gv[rows].astype(BF16)
        zk = jnp.zeros_like(k_inv)
        k_bd = jnp.concatenate([jnp.where(head_of_k == h, k_inv, zk) for h in range(GLA_HEADS)], axis=0)
        a = lax.dot_general(q_dec, k_bd, (((1,), (1,)), ((), ())), preferred_element_type=F32)
        a = jnp.where(causal, a, 0.0).astype(BF16)
        zv = jnp.zeros_like(vc)
        v_bd = jnp.concatenate([jnp.where(head_of_v == h, vc, zv) for h in range(GLA_HEADS)], axis=0)
        o_intra = jnp.dot(a, v_bd, preferred_element_type=F32)
        ds = jnp.where(state_diag, jnp.dot(k_end_t, vc, preferred_element_type=F32), 0.0)
        gla["chunks"].append((rows, q_dec, o_intra, ds, _row_to_col(jnp.exp(b_last))))

    n_pairs = N_KV_HEADS * nblk
    second = min(1, n_pairs - 1)
    side_work = {}
    for pair, issue in ([(0, issue_decay), (second, issue_cumsum)]
                        + [(max(second, n_pairs - nblk + c), functools.partial(issue_chunk, c)) for c in range(nblk)]):
        side_work.setdefault(pair, []).append(issue)

    q_bf = (q * (HEAD_DIM ** -0.5 * LOG2_E)).astype(BF16)
    low_half = lax.broadcasted_iota(jnp.int32, (1, KV_W), 1) < HEAD_DIM
    k_swapped = pltpu.roll(k, HEAD_DIM, 1)
    k_dup = [jnp.where(low_half, k, k_swapped).astype(BF16), jnp.where(low_half, k_swapped, k).astype(BF16)]
    k_keys = [jnp.concatenate([kprev_scr[i], k_dup[i]], axis=0) for i in range(N_KV_HEADS)]
    vt_bf = v_t.astype(BF16)
    vt_keys = jnp.concatenate([vprev_scr[...], vt_bf], axis=1)
    kj = lax.broadcasted_iota(jnp.int32, (2 * BLOCK, Q_PER_KV * BLOCK), 0)
    qi = lax.broadcasted_iota(jnp.int32, (2 * BLOCK, Q_PER_KV * BLOCK), 1) % BLOCK
    band = (kj - qi >= 1) & (kj - qi <= BLOCK)
    q_low_half = lax.broadcasted_iota(jnp.int32, (BLOCK, LANES), 1) < HEAD_DIM
    pieces_done = 0
    for blk in range(nblk):
        first_key_slot = (s * nblk + blk - 1) * BLOCK
        valid = band & (kj + first_key_slot >= META_PAD)
        r0 = blk * BLOCK
        for kv in range(N_KV_HEADS):
            heads = [kv * Q_PER_KV + g for g in range(Q_PER_KV)]
            q_rows = []
            for hq in heads:
                grp = q_bf[r0:r0 + BLOCK, (hq // 2) * LANES:(hq // 2 + 1) * LANES]
                own = q_low_half if hq % 2 == 0 else jnp.logical_not(q_low_half)
                q_rows.append(jnp.where(own, grp, jnp.zeros_like(grp)))
            st = lax.dot_general(k_keys[kv][r0:r0 + 2 * BLOCK, :], jnp.concatenate(q_rows, axis=0),
                                 (((1,), (1,)), ((), ())), preferred_element_type=F32)
            gate_val = gate_piece_matmul(pieces_done) if pieces_done < n_gate_pieces else None
            for issue in side_work.get(blk * N_KV_HEADS + kv, ()):
                issue()
            st = jnp.where(valid, st, -jnp.inf)
            sink_row = jnp.concatenate([jnp.full((1, BLOCK), sink_ref[hq] * LOG2_E, F32) for hq in heads], axis=1)
            m = jnp.maximum(jnp.max(st, axis=0, keepdims=True), sink_row)
            p = jnp.exp2(st - m)
            den = jnp.sum(p, axis=0, keepdims=True) + jnp.exp2(sink_row - m)
            ot = jnp.dot(vt_keys[kv * HEAD_DIM:(kv + 1) * HEAD_DIM, r0:r0 + 2 * BLOCK], p.astype(BF16),
                         preferred_element_type=F32) * (1.0 / den)
            for g, hq in enumerate(heads):
                att_t_scr[hq * HEAD_DIM:(hq + 1) * HEAD_DIM, r0:r0 + BLOCK] = ot[:, g * BLOCK:(g + 1) * BLOCK]
            if gate_val is not None:
                gate_piece_store(pieces_done, gate_val)
                pieces_done += 1
    for i in range(pieces_done, n_gate_pieces):
        gate_piece_store(i, gate_piece_matmul(i))
    for i in range(N_KV_HEADS):
        kprev_scr[i] = k_dup[i][ts - BLOCK:, :]
    vprev_scr[...] = vt_bf[:, ts - BLOCK:]

    fast_ok = jnp.max(gla["worst"]) <= GLA_FAST_MAX_DECAY

    def proj_att():
        return jnp.dot(att_t_scr[...].T.astype(BF16), w_pa_ref[...], preferred_element_type=F32)

    def gla_fast():
        pa = proj_att()
        for rows, q_dec, o_intra, ds, decay_col in gla["chunks"]:
            s0 = s_scr[...]
            o_scr[rows, :] = o_intra + jnp.dot(q_dec, s0.astype(BF16), preferred_element_type=F32)
            s_scr[...] = decay_col * s0 + ds
        return pa

    def gla_slow():
        def body(i, carry):
            rows = pl.ds(pl.multiple_of(i * SUBLANES, SUBLANES), SUBLANES)
            la8, k8, q8, v8 = la_scr[rows, :], gk_scr[rows, :], gq_scr[rows, :], gv_scr[rows, :]
            outs = []
            for r in range(SUBLANES):
                a_col = _row_to_col(jnp.exp(la8[r:r + 1]))
                k_col = _row_to_col(k8[r:r + 1])
                q_col = _row_to_col(q8[r:r + 1])
                s1 = a_col * s_scr[...] + jnp.where(state_diag, k_col * v8[r:r + 1], 0.0)
                s_scr[...] = s1
                outs.append(jnp.sum(q_col * s1, axis=0, keepdims=True))
            o_scr[rows, :] = jnp.concatenate(outs, axis=0)
            return carry
        lax.fori_loop(0, ts // SUBLANES, body, 0)

    def finish_step(x_val, pa):
        gla = _gla_gate_out(o_scr[...], gate_scr[:, 0:GV_W], gn_ref[...])
        y_scr[...] = _finish(x_val, pa, gla, gate_scr[:, GV_W:GV_W + D_MODEL], gate_scr[:, GV_W + D_MODEL:],
                             w_pb_ref[...], w_out_ref[...], alpha)

    sprev_scr[...] = s_scr[...]
    pa = gla_fast()
    h_ref[0] = _layer_norm(y_scr[...], g1_ref[...], b1_ref[...])
    finish_step(x, pa)

    @pl.when(jnp.logical_not(fast_ok))
    def _():
        x_again = load_x()
        proj_again = functools.partial(_in_proj, x_again.astype(BF16), w_in_ref)
        gq_scr[...] = proj_again(C_GQ, C_GK) * (GLA_DK ** -0.5)
        gk_scr[...] = jnp.where(live, proj_again(C_GK, C_GV), 0.0)
        gv_scr[...] = jnp.where(live, proj_again(C_GV, C_LR), 0.0)
        la_scr[...] = jnp.where(live, _log_decay(proj_again(C_LR, C_LR_END), w_a2_ref[...], b_a_ref[...]), 0.0)
        s_scr[...] = sprev_scr[...]
        gla_slow()
        finish_step(x_again, proj_att())

    @pl.when(s == n_steps - 1)
    def _():
        for h in range(GLA_HEADS):
            sfin_ref[0, h] = s_scr[h * GLA_DK:(h + 1) * GLA_DK, h * GLA_DV:(h + 1) * GLA_DV]


def _ffn_update(h, w_up_ref, w_dn_ref, col_chunk, after_first_chunk=None):
    hb = h.astype(BF16)
    acc = jnp.zeros(h.shape, F32)
    for c in range(D_FF // col_chunk):
        u = jnp.dot(hb, w_up_ref[:, c * col_chunk:(c + 1) * col_chunk], preferred_element_type=F32)
        u = jnp.maximum(u, 0.0)
        acc = acc + jnp.dot((u * u).astype(BF16), w_dn_ref[c * col_chunk:(c + 1) * col_chunk, :],
                            preferred_element_type=F32)
        if c == 0 and after_first_chunk is not None:
            after_first_chunk()
    return acc


def _ffn_kernel(*refs, alpha, col_chunk, n_total):
    h_refs, (w_up_ref, w_dn_ref, g_ref, b_ref, o_ref, y_scr) = refs[:-6], refs[-6:]

    def write_norm():
        o_ref[...] = _layer_norm(y_scr[...], g_ref[...], b_ref[...]).reshape(o_ref.shape)

    def tile():
        blocks = [r[0] if len(r.shape) == 3 else r[...] for r in h_refs]
        h = blocks[0] if len(blocks) == 1 else jnp.concatenate(blocks, axis=0)
        acc = _ffn_update(h, w_up_ref, w_dn_ref, col_chunk, after_first_chunk=write_norm)
        y_scr[...] = alpha * h + acc

    t = pl.program_id(0)

    @pl.when(t == 0)
    def _():
        y_scr[...] = jnp.zeros(y_scr.shape, F32)

    pl.when(t < n_total)(tile)
    pl.when(t == n_total)(write_norm)


def _sample_proj_kernel(x_ref, cos_ref, slo_ref, shi_ref, w_in_ref, w_a2_ref, b_a_ref,
                        qkv_ref, gv_ref, gate_ref, kvt_ref, gcol_ref):
    proj = functools.partial(_in_proj, x_ref[...].astype(BF16), w_in_ref)

    cos, slo, shi = cos_ref[...], slo_ref[...], shi_ref[...]
    k = _rope(proj(C_K, C_V), cos, slo, shi)
    v = proj(C_V, C_GQ)
    qkv_ref[:, C_Q:C_K] = _rope(proj(C_Q, C_K), cos, slo, shi)
    qkv_ref[:, C_K:C_V] = k
    qkv_ref[:, C_V:C_GQ] = v
    gv_ref[...] = proj(C_GV, C_LR)
    gate_ref[...] = proj(C_GR, W_IN_COLS)
    def store_planes(ref, i, t):
        for p, part in enumerate(_split3_bf16(t)):
            ref[N_SPLIT * i + p] = part.astype(F32).T

    store_planes(kvt_ref, 0, k)
    store_planes(kvt_ref, 1, v)
    store_planes(gcol_ref, 0, jnp.exp(_log_decay(proj(C_LR, C_LR_END), w_a2_ref[...], b_a_ref[...])))
    store_planes(gcol_ref, 1, proj(C_GQ, C_GK) * (GLA_DK ** -0.5))
    store_planes(gcol_ref, 2, proj(C_GK, C_GV))


def _sample_mix_kernel(*refs, group, n_groups):
    i = pl.program_id(0)
    pl.when(i < n_groups)(functools.partial(_sample_mix_step, *refs, group=group))

    @pl.when(i >= n_groups)
    def _():
        for ref in refs[-3:]:
            ref[...] = jnp.zeros(ref.shape, F32)


def _sample_mix_step(qkv_ref, gv_ref, kvt_ref, gcol_ref, ck_ref, cv_ref, st_ref, sink_ref, *rest, group):
    att_ref, o_ref, nk_ref, nv_ref, nst_ref = rest[-5:]
    head_row = lax.broadcasted_iota(jnp.int32, (N_Q_HEADS, ATT_W), 0)
    head_lane = lax.broadcasted_iota(jnp.int32, (N_Q_HEADS, ATT_W), 1) // HEAD_DIM
    own = head_row == head_lane
    r8 = lax.broadcasted_iota(jnp.int32, (N_Q_HEADS, KV_W), 0)
    swap = (r8 % 2) != (r8 // Q_PER_KV)
    key_i = lax.broadcasted_iota(jnp.int32, (N_Q_HEADS, BLOCK), 1)
    last_row = lax.broadcasted_iota(jnp.int32, (KV_W, BLOCK), 1) == BLOCK - 1
    sink = sink_ref[...][:, 0:1]

    qkv8 = qkv_ref[...]
    gv8 = gv_ref[...]
    q8s, scores = [], []
    for j in range(group):
        q_row = qkv8[j:j + 1, C_Q:C_K]
        qm = jnp.where(own, jnp.broadcast_to(q_row, (N_Q_HEADS, ATT_W)), 0.0)
        fold = qm[:, 0:128] + qm[:, 128:256] + qm[:, 256:384] + qm[:, 384:512]
        q8 = (jnp.where(swap, pltpu.roll(fold, HEAD_DIM, 1), fold) * (HEAD_DIM ** -0.5)).astype(BF16)
        q8s.append(q8)
        scores.append(jnp.dot(q8, ck_ref[j].reshape(KV_W, BLOCK).astype(BF16), preferred_element_type=F32))

    sel = (lax.broadcasted_iota(jnp.int32, (N_SPLIT * group, group * LANES), 0) % group
           == lax.broadcasted_iota(jnp.int32, (N_SPLIT * group, group * LANES), 1) // LANES)
    sel = jnp.where(sel, 1.0, 0.0).astype(BF16)
    spread = lambda ref, i: jnp.dot(ref[i].astype(BF16), sel, preferred_element_type=F32)
    k_cols, v_cols = spread(kvt_ref, 0), spread(kvt_ref, 1)
    a_cols, q_cols, k_gla_cols = (spread(gcol_ref, i) for i in range(3))

    probs = []
    for j in range(group):
        k_new = qkv8[j:j + 1, C_K:C_V]
        sc = jnp.where(key_i >= 1, scores[j], -jnp.inf)
        s_new = jnp.sum(q8s[j].astype(F32) * k_new.astype(BF16).astype(F32), axis=-1, keepdims=True)
        m = jnp.maximum(jnp.maximum(jnp.max(sc, axis=-1, keepdims=True), s_new), sink)
        p = jnp.exp(sc - m)
        p_new = jnp.exp(s_new - m)
        den = jnp.sum(p, axis=-1, keepdims=True) + p_new + jnp.exp(sink - m)
        probs.append((p.astype(BF16), p_new, den))

    att_rows, o_rows = [], []
    for j in range(group):
        p_bf, p_new, den = probs[j]
        v_new = qkv8[j:j + 1, C_V:C_GQ]
        o8 = (lax.dot_general(p_bf, cv_ref[j].reshape(KV_W, BLOCK).astype(BF16), (((1,), (1,)), ((), ())),
                              preferred_element_type=F32)
              + p_new.astype(BF16).astype(F32) * v_new.astype(BF16).astype(F32)) / den
        o8 = jnp.where(swap, pltpu.roll(o8, HEAD_DIM, 1), o8)
        o_wide = jnp.concatenate([o8, o8, o8, o8], axis=1)
        att_rows.append(jnp.sum(jnp.where(own, o_wide, 0.0), axis=0, keepdims=True))

    for j in range(group):
        lanes_j = slice(j * LANES, (j + 1) * LANES)
        k_old = ck_ref[j].reshape(KV_W, BLOCK)
        v_old = cv_ref[j].reshape(KV_W, BLOCK)
        nk_ref[j] = jnp.where(last_row, k_cols[:, lanes_j], pltpu.roll(k_old, BLOCK - 1, 1)).reshape(
            N_KV_HEADS, HEAD_DIM, BLOCK)
        nv_ref[j] = jnp.where(last_row, v_cols[:, lanes_j], pltpu.roll(v_old, BLOCK - 1, 1)).reshape(
            N_KV_HEADS, HEAD_DIM, BLOCK)
        a_col, q_col, k_col = a_cols[:, lanes_j], q_cols[:, lanes_j], k_gla_cols[:, lanes_j]
        v_row = gv8[j:j + 1, :]
        v_exp = jnp.concatenate([jnp.broadcast_to(v_row[:, h * GLA_DV:(h + 1) * GLA_DV], (GLA_DK, GLA_DV))
                                 for h in range(GLA_HEADS)], axis=0)
        s1 = a_col * st_ref[j].reshape(GK_W, GLA_DV) + k_col * v_exp
        nst_ref[j] = s1.reshape(GLA_HEADS, GLA_DK, GLA_DV)
        qs = q_col * s1
        o_rows.append(jnp.concatenate(
            [jnp.sum(qs[h * GLA_DK:(h + 1) * GLA_DK], axis=0, keepdims=True) for h in range(GLA_HEADS)], axis=1))
    att_ref[...] = jnp.concatenate(att_rows, axis=0)
    o_ref[...] = jnp.concatenate(o_rows, axis=0)


def _sample_tail_kernel(x_ref, att_ref, o_ref, gate_ref, gn_ref, w_pa_ref, w_pb_ref, w_out_ref, g1_ref, b1_ref,
                        w_up_ref, w_dn_ref, g2_ref, b2_ref, out_ref, *, alpha, col_chunk):
    gla = _gla_gate_out(o_ref[...], _swish(gate_ref[:, 0:GV_W]), gn_ref[...])
    pa = jnp.dot(att_ref[...].astype(BF16), w_pa_ref[...], preferred_element_type=F32)
    y = _finish(x_ref[...], pa, gla, _sigmoid(gate_ref[:, GV_W:GV_W + D_MODEL]),
                _sigmoid(gate_ref[:, GV_W + D_MODEL:]), w_pb_ref[...], w_out_ref[...], alpha)
    h = _layer_norm(y, g1_ref[...], b1_ref[...])
    acc = _ffn_update(h, w_up_ref, w_dn_ref, col_chunk)
    out_ref[...] = _layer_norm(alpha * h + acc, g2_ref[...], b2_ref[...])


def _rope_tables(pos):
    half = ROT_DIM // 2
    inv = ROPE_THETA ** (-jnp.arange(half, dtype=F32) * 2.0 / ROT_DIM)
    d = jnp.arange(LANES) % HEAD_DIM
    ang = pos.astype(F32)[:, None] * inv[d % half][None, :]
    cos, sin = jnp.cos(ang), jnp.sin(ang)
    cos_t = jnp.where(d < ROT_DIM, cos, 1.0)
    sin_lo = jnp.where(d < half, -sin, 0.0)
    sin_hi = jnp.where((d >= half) & (d < ROT_DIM), sin, 0.0)
    return cos_t, sin_lo, sin_hi


def _const_spec(shape, layer=None):
    if layer is None:
        return pl.BlockSpec(shape, lambda *_: (0,) * len(shape), pipeline_mode=pl.Buffered(1))
    return pl.BlockSpec((None,) + shape, lambda *_: (layer,) + (0,) * len(shape), pipeline_mode=pl.Buffered(1))


MIXER_STEP_ROWS = (3 * BLOCK, 2 * BLOCK, BLOCK)
FFN_TILE_ROWS = (6 * BLOCK, 4 * BLOCK, 3 * BLOCK, 2 * BLOCK, BLOCK)
FFN_COL_CHUNK = D_MODEL


def _first_divisor(total, candidates):
    for t in candidates:
        if total % t == 0:
            return t
    raise ValueError(f"{total} rows are not a multiple of {candidates[-1]}")


def kernel(x_prompt, x_sample, cache_k_win, cache_v_win, state_gla, meta_tokens, w_in, w_a2, b_a, attn_sink,
           gla_norm_g, w_proj_a, w_proj_b, w_out, ln1_g, ln1_b, w_up, w_down, ln2_g, ln2_b):
    depth = w_in.shape[0]
    bsz, seq, _ = x_prompt.shape
    nsmp, dec_seq, _ = x_sample.shape
    assert dec_seq == 1 and cache_k_win.shape[2] == BLOCK and seq % BLOCK == 0
    alpha = (2 * depth) ** 0.25
    lp = seq + BLOCK
    ts = _first_divisor(lp, MIXER_STEP_ROWS)
    n_steps = lp // ts
    rows = bsz * lp
    ffn_tile = _first_divisor(rows, FFN_TILE_ROWS)
    last_tile = _first_divisor(seq, FFN_TILE_ROWS)
    group = next(g for g in (2 * SUBLANES, SUBLANES) if nsmp % g == 0)

    w_in_r = jnp.swapaxes(w_in, 1, 2).astype(BF16)
    w_a2_p = jnp.concatenate([w_a2, jnp.zeros((depth, LANES - GLA_RANK, GK_W), w_a2.dtype)], axis=1).astype(BF16)
    w_pa, w_pb, w_o = w_proj_a.astype(BF16), w_proj_b.astype(BF16), w_out.astype(BF16)
    w_u, w_d = w_up.astype(BF16), w_down.astype(BF16)
    b_a3 = b_a.reshape(depth, 1, GK_W)
    gn3 = gla_norm_g.reshape(depth, 1, GLA_DV)
    g1, b1 = ln1_g.reshape(depth, 1, D_MODEL), ln1_b.reshape(depth, 1, D_MODEL)
    g2, b2 = ln2_g.reshape(depth, 1, D_MODEL), ln2_b.reshape(depth, 1, D_MODEL)
    sink_lanes = jnp.broadcast_to(attn_sink[:, :, None], (depth, N_Q_HEADS, LANES))

    cos_p, slo_p, shi_p = _rope_tables(jnp.arange(lp) - META_PAD)
    cos_s, slo_s, shi_s = (jnp.broadcast_to(t, (nsmp, LANES)) for t in _rope_tables(PAST_LEN + jnp.arange(1)))

    meta_block = jnp.concatenate([jnp.zeros((META_PAD, D_MODEL), x_prompt.dtype),
                                  meta_tokens.astype(x_prompt.dtype)], axis=0)
    nblk = ts // BLOCK
    xp = None
    xs = x_sample.reshape(nsmp, D_MODEL)
    ck = jnp.transpose(cache_k_win, (0, 1, 3, 4, 2))
    cv = jnp.transpose(cache_v_win, (0, 1, 3, 4, 2))

    cparams = functools.partial(pltpu.CompilerParams, vmem_limit_bytes=VMEM_LIMIT)
    pk, pv, pst = [], [], []
    stacked = []
    for l in range(depth):
        n_total = bsz * n_steps
        seq_of = lambda t: jnp.minimum(t, n_total - 1) // n_steps
        step_of = lambda t: jnp.minimum(t, n_total - 1) % n_steps
        step_spec = pl.BlockSpec((1, ts, D_MODEL), lambda t: (seq_of(t), step_of(t), 0))
        h_spec = pl.BlockSpec((1, ts, D_MODEL), lambda t: (seq_of(jnp.maximum(t - 1, 0)), step_of(jnp.maximum(t - 1, 0)), 0))
        tab_spec = pl.BlockSpec((ts, LANES), lambda t: (step_of(t), 0))
        if l == 0:
            x_specs = [_const_spec((BLOCK, D_MODEL))] + [
                pl.BlockSpec((1, BLOCK, D_MODEL),
                             lambda t, j=j: (seq_of(t), jnp.maximum(nblk * step_of(t) + j - 1, 0), 0))
                for j in range(nblk)]
            x_args = [meta_block] + [x_prompt] * nblk
        else:
            x_specs, x_args = [step_spec], [xp]
        hp, kwin, vwin, sfin = pl.pallas_call(
            functools.partial(_mixer_prompt_kernel, ts=ts, n_steps=n_steps, n_total=n_total, alpha=alpha,
                              from_tokens=(l == 0)),
            grid=(n_total + 1,),
            in_specs=x_specs + [tab_spec, tab_spec, tab_spec,
                      _const_spec((W_IN_COLS, D_MODEL), l), _const_spec((LANES, GK_W), l), _const_spec((1, GK_W), l),
                      pl.BlockSpec(memory_space=pltpu.SMEM), _const_spec((1, GLA_DV), l),
                      _const_spec((ATT_W, D_MODEL), l), _const_spec((GV_W, D_MODEL), l),
                      _const_spec((D_MODEL, D_MODEL), l), _const_spec((1, D_MODEL), l), _const_spec((1, D_MODEL), l)],
            out_specs=[h_spec,
                       pl.BlockSpec((1, BLOCK, KV_W), lambda t: (seq_of(t), 0, 0)),
                       pl.BlockSpec((1, BLOCK, KV_W), lambda t: (seq_of(t), 0, 0)),
                       pl.BlockSpec((1, GLA_HEADS, GLA_DK, GLA_DV), lambda t: (seq_of(t), 0, 0, 0))],
            out_shape=[jax.ShapeDtypeStruct((bsz, lp, D_MODEL), F32),
                       jax.ShapeDtypeStruct((bsz, BLOCK, KV_W), F32),
                       jax.ShapeDtypeStruct((bsz, BLOCK, KV_W), F32),
                       jax.ShapeDtypeStruct((bsz, GLA_HEADS, GLA_DK, GLA_DV), F32)],
            scratch_shapes=[pltpu.VMEM((N_KV_HEADS, BLOCK, KV_W), BF16), pltpu.VMEM((KV_W, BLOCK), BF16),
                            pltpu.VMEM((GK_W, GV_W), F32), pltpu.VMEM((GK_W, GV_W), F32),
                            pltpu.VMEM((ts, GK_W), F32), pltpu.VMEM((ts, GK_W), F32), pltpu.VMEM((ts, GV_W), F32),
                            pltpu.VMEM((ts, GK_W), F32), pltpu.VMEM((ts, GV_W), F32), pltpu.VMEM((ATT_W, ts), F32),
                            pltpu.VMEM((ts, W_IN_COLS - C_GR), F32), pltpu.VMEM((ts, D_MODEL), F32)],
            compiler_params=cparams(dimension_semantics=("arbitrary",)),
            name=f"mixer_prompt_{l}",
        )(*x_args, cos_p, slo_p, shi_p, w_in_r, w_a2_p, b_a3, attn_sink[l], gn3, w_pa, w_pb, w_o, g1, b1)
        pk.append(kwin.reshape(bsz, N_KV_HEADS, HEAD_DIM, BLOCK))
        pv.append(vwin.reshape(bsz, N_KV_HEADS, HEAD_DIM, BLOCK))
        pst.append(sfin)

        ffn = functools.partial(_ffn_kernel, alpha=alpha, col_chunk=FFN_COL_CHUNK)
        ffn_w = [_const_spec((D_MODEL, D_FF), l), _const_spec((D_FF, D_MODEL), l),
                 _const_spec((1, D_MODEL), l), _const_spec((1, D_MODEL), l)]
        if l < depth - 1:
            n_tiles = rows // ffn_tile
            xp = pl.pallas_call(
                functools.partial(ffn, n_total=n_tiles), grid=(n_tiles + 1,),
                in_specs=[pl.BlockSpec((ffn_tile, D_MODEL), lambda t: (jnp.minimum(t, n_tiles - 1), 0))] + ffn_w,
                out_specs=pl.BlockSpec((ffn_tile, D_MODEL), lambda t: (jnp.maximum(t - 1, 0), 0)),
                out_shape=jax.ShapeDtypeStruct((rows, D_MODEL), F32),
                scratch_shapes=[pltpu.VMEM((ffn_tile, D_MODEL), F32)],
                compiler_params=cparams(dimension_semantics=("arbitrary",)),
                name=f"ffn_prompt_{l}",
            )(hp.reshape(rows, D_MODEL), w_u, w_d, g2, b2).reshape(bsz, lp, D_MODEL)
        else:
            pieces = last_tile // BLOCK
            per_seq = seq // last_tile
            n_tiles = bsz * per_seq
            rd = lambda t: jnp.minimum(t, n_tiles - 1)
            wr = lambda t: jnp.maximum(t - 1, 0)
            y_prompt = pl.pallas_call(
                functools.partial(ffn, n_total=n_tiles), grid=(n_tiles + 1,),
                in_specs=[pl.BlockSpec((1, BLOCK, D_MODEL),
                                       lambda t, j=j: (rd(t) // per_seq, 1 + pieces * (rd(t) % per_seq) + j, 0))
                          for j in range(pieces)] + ffn_w,
                out_specs=pl.BlockSpec((1, last_tile, D_MODEL), lambda t: (wr(t) // per_seq, wr(t) % per_seq, 0)),
                out_shape=jax.ShapeDtypeStruct((bsz, seq, D_MODEL), F32),
                scratch_shapes=[pltpu.VMEM((last_tile, D_MODEL), F32)],
                compiler_params=cparams(dimension_semantics=("arbitrary",)),
                name=f"ffn_prompt_{l}",
            )(*([hp] * pieces), w_u, w_d, g2, b2)

        proj_out = [(nsmp, C_GQ), (nsmp, GV_W), (nsmp, W_IN_COLS - C_GR),
                    (2 * N_SPLIT, KV_W, nsmp), (3 * N_SPLIT, GK_W, nsmp)]
        qkv, gv_s, gate, kvt, gcol = pl.pallas_call(
            _sample_proj_kernel, grid=(1,),
            in_specs=[_const_spec((nsmp, D_MODEL)), _const_spec((nsmp, LANES)), _const_spec((nsmp, LANES)),
                      _const_spec((nsmp, LANES)), _const_spec((W_IN_COLS, D_MODEL), l),
                      _const_spec((LANES, GK_W), l), _const_spec((1, GK_W), l)],
            out_specs=[_const_spec(s) for s in proj_out],
            out_shape=[jax.ShapeDtypeStruct(s, F32) for s in proj_out],
            compiler_params=cparams(dimension_semantics=("arbitrary",)),
            name=f"sample_proj_{l}",
        )(xs, cos_s, slo_s, shi_s, w_in_r, w_a2_p, b_a3)
        def by_group(t):
            t = t.reshape(t.shape[0] // N_SPLIT, N_SPLIT, t.shape[1], nsmp // group, group)
            return jnp.transpose(t, (3, 0, 2, 1, 4)).reshape(nsmp // group, t.shape[0], t.shape[2], N_SPLIT * group)
        n_groups = nsmp // group
        mix_steps = n_groups * depth if l == 0 else n_groups
        g_of = lambda i: jnp.minimum(i, n_groups - 1)
        col_spec = lambda n, width: pl.BlockSpec((None, n, width, N_SPLIT * group), lambda i: (g_of(i), 0, 0, 0))

        grp = lambda width: pl.BlockSpec((group, width), lambda i: (g_of(i), 0))
        cache_spec = pl.BlockSpec((None, group, N_KV_HEADS, HEAD_DIM, BLOCK), lambda i: (l, g_of(i), 0, 0, 0))
        state_spec = pl.BlockSpec((None, group, GLA_HEADS, GLA_DK, GLA_DV), lambda i: (l, g_of(i), 0, 0, 0))
        cache_out = pl.BlockSpec((None, group, N_KV_HEADS, HEAD_DIM, BLOCK),
                                 lambda i: (l + i // n_groups, i % n_groups, 0, 0, 0))
        state_out = pl.BlockSpec((None, group, GLA_HEADS, GLA_DK, GLA_DV),
                                 lambda i: (l + i // n_groups, i % n_groups, 0, 0, 0))
        n_mix_in = 8
        att_s, o_s, *stacked = pl.pallas_call(
            functools.partial(_sample_mix_kernel, group=group, n_groups=n_groups), grid=(mix_steps,),
            in_specs=[grp(C_GQ), grp(GV_W), col_spec(2, KV_W), col_spec(3, GK_W), cache_spec, cache_spec, state_spec,
                      _const_spec((N_Q_HEADS, LANES), l)] + [pl.BlockSpec(memory_space=pl.ANY)] * len(stacked),
            out_specs=[grp(ATT_W), grp(GV_W), cache_out, cache_out, state_out],
            out_shape=[jax.ShapeDtypeStruct((nsmp, ATT_W), F32), jax.ShapeDtypeStruct((nsmp, GV_W), F32),
                       jax.ShapeDtypeStruct(ck.shape, F32), jax.ShapeDtypeStruct(cv.shape, F32),
                       jax.ShapeDtypeStruct(state_gla.shape, F32)],
            input_output_aliases={n_mix_in + i: 2 + i for i in range(len(stacked))},
            compiler_params=cparams(dimension_semantics=("arbitrary",)),
            name=f"sample_mix_{l}",
        )(qkv, gv_s, by_group(kvt), by_group(gcol), ck, cv, state_gla, sink_lanes, *stacked)

        xs = pl.pallas_call(
            functools.partial(_sample_tail_kernel, alpha=alpha, col_chunk=FFN_COL_CHUNK), grid=(1,),
            in_specs=[_const_spec((nsmp, D_MODEL)), _const_spec((nsmp, ATT_W)), _const_spec((nsmp, GV_W)),
                      _const_spec((nsmp, W_IN_COLS - C_GR)), _const_spec((1, GLA_DV), l),
                      _const_spec((ATT_W, D_MODEL), l), _const_spec((GV_W, D_MODEL), l),
                      _const_spec((D_MODEL, D_MODEL), l), _const_spec((1, D_MODEL), l), _const_spec((1, D_MODEL), l)]
            + ffn_w,
            out_specs=_const_spec((nsmp, D_MODEL)),
            out_shape=jax.ShapeDtypeStruct((nsmp, D_MODEL), F32),
            compiler_params=cparams(dimension_semantics=("arbitrary",)),
            name=f"sample_tail_{l}",
        )(xs, att_s, o_s, gate, gn3, w_pa, w_pb, w_o, g1, b1, w_u, w_d, g2, b2)

    y_sample = xs.reshape(nsmp, 1, D_MODEL)
    to_rows = lambda t: jnp.transpose(t, (0, 1, 4, 2, 3))
    return (y_prompt, y_sample, to_rows(jnp.stack(pk)), to_rows(jnp.stack(pv)), jnp.stack(pst),
            to_rows(stacked[0]), to_rows(stacked[1]), stacked[2])
```
